```python
import jax
import jax.numpy as jnp
from jax import lax
import numpy as np

D_MODEL = 1024
BATCH = 16
SEQ = 256
DEPTH = 1
DEC_BATCH = 8
DEC_SEQ = 1024
PAST_LEN = 256

GRID_W = 64
HEAD_DIM = 64
D_RWKV = D_MODEL // 2
H_RWKV = D_RWKV // HEAD_DIM
D_ATTN = D_MODEL - D_RWKV
H_Q = D_ATTN // HEAD_DIM
H_KV = 2
GQA_GROUP = H_Q // H_KV
D_KV = H_KV * HEAD_DIM
LORA_W = 64
LORA_A = 64
LORA_G = 128
N_DIR = 2
D_FF = 2816
CONV_W = 3
Q_BLOCK = 128
ROPE_THETA = 10000.0
RMS_EPS = 1e-6
GN_EPS = 64e-5
IN_WIDTHS = (D_RWKV, D_RWKV, D_RWKV, LORA_W, LORA_A, LORA_G, D_ATTN, D_KV, D_KV)
D_IN = 3 * D_RWKV + LORA_W + LORA_A + LORA_G + D_ATTN + 2 * D_KV

kernel_name = 'hybrid_rwkv7_gqa_dit_step'


def rms_norm(x, g):
    xf = x.astype(jnp.float32)
    y = xf * lax.rsqrt(jnp.mean(xf * xf, axis=-1, keepdims=True) + RMS_EPS)
    return (y * g.astype(jnp.float32)).astype(x.dtype)


def ada_ln(cvec, w_mod, b_mod):
    m = jax.nn.silu(cvec) @ w_mod + b_mod
    return [t[:, None, :] for t in jnp.split(m, 6, axis=-1)]


def modulate(h, shift, scale):
    return h * (1.0 + scale) + shift


def heads(t):
    return t.reshape(t.shape[0], t.shape[1], -1, HEAD_DIM)


def split_projection(h, w_in):
    bounds = [int(b) for b in np.cumsum(IN_WIDTHS)[:-1]]
    return jnp.split(h @ w_in, bounds, axis=-1)


def grid_rope(x):
    n_tok = x.shape[1]
    n_rows = n_tok // GRID_W
    row = jnp.repeat(jnp.arange(n_rows), GRID_W).astype(jnp.float32)
    col = jnp.tile(jnp.arange(GRID_W), n_rows).astype(jnp.float32)
    half = HEAD_DIM // 2
    quarter = half // 2
    inv_freq = ROPE_THETA ** (-jnp.arange(quarter, dtype=jnp.float32) / quarter)

    def rotate(xp, pos):
        ang = pos[:, None] * inv_freq[None, :]
        cos = jnp.cos(ang)[None, :, None, :]
        sin = jnp.sin(ang)[None, :, None, :]
        x1, x2 = xp[..., :quarter], xp[..., quarter:]
        return jnp.concatenate([x1 * cos - x2 * sin, x1 * sin + x2 * cos], axis=-1)

    xf = x.astype(jnp.float32)
    out = jnp.concatenate([rotate(xf[..., :half], row), rotate(xf[..., half:], col)], axis=-1)
    return out.astype(x.dtype)


def block_attention(q, k, v):
    b, lq = q.shape[0], q.shape[1]
    n_blk = lq // Q_BLOCK
    qb = q.reshape(b, n_blk, Q_BLOCK, H_KV, GQA_GROUP, HEAD_DIM).transpose(1, 0, 2, 3, 4, 5)
    kf = k.astype(jnp.float32)
    vf = v.astype(jnp.float32)
    scale = HEAD_DIM ** -0.5

    def one_block(q_blk):
        s = jnp.einsum('bqkgd,bskd->bkgqs', q_blk.astype(jnp.float32), kf) * scale
        p = jax.nn.softmax(s, axis=-1)
        return jnp.einsum('bkgqs,bskd->bqkgd', p, vf).astype(q.dtype)

    o = lax.map(one_block, qb)
    return o.transpose(1, 0, 2, 3, 4, 5).reshape(b, lq, H_Q * HEAD_DIM)


def wkv_scan(r, w, k, v, kk, a, s0, reverse):
    def step(s, inp):
        r_t, w_t, k_t, v_t, kk_t, a_t = inp
        sa = jnp.einsum('bhvk,bhk->bhv', s, -kk_t)
        s = (s * w_t[:, :, None, :]
             + sa[..., None] * (kk_t * a_t)[:, :, None, :]
             + v_t[..., None] * k_t[:, :, None, :])
        return s, jnp.einsum('bhvk,bhk->bhv', s, r_t)

    xs = tuple(jnp.moveaxis(t, 1, 0) for t in (r, w, k, v, kk, a))
    s_fin, ys = lax.scan(step, s0, xs, reverse=reverse)
    return jnp.moveaxis(ys, 0, 1), s_fin


def rwkv_mixer(r, k, v, xw, xa, xg, s0, lp):
    f32 = jnp.float32
    out_dtype = r.dtype
    r, k, v, xw, xa, xg = (t.astype(f32) for t in (r, k, v, xw, xa, xg))
    g = jax.nn.sigmoid(xg) @ lp['g_up'].astype(f32)
    kk = heads(k * lp['k_k'].astype(f32))
    kk = kk * lax.rsqrt(jnp.sum(kk * kk, axis=-1, keepdims=True) + 1e-12)
    tw = jnp.tanh(xw)
    rh, vh = heads(r), heads(v)
    r_k = lp['r_k'].astype(f32)
    y_sum = jnp.zeros_like(rh)
    bonus = jnp.zeros_like(rh)
    finals = []
    for d in range(N_DIR):
        logw = -jax.nn.softplus(-(lp['w0'][d].astype(f32) + tw @ lp['w_up'][d].astype(f32))) - 0.5
        decay = jnp.exp(-jnp.exp(logw))
        a = jax.nn.sigmoid(lp['a0'][d].astype(f32) + xa @ lp['a_up'][d].astype(f32))
        kd = heads(k * (1.0 + (a - 1.0) * lp['k_a'].astype(f32)))
        y_d, s_d = wkv_scan(rh, heads(decay), kd, vh, kk, heads(a),
                            s0[:, d].astype(f32), reverse=(d == 1))
        y_sum = y_sum + y_d
        bonus = bonus + jnp.sum(rh * kd * r_k, axis=-1, keepdims=True) * vh
        finals.append(s_d)
    mu = jnp.mean(y_sum, axis=-1, keepdims=True)
    var = jnp.mean(jnp.square(y_sum - mu), axis=-1, keepdims=True)
    yn = ((y_sum - mu) * lax.rsqrt(var + GN_EPS)).reshape(r.shape)
    yn = yn * lp['gn_w'].astype(f32) + lp['gn_b'].astype(f32)
    y = (yn + bonus.reshape(r.shape)) * g
    return y.astype(out_dtype), jnp.stack(finals, axis=1)


def conv_ffn(h, lp):
    u = h @ lp['ffn_up']
    n_tok = u.shape[1]
    pad = CONV_W // 2
    up = jnp.pad(u, ((0, 0), (pad, pad), (0, 0)))
    conv = lp['conv_b'] + up[:, 0:n_tok] * lp['conv_w'][0]
    for j in range(1, CONV_W):
        conv = conv + up[:, j:j + n_tok] * lp['conv_w'][j]
    gate, val = jnp.split(conv, 2, axis=-1)
    return (jax.nn.silu(gate) * val) @ lp['ffn_down']


def trunk_layer(x, mod, lp, ctx=None):
    shift1, scale1, gate1, shift2, scale2, gate2 = mod
    b = x.shape[0]
    h = modulate(rms_norm(x, lp['norm_mix_pre']), shift1, scale1)
    r, k, v, xw, xa, xg, q, ka, va = split_projection(h, lp['w_in'])
    q = rms_norm(heads(q), lp['q_norm'])
    ka = rms_norm(heads(ka), lp['k_norm'])
    va = heads(va)
    if ctx is None:
        s0 = jnp.zeros((b, N_DIR, H_RWKV, HEAD_DIM, HEAD_DIM), jnp.float32)
        y_attn = block_attention(q, ka, va)
    else:
        k_ctx, v_ctx, s0 = ctx
        k_all = jnp.concatenate([grid_rope(ka), k_ctx.astype(ka.dtype)], axis=1)
        v_all = jnp.concatenate([va, v_ctx.astype(va.dtype)], axis=1)
        y_attn = block_attention(grid_rope(q), k_all, v_all)
    y_rwkv, s_fin = rwkv_mixer(r, k, v, xw, xa, xg, s0, lp)
    mix = jnp.concatenate([y_rwkv, y_attn], axis=-1) @ lp['w_out']
    x = x + gate1 * rms_norm(mix, lp['norm_mix_post'])
    h2 = modulate(rms_norm(x, lp['norm_ffn_pre']), shift2, scale2)
    x = x + gate2 * rms_norm(conv_ffn(h2, lp), lp['norm_ffn_post'])
    return x, ka, va, s_fin


def setup_inputs(seed: int = 0) -> dict:
    key = jax.random.key(seed)
    ks = jax.random.split(key, 32)

    def nrm(k, shape, scale):
        return scale * jax.random.normal(k, shape, jnp.float32)

    return {
        'x_prompt': nrm(ks[0], (BATCH, SEQ, D_MODEL), 1.0),
        'x_sample': nrm(ks[1], (DEC_BATCH, DEC_SEQ, D_MODEL), 1.0),
        'cache_k': nrm(ks[2], (DEC_BATCH, DEPTH, PAST_LEN, H_KV, HEAD_DIM), 1.0),
        'cache_v': nrm(ks[3], (DEC_BATCH, DEPTH, PAST_LEN, H_KV, HEAD_DIM), 0.5),
        'state_rwkv': nrm(ks[4], (DEC_BATCH, DEPTH, N_DIR, H_RWKV, HEAD_DIM, HEAD_DIM), 0.3),
        'c': nrm(ks[5], (DEC_BATCH, D_MODEL), 1.0),
        'c_ctx': nrm(ks[6], (D_MODEL,), 1.0),
        'w_mod': nrm(ks[7], (DEPTH, D_MODEL, 6 * D_MODEL), 0.5 * D_MODEL ** -0.5),
        'b_mod': nrm(ks[8], (DEPTH, 6 * D_MODEL), 0.1),
        'norm_mix_pre': 1.0 + nrm(ks[9], (DEPTH, D_MODEL), 0.05),
        'norm_mix_post': 1.0 + nrm(ks[10], (DEPTH, D_MODEL), 0.05),
        'norm_ffn_pre': 1.0 + nrm(ks[11], (DEPTH, D_MODEL), 0.05),
        'norm_ffn_post': 1.0 + nrm(ks[12], (DEPTH, D_MODEL), 0.05),
        'w_in': nrm(ks[13], (DEPTH, D_MODEL, D_IN), D_MODEL ** -0.5),
        'w0': 0.5 + nrm(ks[14], (DEPTH, N_DIR, D_RWKV), 0.5),
        'w_up': nrm(ks[15], (DEPTH, N_DIR, LORA_W, D_RWKV), 0.3 * LORA_W ** -0.5),
        'a0': nrm(ks[16], (DEPTH, N_DIR, D_RWKV), 0.3),
        'a_up': nrm(ks[17], (DEPTH, N_DIR, LORA_A, D_RWKV), 0.3 * LORA_A ** -0.5),
        'g_up': nrm(ks[18], (DEPTH, LORA_G, D_RWKV), LORA_G ** -0.5),
        'k_k': 0.85 + nrm(ks[19], (DEPTH, D_RWKV), 0.05),
        'k_a': 1.0 + nrm(ks[20], (DEPTH, D_RWKV), 0.05),
        'r_k': nrm(ks[21], (DEPTH, H_RWKV, HEAD_DIM), 0.1),
        'gn_w': 1.0 + nrm(ks[22], (DEPTH, D_RWKV), 0.05),
        'gn_b': nrm(ks[23], (DEPTH, D_RWKV), 0.02),
        'q_norm': 1.0 + nrm(ks[24], (DEPTH, HEAD_DIM), 0.05),
        'k_norm': 1.0 + nrm(ks[25], (DEPTH, HEAD_DIM), 0.05),
        'w_out': nrm(ks[26], (DEPTH, D_MODEL, D_MODEL), D_MODEL ** -0.5),
        'ffn_up': nrm(ks[27], (DEPTH, D_MODEL, 2 * D_FF), D_MODEL ** -0.5),
        'conv_w': nrm(ks[28], (DEPTH, CONV_W, 2 * D_FF), CONV_W ** -0.5),
        'conv_b': nrm(ks[29], (DEPTH, 2 * D_FF), 0.02),
        'ffn_down': nrm(ks[30], (DEPTH, D_FF, D_MODEL), D_FF ** -0.5),
    }


def reference(x_prompt, x_sample, cache_k, cache_v, state_rwkv, c, c_ctx,
              w_mod, b_mod, norm_mix_pre, norm_mix_post, norm_ffn_pre, norm_ffn_post,
              w_in, w0, w_up, a0, a_up, g_up, k_k, k_a, r_k, gn_w, gn_b,
              q_norm, k_norm, w_out, ffn_up, conv_w, conv_b, ffn_down):
    y_prompt = x_prompt
    y_sample = x_sample
    new_k, new_v, new_s = [], [], []
    for layer in range(DEPTH):
        lp = {
            'norm_mix_pre': norm_mix_pre[layer], 'norm_mix_post': norm_mix_post[layer],
            'norm_ffn_pre': norm_ffn_pre[layer], 'norm_ffn_post': norm_ffn_post[layer],
            'w_in': w_in[layer], 'w0': w0[layer], 'w_up': w_up[layer],
            'a0': a0[layer], 'a_up': a_up[layer], 'g_up': g_up[layer],
            'k_k': k_k[layer], 'k_a': k_a[layer], 'r_k': r_k[layer],
            'gn_w': gn_w[layer], 'gn_b': gn_b[layer],
            'q_norm': q_norm[layer], 'k_norm': k_norm[layer], 'w_out': w_out[layer],
            'ffn_up': ffn_up[layer], 'conv_w': conv_w[layer], 'conv_b': conv_b[layer],
            'ffn_down': ffn_down[layer],
        }
        mod_ctx = ada_ln(c_ctx[None, :], w_mod[layer], b_mod[layer])
        mod_lat = ada_ln(c, w_mod[layer], b_mod[layer])
        y_prompt, k_l, v_l, s_l = trunk_layer(y_prompt, mod_ctx, lp)
        new_k.append(k_l)
        new_v.append(v_l)
        new_s.append(s_l.astype(x_prompt.dtype))
        y_sample, _, _, _ = trunk_layer(
            y_sample, mod_lat, lp,
            ctx=(cache_k[:, layer], cache_v[:, layer], state_rwkv[:, layer]))
    new_cache_k = jnp.stack(new_k, axis=1)
    new_cache_v = jnp.stack(new_v, axis=1)
    new_state_rwkv = jnp.stack(new_s, axis=1)
    return (y_prompt, y_sample, new_cache_k, new_cache_v, new_state_rwkv)
```

```python
import functools

import numpy as np
import jax
import jax.numpy as jnp
from jax import lax
from jax.experimental import pallas as pl
from jax.experimental.pallas import tpu as pltpu

F32 = jnp.float32
BF16 = jnp.bfloat16

D_MODEL = 1024
HEAD_DIM = 64
D_RWKV = 512
H_RWKV = 8
D_ATTN = 512
H_Q = 8
H_KV = 2
D_KV = 128
LORA_W = 64
LORA_A = 64
LORA_G = 128
D_FF = 2816
D_IN = 2560
GRID_W = 64
ROPE_THETA = 10000.0
RMS_EPS = 1e-6
GN_EPS = 64e-5

LANES = 128
TOK_TILE = 256
FFN_TOK_TILE = 1024
FFN_COL_TILE = 256
SCAN_STEPS = 16
ATTN_Q_TILE = 256
MOD_COL_TILE = 768
VMEM_LIMIT = 56 * 1024 * 1024


def _dot(a, b):
    return jnp.dot(a, b, preferred_element_type=F32)


def _split_bf16(x):
    hi = x.astype(BF16)
    lo = (x - hi.astype(F32)).astype(BF16)
    return hi, lo


def _dot_exact_rhs(x, w_bf16):
    hi, lo = _split_bf16(x)
    return _dot(hi, w_bf16) + _dot(lo, w_bf16)


def _dot3(a, b):
    ah, al = _split_bf16(a)
    bh, bl = _split_bf16(b)
    return _dot(ah, bh) + (_dot(ah, bl) + _dot(al, bh))


def _head_ones(n):
    shift = HEAD_DIM.bit_length() - 1
    r = lax.shift_right_logical(lax.broadcasted_iota(jnp.int32, (n, n), 0), shift)
    c = lax.shift_right_logical(lax.broadcasted_iota(jnp.int32, (n, n), 1), shift)
    return jnp.where(r == c, 1.0, 0.0).astype(BF16)


def _sigmoid(x):
    return 1.0 / (1.0 + jnp.exp(-x))


def _softplus(x):
    return jnp.maximum(x, 0.0) + jnp.log(1.0 + jnp.exp(-jnp.abs(x)))


def _rms(x, g):
    return x * lax.rsqrt(jnp.mean(x * x, axis=-1, keepdims=True) + RMS_EPS) * g


def _rope(x, cos, sgn_sin):
    n = x.shape[-1]
    lane = lax.broadcasted_iota(jnp.int32, x.shape, 1)
    partner = jnp.where((lane & 16) == 0, pltpu.roll(x, n - 16, 1), pltpu.roll(x, 16, 1))
    return x * cos + partner * sgn_sin


def _mod_kernel(c_ref, w_ref, b_ref, o_ref):
    c = c_ref[...]
    o_ref[...] = _dot3(c * _sigmoid(c), w_ref[...]) + b_ref[...]


def _mod_call(cvec, w_mod, b_mod):
    n = w_mod.shape[1]
    rows = cvec.shape[0]
    return pl.pallas_call(
        _mod_kernel,
        grid=(n // MOD_COL_TILE,),
        in_specs=[
            pl.BlockSpec((rows, D_MODEL), lambda j: (0, 0)),
            pl.BlockSpec((D_MODEL, MOD_COL_TILE), lambda j: (0, j)),
            pl.BlockSpec((1, MOD_COL_TILE), lambda j: (0, j)),
        ],
        out_specs=pl.BlockSpec((rows, MOD_COL_TILE), lambda j: (0, j)),
        out_shape=jax.ShapeDtypeStruct((rows, n), F32),
        compiler_params=pltpu.CompilerParams(dimension_semantics=("arbitrary",)),
        name="adaln_mod",
    )(cvec, w_mod, b_mod.reshape(1, n))


_V_W0F, _V_W0B, _V_A0F, _V_A0B, _V_KK, _V_KA, _V_RK, _V_QN = range(8)


def _pre_kernel(*refs, rope):
    if rope:
        (x_ref, mod_ref, gpre_ref, win_ref, wlora_ref, gup_ref, vec_ref, kn_ref,
         cq_ref, sq_ref, ck_ref, sk_ref,
         r_o, v_o, kk_o, w0_o, w1_o, b0_o, b1_o, kd0_o, kd1_o, g_o, bonus_o, q_o, ka_o, va_o) = refs
    else:
        (x_ref, mod_ref, gpre_ref, win_ref, wlora_ref, gup_ref, vec_ref, kn_ref,
         r_o, v_o, kk_o, w0_o, w1_o, b0_o, b1_o, kd0_o, kd1_o, g_o, bonus_o, q_o, ka_o, va_o) = refs

    x = x_ref[...]
    m = mod_ref[0]
    shift1 = m[:, 0:D_MODEL]
    scale1 = m[:, D_MODEL:2 * D_MODEL]
    h = _rms(x, gpre_ref[...]) * (1.0 + scale1) + shift1
    p = _dot(h.astype(BF16), win_ref[...])

    r = p[:, 0:512]
    k = p[:, 512:1024]
    v = p[:, 1024:1536]
    xwa = p[:, 1536:1664]
    xg = p[:, 1664:1792]
    q = p[:, 1792:2304]
    ka = p[:, 2304:2432]
    va = p[:, 2432:2560]

    vec = vec_ref[...]
    ones512 = _head_ones(D_RWKV)

    g_o[...] = _dot(_sigmoid(xg).astype(BF16), gup_ref[...])

    kx = k * vec[_V_KK:_V_KK + 1]
    kk = kx * lax.rsqrt(_dot_exact_rhs(kx * kx, ones512) + 1e-12)

    lane = lax.broadcasted_iota(jnp.int32, xwa.shape, 1)
    lora_in = jnp.where(lane < LORA_W, jnp.tanh(xwa), xwa)
    lora = _dot3(lora_in, wlora_ref[...])

    k_a = vec[_V_KA:_V_KA + 1]
    kd_sum = None
    for d, (w_o, b_o, kd_o) in enumerate(((w0_o, b0_o, kd0_o), (w1_o, b1_o, kd1_o))):
        z = vec[_V_W0F + d:_V_W0F + d + 1] + lora[:, 512 * d:512 * (d + 1)]
        logw = -_softplus(-z) - 0.5
        w_o[...] = jnp.exp(-jnp.exp(logw))
        a = _sigmoid(vec[_V_A0F + d:_V_A0F + d + 1] + lora[:, 1024 + 512 * d:1024 + 512 * (d + 1)])
        kd = k * (1.0 + (a - 1.0) * k_a)
        kd_o[...] = kd
        b_o[...] = kk * a
        kd_sum = kd if kd_sum is None else kd_sum + kd

    bonus_o[...] = _dot_exact_rhs(r * kd_sum * vec[_V_RK:_V_RK + 1], ones512) * v

    r_o[...] = r
    v_o[...] = v
    kk_o[...] = kk

    qn = q * lax.rsqrt(_dot_exact_rhs(q * q, ones512) * (1.0 / HEAD_DIM) + RMS_EPS) * vec[_V_QN:_V_QN + 1]
    kan = ka * lax.rsqrt(_dot_exact_rhs(ka * ka, _head_ones(D_KV)) * (1.0 / HEAD_DIM) + RMS_EPS) * kn_ref[...]
    if rope:
        qn = _rope(qn, cq_ref[...], sq_ref[...])
        kan = _rope(kan, ck_ref[...], sk_ref[...])
    q_o[...] = qn
    ka_o[...] = kan
    va_o[...] = va


def _pre_call(x2d, mod3d, seq_len, wts, rope_tabs):
    n_tok = x2d.shape[0]
    tiles_per_seq = seq_len // TOK_TILE
    n_mod = mod3d.shape[0]
    rope = rope_tabs is not None

    def const(shape):
        return pl.BlockSpec(shape, lambda i: tuple(0 for _ in shape))

    mod_map = (lambda i: (i // tiles_per_seq, 0, 0)) if n_mod > 1 else (lambda i: (0, 0, 0))
    in_specs = [
        pl.BlockSpec((TOK_TILE, D_MODEL), lambda i: (i, 0)),
        pl.BlockSpec((1, 1, 6 * D_MODEL), mod_map),
        const((1, D_MODEL)),
        const((D_MODEL, D_IN)),
        const((LORA_W + LORA_A, 4 * D_RWKV)),
        const((LORA_G, D_RWKV)),
        const((8, D_RWKV)),
        const((1, D_KV)),
    ]
    args = [x2d, mod3d, wts["g_pre"], wts["w_in"], wts["w_lora"], wts["g_up"], wts["vec"], wts["k_norm"]]
    if rope:
        tab_map = lambda i: (i % tiles_per_seq, 0)
        in_specs += [pl.BlockSpec((TOK_TILE, D_ATTN), tab_map), pl.BlockSpec((TOK_TILE, D_ATTN), tab_map),
                     pl.BlockSpec((TOK_TILE, D_KV), tab_map), pl.BlockSpec((TOK_TILE, D_KV), tab_map)]
        args += list(rope_tabs)

    widths = [512] * 12 + [D_KV, D_KV]
    out_specs = [pl.BlockSpec((TOK_TILE, wd), lambda i: (i, 0)) for wd in widths]
    out_shape = [jax.ShapeDtypeStruct((n_tok, wd), F32) for wd in widths]
    return pl.pallas_call(
        functools.partial(_pre_kernel, rope=rope),
        grid=(n_tok // TOK_TILE,),
        in_specs=in_specs,
        out_specs=out_specs,
        out_shape=out_shape,
        compiler_params=pltpu.CompilerParams(dimension_semantics=("arbitrary",),
                                             vmem_limit_bytes=VMEM_LIMIT),
        name="pre_rope" if rope else "pre",
    )(*args)


def _scan_kernel(*refs, has_init):
    if has_init:
        (w_ref, b_ref, kd_ref, r_ref, kkn_ref, v_ref, kk0_ref, s0_ref,
         y_ref, sfin_ref, s_scr, sa_scr) = refs
    else:
        (w_ref, b_ref, kd_ref, r_ref, kkn_ref, v_ref,
         y_ref, sfin_ref, s_scr, sa_scr) = refs
    j = pl.program_id(1)

    @pl.when(j == 0)
    def _():
        if has_init:
            s_scr[...] = s0_ref[...]
            acc = jnp.zeros((HEAD_DIM, LANES), F32)
            for k in range(HEAD_DIM):
                acc = acc + s0_ref[k] * kk0_ref[0, k:k + 1, :]
            sa_scr[...] = acc
        else:
            s_scr[...] = jnp.zeros(s_scr.shape, F32)
            sa_scr[...] = jnp.zeros(sa_scr.shape, F32)

    def step(s, carry):
        sa = sa_scr[...]
        vt = v_ref[s]

        def kblock(kb, acc):
            y, san = acc
            for ki in range(8):
                k = kb * 8 + ki
                wk = w_ref[s, pl.ds(k, 1), :]
                bk = b_ref[s, pl.ds(k, 1), :]
                kdk = kd_ref[s, pl.ds(k, 1), :]
                rk = r_ref[s, pl.ds(k, 1), :]
                kn = kkn_ref[s, pl.ds(k, 1), :]
                sn = s_scr[k] * wk - sa * bk + vt * kdk
                s_scr[k] = sn
                y = y + sn * rk
                san = san + sn * kn
            return y, san

        zero = jnp.zeros((HEAD_DIM, LANES), F32)
        y, san = lax.fori_loop(0, HEAD_DIM // 8, kblock, (zero, zero))
        y_ref[s] = y
        sa_scr[...] = san
        return carry

    lax.fori_loop(0, SCAN_STEPS, step, 0)

    @pl.when(j == pl.num_programs(1) - 1)
    def _():
        sfin_ref[...] = s_scr[...]


def _scan_call(w, b, kd, r, kkn, v, kk0=None, s0=None):
    n_grp, n_step = w.shape[0], w.shape[1]
    has_init = s0 is not None
    blk = pl.BlockSpec((None, SCAN_STEPS, HEAD_DIM, LANES), lambda g, j: (g, j, 0, 0))
    in_specs = [blk] * 6
    args = [w, b, kd, r, kkn, v]
    if has_init:
        in_specs += [pl.BlockSpec((None, 1, HEAD_DIM, LANES), lambda g, j: (g, 0, 0, 0)),
                     pl.BlockSpec((None, HEAD_DIM, HEAD_DIM, LANES), lambda g, j: (g, 0, 0, 0))]
        args += [kk0, s0]
    return pl.pallas_call(
        functools.partial(_scan_kernel, has_init=has_init),
        grid=(n_grp, n_step // SCAN_STEPS),
        in_specs=in_specs,
        out_specs=[blk, pl.BlockSpec((None, HEAD_DIM, HEAD_DIM, LANES), lambda g, j: (g, 0, 0, 0))],
        out_shape=[jax.ShapeDtypeStruct((n_grp, n_step, HEAD_DIM, LANES), F32),
                   jax.ShapeDtypeStruct((n_grp, HEAD_DIM, HEAD_DIM, LANES), F32)],
        scratch_shapes=[pltpu.VMEM((HEAD_DIM, HEAD_DIM, LANES), F32), pltpu.VMEM((HEAD_DIM, LANES), F32)],
        compiler_params=pltpu.CompilerParams(dimension_semantics=("arbitrary", "arbitrary"),
                                             vmem_limit_bytes=VMEM_LIMIT),
        name="wkv_scan_init" if has_init else "wkv_scan",
    )(*args)


def _attn_kernel(q_ref, k_ref, v_ref, o_ref):
    q = q_ref[...]
    k = k_ref[...]
    v = v_ref[...]
    group = H_Q // H_KV
    for kvh in range(H_KV):
        kh = k[:, HEAD_DIM * kvh:HEAD_DIM * (kvh + 1)].astype(BF16)
        vh = v[:, HEAD_DIM * kvh:HEAD_DIM * (kvh + 1)].astype(BF16)
        for gq in range(group):
            hq = kvh * group + gq
            qh = (q[:, HEAD_DIM * hq:HEAD_DIM * (hq + 1)] * (HEAD_DIM ** -0.5)).astype(BF16)
            s = lax.dot_general(qh, kh, (((1,), (1,)), ((), ())), preferred_element_type=F32)
            e = jnp.exp(s - jnp.max(s, axis=-1, keepdims=True))
            o = _dot(e.astype(BF16), vh) / jnp.sum(e, axis=-1, keepdims=True)
            o_ref[:, HEAD_DIM * hq:HEAD_DIM * (hq + 1)] = o


def _attn_call(q2d, k2d, v2d, n_batch, lq, lk):
    q_tiles = lq // ATTN_Q_TILE
    return pl.pallas_call(
        _attn_kernel,
        grid=(n_batch, q_tiles),
        in_specs=[
            pl.BlockSpec((ATTN_Q_TILE, D_ATTN), lambda b, i: (b * q_tiles + i, 0)),
            pl.BlockSpec((lk, D_KV), lambda b, i: (b, 0)),
            pl.BlockSpec((lk, D_KV), lambda b, i: (b, 0)),
        ],
        out_specs=pl.BlockSpec((ATTN_Q_TILE, D_ATTN), lambda b, i: (b * q_tiles + i, 0)),
        out_shape=jax.ShapeDtypeStruct((n_batch * lq, D_ATTN), F32),
        compiler_params=pltpu.CompilerParams(dimension_semantics=("arbitrary", "arbitrary"),
                                             vmem_limit_bytes=VMEM_LIMIT),
        name="gqa_attn",
    )(q2d, k2d, v2d)


def _post_kernel(x_ref, mod_ref, yf_ref, yb_ref, bonus_ref, g_ref, attn_ref,
                 gn_ref, wout_ref, gpost_ref, gffn_ref, x1_o, h2_o):
    ones512 = _head_ones(D_RWKV)
    ys = yf_ref[...] + yb_ref[...]
    mu = _dot_exact_rhs(ys, ones512) * (1.0 / HEAD_DIM)
    dlt = ys - mu
    var = _dot_exact_rhs(dlt * dlt, ones512) * (1.0 / HEAD_DIM)
    gn = gn_ref[...]
    yn = dlt * lax.rsqrt(var + GN_EPS) * gn[0:1] + gn[1:2]
    yr = (yn + bonus_ref[...]) * g_ref[...]
    mix = (_dot(yr.astype(BF16), wout_ref[0:D_RWKV, :])
           + _dot(attn_ref[...].astype(BF16), wout_ref[D_RWKV:D_MODEL, :]))
    m = mod_ref[0]
    gate1 = m[:, 2 * D_MODEL:3 * D_MODEL]
    shift2 = m[:, 3 * D_MODEL:4 * D_MODEL]
    scale2 = m[:, 4 * D_MODEL:5 * D_MODEL]
    x1 = x_ref[...] + gate1 * _rms(mix, gpost_ref[...])
    x1_o[...] = x1
    h2_o[...] = (_rms(x1, gffn_ref[...]) * (1.0 + scale2) + shift2).astype(BF16)


def _post_call(x2d, mod3d, seq_len, yf, yb, bonus, g, attn, wts):
    n_tok = x2d.shape[0]
    tiles_per_seq = seq_len // TOK_TILE
    n_mod = mod3d.shape[0]

    def const(shape):
        return pl.BlockSpec(shape, lambda i: tuple(0 for _ in shape))

    mod_map = (lambda i: (i // tiles_per_seq, 0, 0)) if n_mod > 1 else (lambda i: (0, 0, 0))
    tok = lambda wd: pl.BlockSpec((TOK_TILE, wd), lambda i: (i, 0))
    return pl.pallas_call(
        _post_kernel,
        grid=(n_tok // TOK_TILE,),
        in_specs=[tok(D_MODEL), pl.BlockSpec((1, 1, 6 * D_MODEL), mod_map),
                  tok(512), tok(512), tok(512), tok(512), tok(512),
                  const((2, D_RWKV)), const((D_MODEL, D_MODEL)), const((1, D_MODEL)), const((1, D_MODEL))],
        out_specs=[tok(D_MODEL), tok(D_MODEL)],
        out_shape=[jax.ShapeDtypeStruct((n_tok, D_MODEL), F32), jax.ShapeDtypeStruct((n_tok, D_MODEL), BF16)],
        compiler_params=pltpu.CompilerParams(dimension_semantics=("arbitrary",),
                                             vmem_limit_bytes=VMEM_LIMIT),
        name="mix_post",
    )(x2d, mod3d, yf, yb, bonus, g, attn, wts["gn"], wts["w_out"], wts["g_post"], wts["g_ffn_pre"])


def _ffn_kernel(h2_ref, x1_ref, mod_ref, wg_ref, wv_ref, cwg_ref, cwv_ref, cbg_ref, cbv_ref,
                wd_ref, gpost_ref, o_ref, acc_ref, *, seq_len):
    j = pl.program_id(1)

    @pl.when(j == 0)
    def _():
        acc_ref[...] = jnp.zeros(acc_ref.shape, F32)

    h2 = h2_ref[...]
    n_rows = h2.shape[0]
    assert seq_len & (seq_len - 1) == 0
    row = lax.broadcasted_iota(jnp.int32, (n_rows, FFN_COL_TILE), 0) & (seq_len - 1)
    has_prev = row != 0
    has_next = row != seq_len - 1

    def conv(u, cw_ref, cb_ref):
        cw = cw_ref[...]
        prev = jnp.where(has_prev, pltpu.roll(u, 1, 0), 0.0)
        nxt = jnp.where(has_next, pltpu.roll(u, n_rows - 1, 0), 0.0)
        return ((cb_ref[...] + prev * cw[0:1]) + u * cw[1:2]) + nxt * cw[2:3]

    gate = conv(_dot(h2, wg_ref[...]), cwg_ref, cbg_ref)
    val = conv(_dot(h2, wv_ref[...]), cwv_ref, cbv_ref)
    act = (gate * _sigmoid(gate) * val).astype(BF16)
    acc_ref[...] += _dot(act, wd_ref[...])

    @pl.when(j == pl.num_programs(1) - 1)
    def _():
        m = mod_ref[0]
        gate2 = m[:, 5 * D_MODEL:6 * D_MODEL]
        o_ref[...] = x1_ref[...] + gate2 * _rms(acc_ref[...], gpost_ref[...])


def _ffn_call(h2, x1, mod3d, seq_len, wts):
    n_tok = h2.shape[0]
    tiles_per_seq = max(seq_len // FFN_TOK_TILE, 1)
    n_mod = mod3d.shape[0]
    n_col = D_FF // FFN_COL_TILE
    mod_map = (lambda i, j: (i // tiles_per_seq, 0, 0)) if n_mod > 1 else (lambda i, j: (0, 0, 0))
    tok = lambda: pl.BlockSpec((FFN_TOK_TILE, D_MODEL), lambda i, j: (i, 0))
    return pl.pallas_call(
        functools.partial(_ffn_kernel, seq_len=seq_len),
        grid=(n_tok // FFN_TOK_TILE, n_col),
        in_specs=[
            tok(), tok(), pl.BlockSpec((1, 1, 6 * D_MODEL), mod_map),
            pl.BlockSpec((D_MODEL, FFN_COL_TILE), lambda i, j: (0, j)),
            pl.BlockSpec((D_MODEL, FFN_COL_TILE), lambda i, j: (0, n_col + j)),
            pl.BlockSpec((3, FFN_COL_TILE), lambda i, j: (0, j)),
            pl.BlockSpec((3, FFN_COL_TILE), lambda i, j: (0, n_col + j)),
            pl.BlockSpec((1, FFN_COL_TILE), lambda i, j: (0, j)),
            pl.BlockSpec((1, FFN_COL_TILE), lambda i, j: (0, n_col + j)),
            pl.BlockSpec((FFN_COL_TILE, D_MODEL), lambda i, j: (j, 0)),
            pl.BlockSpec((1, D_MODEL), lambda i, j: (0, 0)),
        ],
        out_specs=tok(),
        out_shape=jax.ShapeDtypeStruct((n_tok, D_MODEL), F32),
        scratch_shapes=[pltpu.VMEM((FFN_TOK_TILE, D_MODEL), F32)],
        compiler_params=pltpu.CompilerParams(dimension_semantics=("arbitrary", "arbitrary"),
                                             vmem_limit_bytes=VMEM_LIMIT),
        name="conv_ffn",
    )(h2, x1, mod3d, wts["ffn_up"], wts["ffn_up"], wts["conv_w"], wts["conv_w"],
      wts["conv_b"], wts["conv_b"], wts["ffn_down"], wts["g_ffn_post"])


def _rope_tables(n_tok, n_heads):
    quarter = HEAD_DIM // 4
    pos = jnp.arange(n_tok)
    row = (pos // GRID_W).astype(F32)
    col = (pos % GRID_W).astype(F32)
    inv_freq = ROPE_THETA ** (-jnp.arange(quarter, dtype=F32) / quarter)

    def half(p):
        ang = p[:, None] * inv_freq[None, :]
        c, s = jnp.cos(ang), jnp.sin(ang)
        return jnp.concatenate([c, c], axis=-1), jnp.concatenate([-s, s], axis=-1)

    cr, sr = half(row)
    cc, sc = half(col)
    cos = jnp.concatenate([cr, cc], axis=-1)
    sin = jnp.concatenate([sr, sc], axis=-1)
    return jnp.tile(cos, (1, n_heads)), jnp.tile(sin, (1, n_heads))


def _chain_major(x2d, n_batch, seq_len):
    return x2d.reshape(n_batch, seq_len, H_RWKV, HEAD_DIM).transpose(1, 3, 0, 2).reshape(
        seq_len, HEAD_DIM, n_batch * H_RWKV)


def _scan_operand(fwd2d, bwd2d, n_batch, seq_len, pack):
    tf = _chain_major(fwd2d, n_batch, seq_len)
    tb = _chain_major(bwd2d, n_batch, seq_len)[::-1]
    if pack:
        return jnp.concatenate([tf, tb], axis=-1)[None]
    return jnp.stack([tf, tb])


def _token_major(y, n_batch, seq_len, pack):
    if pack:
        half = n_batch * H_RWKV
        yf, yb = y[0, :, :, :half], y[0, ::-1, :, half:]
    else:
        yf, yb = y[0], y[1, ::-1]

    def back(t):
        return t.reshape(seq_len, HEAD_DIM, n_batch, H_RWKV).transpose(2, 0, 3, 1).reshape(
            n_batch * seq_len, D_RWKV)

    return back(yf), back(yb)


def _mixer_inputs(pre, n_batch, seq_len, pack):
    r, v, kk, w0, w1, b0, b1, kd0, kd1 = pre[:9]
    op = functools.partial(_scan_operand, n_batch=n_batch, seq_len=seq_len, pack=pack)
    kk_s = op(kk, kk)
    kk_next = jnp.concatenate([kk_s[:, 1:], jnp.zeros_like(kk_s[:, :1])], axis=1)
    return dict(w=op(w0, w1), b=op(b0, b1), kd=op(kd0, kd1), r=op(r, r), kkn=kk_next, v=op(v, v)), kk_s[:, :1]


def kernel(x_prompt, x_sample, cache_k, cache_v, state_rwkv, c, c_ctx, w_mod, b_mod, norm_mix_pre, norm_mix_post, norm_ffn_pre, norm_ffn_post, w_in, w0, w_up, a0, a_up, g_up, k_k, k_a, r_k, gn_w, gn_b, q_norm, k_norm, w_out, ffn_up, conv_w, conv_b, ffn_down):
    n_ctx, l_ctx = x_prompt.shape[0], x_prompt.shape[1]
    n_lat, l_lat = x_sample.shape[0], x_sample.shape[1]
    l_past = cache_k.shape[2]
    layer = 0

    zeros_w = jnp.zeros((LORA_W, D_RWKV), F32)
    w_lora = jnp.concatenate([
        jnp.concatenate([w_up[layer, 0], w_up[layer, 1], zeros_w, zeros_w], axis=1),
        jnp.concatenate([zeros_w, zeros_w, a_up[layer, 0], a_up[layer, 1]], axis=1)], axis=0)
    vec = jnp.stack([w0[layer, 0], w0[layer, 1], a0[layer, 0], a0[layer, 1], k_k[layer], k_a[layer],
                     r_k[layer].reshape(D_RWKV), jnp.tile(q_norm[layer], H_Q)])
    wts = dict(
        g_pre=norm_mix_pre[layer].reshape(1, D_MODEL),
        g_post=norm_mix_post[layer].reshape(1, D_MODEL),
        g_ffn_pre=norm_ffn_pre[layer].reshape(1, D_MODEL),
        g_ffn_post=norm_ffn_post[layer].reshape(1, D_MODEL),
        w_in=w_in[layer].astype(BF16),
        w_lora=w_lora,
        g_up=g_up[layer].astype(BF16),
        vec=vec,
        k_norm=jnp.tile(k_norm[layer], H_KV).reshape(1, D_KV),
        gn=jnp.stack([gn_w[layer], gn_b[layer]]),
        w_out=w_out[layer].astype(BF16),
        ffn_up=ffn_up[layer].astype(BF16),
        conv_w=conv_w[layer],
        conv_b=conv_b[layer].reshape(1, 2 * D_FF),
        ffn_down=ffn_down[layer].astype(BF16),
    )

    cvec = jnp.concatenate([c, c_ctx[None, :], jnp.zeros((16 - n_lat - 1, D_MODEL), F32)], axis=0)
    mod = _mod_call(cvec, w_mod[layer], b_mod[layer])
    mod_lat = mod[:n_lat].reshape(n_lat, 1, 6 * D_MODEL)
    mod_ctx = mod[n_lat:n_lat + 1].reshape(1, 1, 6 * D_MODEL)

    xc = x_prompt.reshape(n_ctx * l_ctx, D_MODEL)
    pre_c = _pre_call(xc, mod_ctx, l_ctx, wts, None)
    g_c, bonus_c, q_c, ka_c, va_c = pre_c[9:]
    ops_c, _ = _mixer_inputs(pre_c, n_ctx, l_ctx, pack=False)
    y_c, s_c = _scan_call(**ops_c)
    yf_c, yb_c = _token_major(y_c, n_ctx, l_ctx, pack=False)
    attn_c = _attn_call(q_c, ka_c, va_c, n_ctx, l_ctx, l_ctx)
    x1_c, h2_c = _post_call(xc, mod_ctx, l_ctx, yf_c, yb_c, bonus_c, g_c, attn_c, wts)
    out_c = _ffn_call(h2_c, x1_c, mod_ctx, l_ctx, wts)

    xl = x_sample.reshape(n_lat * l_lat, D_MODEL)
    tabs = _rope_tables(l_lat, H_Q) + _rope_tables(l_lat, H_KV)
    pre_l = _pre_call(xl, mod_lat, l_lat, wts, tabs)
    g_l, bonus_l, q_l, kr_l, va_l = pre_l[9:]
    ops_l, kk0_l = _mixer_inputs(pre_l, n_lat, l_lat, pack=True)
    s0 = state_rwkv[:, layer].transpose(4, 3, 1, 0, 2).reshape(1, HEAD_DIM, HEAD_DIM, 2 * n_lat * H_RWKV)
    y_l, _ = _scan_call(kk0=kk0_l, s0=s0, **ops_l)
    yf_l, yb_l = _token_major(y_l, n_lat, l_lat, pack=True)
    k_all = jnp.concatenate([kr_l.reshape(n_lat, l_lat, D_KV), cache_k[:, layer].reshape(n_lat, l_past, D_KV)],
                            axis=1).reshape(n_lat * (l_lat + l_past), D_KV)
    v_all = jnp.concatenate([va_l.reshape(n_lat, l_lat, D_KV), cache_v[:, layer].reshape(n_lat, l_past, D_KV)],
                            axis=1).reshape(n_lat * (l_lat + l_past), D_KV)
    attn_l = _attn_call(q_l, k_all, v_all, n_lat, l_lat, l_lat + l_past)
    x1_l, h2_l = _post_call(xl, mod_lat, l_lat, yf_l, yb_l, bonus_l, g_l, attn_l, wts)
    out_l = _ffn_call(h2_l, x1_l, mod_lat, l_lat, wts)

    y_prompt = out_c.reshape(n_ctx, l_ctx, D_MODEL)
    y_sample = out_l.reshape(n_lat, l_lat, D_MODEL)
    new_cache_k = ka_c.reshape(n_ctx, 1, l_ctx, H_KV, HEAD_DIM)
    new_cache_v = va_c.reshape(n_ctx, 1, l_ctx, H_KV, HEAD_DIM)
    new_state = s_c.reshape(2, HEAD_DIM, HEAD_DIM, n_ctx, H_RWKV).transpose(3, 0, 4, 2, 1)[:, None]
    return (y_prompt, y_sample, new_cache_k, new_cache_v, new_state)
```

```python
import functools

import jax
import jax.numpy as jnp
from jax import lax
from jax.experimental import pallas as pl
from jax.experimental.pallas import tpu as pltpu

F32 = jnp.float32
BF16 = jnp.bfloat16

D_MODEL = 1024
HEAD_DIM = 64
D_RWKV = 512
H_RWKV = 8
D_ATTN = 512
H_Q = 8
H_KV = 2
D_KV = 128
LORA_W = 64
LORA_A = 64
LORA_G = 128
D_FF = 2816
D_IN = 2560
GRID_W = 64
ROPE_THETA = 10000.0
RMS_EPS = 1e-6
GN_EPS = 64e-5

LANES = 128
SUBLANES = 8
TOK_TILE = 256
FFN_TOK_TILE = 1024
FFN_COL_TILE = 256
SCAN_STEPS = 32
ATTN_Q_TILE = 256
MOD_COL_TILE = 768
VMEM_LIMIT = 56 * 1024 * 1024


def _dot(a, b):
    return jnp.dot(a, b, preferred_element_type=F32)


def _split_bf16(x):
    hi = x.astype(BF16)
    lo = (x - hi.astype(F32)).astype(BF16)
    return hi, lo


def _dot_exact_rhs(x, w_bf16):
    hi, lo = _split_bf16(x)
    return _dot(hi, w_bf16) + _dot(lo, w_bf16)


def _dot3(a, b):
    ah, al = _split_bf16(a)
    bh, bl = _split_bf16(b)
    return _dot(ah, bh) + (_dot(ah, bl) + _dot(al, bh))


def _head_ones(n):
    shift = HEAD_DIM.bit_length() - 1
    r = lax.shift_right_logical(lax.broadcasted_iota(jnp.int32, (n, n), 0), shift)
    c = lax.shift_right_logical(lax.broadcasted_iota(jnp.int32, (n, n), 1), shift)
    return jnp.where(r == c, 1.0, 0.0).astype(BF16)


def _sigmoid(x):
    return 1.0 / (1.0 + jnp.exp(-x))


def _softplus(x):
    return jnp.maximum(x, 0.0) + jnp.log(1.0 + jnp.exp(-jnp.abs(x)))


def _rms(x, g):
    return x * lax.rsqrt(jnp.mean(x * x, axis=-1, keepdims=True) + RMS_EPS) * g


def _rope(x, cos, sgn_sin):
    n = x.shape[-1]
    lane = lax.broadcasted_iota(jnp.int32, x.shape, 1)
    partner = jnp.where((lane & 16) == 0, pltpu.roll(x, n - 16, 1), pltpu.roll(x, 16, 1))
    return x * cos + partner * sgn_sin


def _mod_kernel(c_ref, w_ref, b_ref, o_ref):
    c = c_ref[...]
    o_ref[...] = _dot3(c * _sigmoid(c), w_ref[...]) + b_ref[...]


def _mod_call(cvec, w_mod, b_mod):
    n = w_mod.shape[1]
    rows = cvec.shape[0]
    return pl.pallas_call(
        _mod_kernel,
        grid=(n // MOD_COL_TILE,),
        in_specs=[
            pl.BlockSpec((rows, D_MODEL), lambda j: (0, 0)),
            pl.BlockSpec((D_MODEL, MOD_COL_TILE), lambda j: (0, j)),
            pl.BlockSpec((1, MOD_COL_TILE), lambda j: (0, j)),
        ],
        out_specs=pl.BlockSpec((rows, MOD_COL_TILE), lambda j: (0, j)),
        out_shape=jax.ShapeDtypeStruct((rows, n), F32),
        compiler_params=pltpu.CompilerParams(dimension_semantics=("arbitrary",)),
        name="adaln_mod",
    )(cvec, w_mod, b_mod.reshape(1, n))


_V_KK, _V_KA, _V_RK, _V_QN = range(4)


def _pre_kernel(*refs, rope):
    if rope:
        (x_ref, mod_ref, gpre_ref, win_ref, wlora_ref, gup_ref, vec_ref, wa0_ref, kn_ref,
         cq_ref, sq_ref, ck_ref, sk_ref,
         r_o, v_o, kk_o, k_o, w_o, a_o, g_o, bonus_o, q_o, ka_o, va_o) = refs
    else:
        (x_ref, mod_ref, gpre_ref, win_ref, wlora_ref, gup_ref, vec_ref, wa0_ref, kn_ref,
         r_o, v_o, kk_o, k_o, w_o, a_o, g_o, bonus_o, q_o, ka_o, va_o) = refs

    x = x_ref[...]
    m = mod_ref[0]
    shift1 = m[:, 0:D_MODEL]
    scale1 = m[:, D_MODEL:2 * D_MODEL]
    h = _rms(x, gpre_ref[...]) * (1.0 + scale1) + shift1
    p = _dot(h.astype(BF16), win_ref[...])

    r = p[:, 0:512]
    k = p[:, 512:1024]
    v = p[:, 1024:1536]
    xwa = p[:, 1536:1664]
    xg = p[:, 1664:1792]
    q = p[:, 1792:2304]
    ka = p[:, 2304:2432]
    va = p[:, 2432:2560]

    vec = vec_ref[...]
    ones512 = _head_ones(D_RWKV)

    g_o[...] = _dot(_sigmoid(xg).astype(BF16), gup_ref[...])

    kx = k * vec[_V_KK:_V_KK + 1]
    kk_o[...] = kx * lax.rsqrt(_dot_exact_rhs(kx * kx, ones512) + 1e-12)

    lane = lax.broadcasted_iota(jnp.int32, xwa.shape, 1)
    lora_in = jnp.where(lane < LORA_W, jnp.tanh(xwa), xwa)
    lora = _dot3(lora_in, wlora_ref[...])

    z = wa0_ref[0:1] + lora[:, 0:2 * D_RWKV]
    logw = -_softplus(-z) - 0.5
    w_o[...] = jnp.exp(-jnp.exp(logw))
    a = _sigmoid(wa0_ref[1:2] + lora[:, 2 * D_RWKV:4 * D_RWKV])
    a_o[...] = a

    k_a = vec[_V_KA:_V_KA + 1]
    kd_sum = k * (1.0 + (a[:, 0:D_RWKV] - 1.0) * k_a) + k * (1.0 + (a[:, D_RWKV:2 * D_RWKV] - 1.0) * k_a)
    bonus_o[...] = _dot_exact_rhs(r * kd_sum * vec[_V_RK:_V_RK + 1], ones512) * v

    r_o[...] = r
    v_o[...] = v
    k_o[...] = k

    qn = q * lax.rsqrt(_dot_exact_rhs(q * q, ones512) * (1.0 / HEAD_DIM) + RMS_EPS) * vec[_V_QN:_V_QN + 1]
    kan = ka * lax.rsqrt(_dot_exact_rhs(ka * ka, _head_ones(D_KV)) * (1.0 / HEAD_DIM) + RMS_EPS) * kn_ref[...]
    if rope:
        qn = _rope(qn, cq_ref[...], sq_ref[...])
        kan = _rope(kan, ck_ref[...], sk_ref[...])
    q_o[...] = qn
    ka_o[...] = kan
    va_o[...] = va


def _pre_call(x2d, mod3d, seq_len, wts, rope_tabs):
    n_tok = x2d.shape[0]
    tiles_per_seq = seq_len // TOK_TILE
    n_mod = mod3d.shape[0]
    rope = rope_tabs is not None

    def const(shape):
        return pl.BlockSpec(shape, lambda i: tuple(0 for _ in shape))

    mod_map = (lambda i: (i // tiles_per_seq, 0, 0)) if n_mod > 1 else (lambda i: (0, 0, 0))
    in_specs = [
        pl.BlockSpec((TOK_TILE, D_MODEL), lambda i: (i, 0)),
        pl.BlockSpec((1, 1, 6 * D_MODEL), mod_map),
        const((1, D_MODEL)),
        const((D_MODEL, D_IN)),
        const((LORA_W + LORA_A, 4 * D_RWKV)),
        const((LORA_G, D_RWKV)),
        const((4, D_RWKV)),
        const((2, 2 * D_RWKV)),
        const((1, D_KV)),
    ]
    args = [x2d, mod3d, wts["g_pre"], wts["w_in"], wts["w_lora"], wts["g_up"], wts["vec"], wts["wa0"],
            wts["k_norm"]]
    if rope:
        tab_map = lambda i: (i % tiles_per_seq, 0)
        in_specs += [pl.BlockSpec((TOK_TILE, D_ATTN), tab_map), pl.BlockSpec((TOK_TILE, D_ATTN), tab_map),
                     pl.BlockSpec((TOK_TILE, D_KV), tab_map), pl.BlockSpec((TOK_TILE, D_KV), tab_map)]
        args += list(rope_tabs)

    widths = [512] * 4 + [1024, 1024] + [512] * 3 + [D_KV, D_KV]
    out_specs = [pl.BlockSpec((TOK_TILE, wd), lambda i: (i, 0)) for wd in widths]
    out_shape = [jax.ShapeDtypeStruct((n_tok, wd), F32) for wd in widths]
    return pl.pallas_call(
        functools.partial(_pre_kernel, rope=rope),
        grid=(n_tok // TOK_TILE,),
        in_specs=in_specs,
        out_specs=out_specs,
        out_shape=out_shape,
        compiler_params=pltpu.CompilerParams(dimension_semantics=("arbitrary",),
                                             vmem_limit_bytes=VMEM_LIMIT),
        name="pre_rope" if rope else "pre",
    )(*args)


def _scan_kernel(*refs, packed, has_init):
    (rF, rB, kkF, kkB, kF, kB, vF, vB, wF, wB, aF, aB, ka_ref), rest = refs[:13], refs[13:]
    if has_init:
        s0_ref, rest = rest[0], rest[1:]
    yF, yB, sfin_ref, s_scr, sa_scr, op_scr = rest
    g = pl.program_id(0)
    j = pl.program_id(1)
    tile = (HEAD_DIM, LANES)

    if packed:
        use_b = lax.broadcasted_iota(jnp.int32, tile, 1) >= LANES // 2
    else:
        use_b = jnp.full(tile, g, jnp.int32) == 1

    def pick(f_ref, b_ref, s):
        return jnp.where(use_b, b_ref[SCAN_STEPS - 1 - s, 0:HEAD_DIM, :], f_ref[s, 0:HEAD_DIM, :])

    @pl.when(j == 0)
    def _():
        if has_init:
            s_scr[...] = s0_ref[...]
        else:
            s_scr[...] = jnp.zeros(s_scr.shape, F32)

    op_scr[4] = pick(kkF, kkB, 0)
    acc = jnp.zeros(tile, F32)
    for k in range(HEAD_DIM):
        acc = acc + s_scr[k] * op_scr[4, k:k + 1, :]
    sa_scr[...] = acc

    ka = ka_ref[...]

    def step(s, carry):
        a = pick(aF, aB, s)
        op_scr[0] = pick(wF, wB, s)
        op_scr[1] = pick(kkF, kkB, s) * a
        op_scr[2] = pick(kF, kB, s) * (1.0 + (a - 1.0) * ka)
        op_scr[3] = pick(rF, rB, s)
        op_scr[4] = pick(kkF, kkB, jnp.minimum(s + 1, SCAN_STEPS - 1))
        vt = pick(vF, vB, s)
        sa = sa_scr[...]

        def kblock(kb, acc):
            y, san = acc
            for ki in range(SUBLANES):
                k = kb * SUBLANES + ki
                wk = op_scr[0, pl.ds(k, 1), :]
                bk = op_scr[1, pl.ds(k, 1), :]
                kdk = op_scr[2, pl.ds(k, 1), :]
                rk = op_scr[3, pl.ds(k, 1), :]
                kn = op_scr[4, pl.ds(k, 1), :]
                sn = s_scr[k] * wk - sa * bk + vt * kdk
                s_scr[k] = sn
                y = y + sn * rk
                san = san + sn * kn
            return y, san

        zero = jnp.zeros(tile, F32)
        y, san = lax.fori_loop(0, HEAD_DIM // SUBLANES, kblock, (zero, zero))
        yF[s] = y
        yB[SCAN_STEPS - 1 - s] = y
        sa_scr[...] = san
        return carry

    lax.fori_loop(0, SCAN_STEPS, step, 0)

    @pl.when(j == pl.num_programs(1) - 1)
    def _():
        sfin_ref[...] = s_scr[...]


def _scan_call(shared, w, a, ka_tab, packed, s0=None):
    n_step = w.shape[1]
    n_grp = 1 if packed else 2
    n_blk = n_step // SCAN_STEPS
    has_init = s0 is not None
    blk = (None, SCAN_STEPS, HEAD_DIM, LANES)
    if packed:
        f_map = lambda g, j: (0, j, 0, 0)
        b_map = lambda g, j: (0, n_blk - 1 - j, 0, 0)
        fd_map, bd_map = f_map, b_map
    else:
        f_map = lambda g, j: (0, jnp.where(g == 0, j, 0), 0, 0)
        b_map = lambda g, j: (0, jnp.where(g == 1, n_blk - 1 - j, 0), 0, 0)
        fd_map = lambda g, j: (0, jnp.where(g == 0, j, 0), 0, 0)
        bd_map = lambda g, j: (1, jnp.where(g == 1, n_blk - 1 - j, 0), 0, 0)
    in_specs, args = [], []
    for x in shared:
        in_specs += [pl.BlockSpec(blk, f_map), pl.BlockSpec(blk, b_map)]
        args += [x, x]
    for x in (w, a):
        in_specs += [pl.BlockSpec(blk, fd_map), pl.BlockSpec(blk, bd_map)]
        args += [x, x]
    in_specs.append(pl.BlockSpec((HEAD_DIM, LANES), lambda g, j: (0, 0)))
    args.append(ka_tab)
    state_spec = pl.BlockSpec((None, HEAD_DIM, HEAD_DIM, LANES), lambda g, j: (g, 0, 0, 0))
    if has_init:
        in_specs.append(state_spec)
        args.append(s0)
    y_shape = jax.ShapeDtypeStruct((n_grp, n_step, HEAD_DIM, LANES), F32)
    return pl.pallas_call(
        functools.partial(_scan_kernel, packed=packed, has_init=has_init),
        grid=(n_grp, n_blk),
        in_specs=in_specs,
        out_specs=[pl.BlockSpec(blk, lambda g, j: (g, j, 0, 0)),
                   pl.BlockSpec(blk, lambda g, j: (g, n_blk - 1 - j, 0, 0)),
                   state_spec],
        out_shape=[y_shape, y_shape, jax.ShapeDtypeStruct((n_grp, HEAD_DIM, HEAD_DIM, LANES), F32)],
        scratch_shapes=[pltpu.VMEM((HEAD_DIM, HEAD_DIM, LANES), F32), pltpu.VMEM((HEAD_DIM, LANES), F32),
                        pltpu.VMEM((5, HEAD_DIM, LANES), F32)],
        compiler_params=pltpu.CompilerParams(dimension_semantics=("arbitrary", "arbitrary"),
                                             vmem_limit_bytes=VMEM_LIMIT),
        name="wkv_scan_packed" if packed else "wkv_scan",
    )(*args)


def _attn_kernel(q_ref, k_ref, v_ref, o_ref):
    q = q_ref[...]
    k = k_ref[...]
    v = v_ref[...]
    group = H_Q // H_KV
    for kvh in range(H_KV):
        kh = k[:, HEAD_DIM * kvh:HEAD_DIM * (kvh + 1)].astype(BF16)
        vh = v[:, HEAD_DIM * kvh:HEAD_DIM * (kvh + 1)].astype(BF16)
        for gq in range(group):
            hq = kvh * group + gq
            qh = (q[:, HEAD_DIM * hq:HEAD_DIM * (hq + 1)] * (HEAD_DIM ** -0.5)).astype(BF16)
            s = lax.dot_general(qh, kh, (((1,), (1,)), ((), ())), preferred_element_type=F32)
            e = jnp.exp(s - jnp.max(s, axis=-1, keepdims=True))
            o = _dot(e.astype(BF16), vh) / jnp.sum(e, axis=-1, keepdims=True)
            o_ref[:, HEAD_DIM * hq:HEAD_DIM * (hq + 1)] = o


def _attn_call(q2d, k2d, v2d, n_batch, lq, lk):
    q_tiles = lq // ATTN_Q_TILE
    return pl.pallas_call(
        _attn_kernel,
        grid=(n_batch, q_tiles),
        in_specs=[
            pl.BlockSpec((ATTN_Q_TILE, D_ATTN), lambda b, i: (b * q_tiles + i, 0)),
            pl.BlockSpec((lk, D_KV), lambda b, i: (b, 0)),
            pl.BlockSpec((lk, D_KV), lambda b, i: (b, 0)),
        ],
        out_specs=pl.BlockSpec((ATTN_Q_TILE, D_ATTN), lambda b, i: (b * q_tiles + i, 0)),
        out_shape=jax.ShapeDtypeStruct((n_batch * lq, D_ATTN), F32),
        compiler_params=pltpu.CompilerParams(dimension_semantics=("arbitrary", "arbitrary"),
                                             vmem_limit_bytes=VMEM_LIMIT),
        name="gqa_attn",
    )(q2d, k2d, v2d)


def _post_kernel(x_ref, mod_ref, yf_ref, yb_ref, bonus_ref, g_ref, attn_ref,
                 gn_ref, wout_ref, gpost_ref, gffn_ref, x1_o, h2_o):
    ones512 = _head_ones(D_RWKV)
    ys = yf_ref[...] + yb_ref[...]
    mu = _dot_exact_rhs(ys, ones512) * (1.0 / HEAD_DIM)
    dlt = ys - mu
    var = _dot_exact_rhs(dlt * dlt, ones512) * (1.0 / HEAD_DIM)
    gn = gn_ref[...]
    yn = dlt * lax.rsqrt(var + GN_EPS) * gn[0:1] + gn[1:2]
    yr = (yn + bonus_ref[...]) * g_ref[...]
    mix = (_dot(yr.astype(BF16), wout_ref[0:D_RWKV, :])
           + _dot(attn_ref[...].astype(BF16), wout_ref[D_RWKV:D_MODEL, :]))
    m = mod_ref[0]
    gate1 = m[:, 2 * D_MODEL:3 * D_MODEL]
    shift2 = m[:, 3 * D_MODEL:4 * D_MODEL]
    scale2 = m[:, 4 * D_MODEL:5 * D_MODEL]
    x1 = x_ref[...] + gate1 * _rms(mix, gpost_ref[...])
    x1_o[...] = x1
    h2_o[...] = (_rms(x1, gffn_ref[...]) * (1.0 + scale2) + shift2).astype(BF16)


def _post_call(x2d, mod3d, seq_len, yf, yb, bonus, g, attn, wts):
    n_tok = x2d.shape[0]
    tiles_per_seq = seq_len // TOK_TILE
    n_mod = mod3d.shape[0]

    def const(shape):
        return pl.BlockSpec(shape, lambda i: tuple(0 for _ in shape))

    mod_map = (lambda i: (i // tiles_per_seq, 0, 0)) if n_mod > 1 else (lambda i: (0, 0, 0))
    tok = lambda wd: pl.BlockSpec((TOK_TILE, wd), lambda i: (i, 0))
    return pl.pallas_call(
        _post_kernel,
        grid=(n_tok // TOK_TILE,),
        in_specs=[tok(D_MODEL), pl.BlockSpec((1, 1, 6 * D_MODEL), mod_map),
                  tok(512), tok(512), tok(512), tok(512), tok(512),
                  const((2, D_RWKV)), const((D_MODEL, D_MODEL)), const((1, D_MODEL)), const((1, D_MODEL))],
        out_specs=[tok(D_MODEL), tok(D_MODEL)],
        out_shape=[jax.ShapeDtypeStruct((n_tok, D_MODEL), F32), jax.ShapeDtypeStruct((n_tok, D_MODEL), BF16)],
        compiler_params=pltpu.CompilerParams(dimension_semantics=("arbitrary",),
                                             vmem_limit_bytes=VMEM_LIMIT),
        name="mix_post",
    )(x2d, mod3d, yf, yb, bonus, g, attn, wts["gn"], wts["w_out"], wts["g_post"], wts["g_ffn_pre"])


def _ffn_kernel(h2_ref, x1_ref, mod_ref, wg_ref, wv_ref, cwg_ref, cwv_ref, cbg_ref, cbv_ref,
                wd_ref, gpost_ref, o_ref, acc_ref, *, seq_len):
    j = pl.program_id(1)

    @pl.when(j == 0)
    def _():
        acc_ref[...] = jnp.zeros(acc_ref.shape, F32)

    h2 = h2_ref[...]
    n_rows = h2.shape[0]
    assert seq_len & (seq_len - 1) == 0
    row = lax.broadcasted_iota(jnp.int32, (n_rows, FFN_COL_TILE), 0) & (seq_len - 1)
    has_prev = row != 0
    has_next = row != seq_len - 1

    def conv(u, cw_ref, cb_ref):
        cw = cw_ref[...]
        prev = jnp.where(has_prev, pltpu.roll(u, 1, 0), 0.0)
        nxt = jnp.where(has_next, pltpu.roll(u, n_rows - 1, 0), 0.0)
        return ((cb_ref[...] + prev * cw[0:1]) + u * cw[1:2]) + nxt * cw[2:3]

    gate = conv(_dot(h2, wg_ref[...]), cwg_ref, cbg_ref)
    val = conv(_dot(h2, wv_ref[...]), cwv_ref, cbv_ref)
    act = (gate * _sigmoid(gate) * val).astype(BF16)
    acc_ref[...] += _dot(act, wd_ref[...])

    @pl.when(j == pl.num_programs(1) - 1)
    def _():
        m = mod_ref[0]
        gate2 = m[:, 5 * D_MODEL:6 * D_MODEL]
        o_ref[...] = x1_ref[...] + gate2 * _rms(acc_ref[...], gpost_ref[...])


def _ffn_call(h2, x1, mod3d, seq_len, wts):
    n_tok = h2.shape[0]
    tiles_per_seq = max(seq_len // FFN_TOK_TILE, 1)
    n_mod = mod3d.shape[0]
    n_col = D_FF // FFN_COL_TILE
    mod_map = (lambda i, j: (i // tiles_per_seq, 0, 0)) if n_mod > 1 else (lambda i, j: (0, 0, 0))
    tok = lambda: pl.BlockSpec((FFN_TOK_TILE, D_MODEL), lambda i, j: (i, 0))
    return pl.pallas_call(
        functools.partial(_ffn_kernel, seq_len=seq_len),
        grid=(n_tok // FFN_TOK_TILE, n_col),
        in_specs=[
            tok(), tok(), pl.BlockSpec((1, 1, 6 * D_MODEL), mod_map),
            pl.BlockSpec((D_MODEL, FFN_COL_TILE), lambda i, j: (0, j)),
            pl.BlockSpec((D_MODEL, FFN_COL_TILE), lambda i, j: (0, n_col + j)),
            pl.BlockSpec((3, FFN_COL_TILE), lambda i, j: (0, j)),
            pl.BlockSpec((3, FFN_COL_TILE), lambda i, j: (0, n_col + j)),
            pl.BlockSpec((1, FFN_COL_TILE), lambda i, j: (0, j)),
            pl.BlockSpec((1, FFN_COL_TILE), lambda i, j: (0, n_col + j)),
            pl.BlockSpec((FFN_COL_TILE, D_MODEL), lambda i, j: (j, 0)),
            pl.BlockSpec((1, D_MODEL), lambda i, j: (0, 0)),
        ],
        out_specs=tok(),
        out_shape=jax.ShapeDtypeStruct((n_tok, D_MODEL), F32),
        scratch_shapes=[pltpu.VMEM((FFN_TOK_TILE, D_MODEL), F32)],
        compiler_params=pltpu.CompilerParams(dimension_semantics=("arbitrary", "arbitrary"),
                                             vmem_limit_bytes=VMEM_LIMIT),
        name="conv_ffn",
    )(h2, x1, mod3d, wts["ffn_up"], wts["ffn_up"], wts["conv_w"], wts["conv_w"],
      wts["conv_b"], wts["conv_b"], wts["ffn_down"], wts["g_ffn_post"])


def _rope_tables(n_tok, n_heads):
    quarter = HEAD_DIM // 4
    pos = jnp.arange(n_tok)
    row = (pos // GRID_W).astype(F32)
    col = (pos % GRID_W).astype(F32)
    inv_freq = ROPE_THETA ** (-jnp.arange(quarter, dtype=F32) / quarter)

    def half(p):
        ang = p[:, None] * inv_freq[None, :]
        c, s = jnp.cos(ang), jnp.sin(ang)
        return jnp.concatenate([c, c], axis=-1), jnp.concatenate([-s, s], axis=-1)

    cr, sr = half(row)
    cc, sc = half(col)
    cos = jnp.concatenate([cr, cc], axis=-1)
    sin = jnp.concatenate([sr, sc], axis=-1)
    return jnp.tile(cos, (1, n_heads)), jnp.tile(sin, (1, n_heads))


def _chain_major(x2d, n_batch, seq_len, packed):
    t = x2d.reshape(n_batch, seq_len, H_RWKV, HEAD_DIM).transpose(1, 3, 0, 2)
    if packed:
        t = jnp.broadcast_to(t[:, :, None], (seq_len, HEAD_DIM, 2, n_batch, H_RWKV))
    return t.reshape(1, seq_len, HEAD_DIM, LANES)


def _chain_major_dirs(x2d, n_batch, seq_len, packed):
    t = x2d.reshape(n_batch, seq_len, 2, H_RWKV, HEAD_DIM)
    if packed:
        return t.transpose(1, 4, 2, 0, 3).reshape(1, seq_len, HEAD_DIM, LANES)
    return t.transpose(2, 1, 4, 0, 3).reshape(2, seq_len, HEAD_DIM, LANES)


def _token_major(y, n_batch, seq_len):
    return y.reshape(seq_len, HEAD_DIM, n_batch, H_RWKV).transpose(2, 0, 3, 1).reshape(
        n_batch * seq_len, D_RWKV)


def _mixer(pre, n_batch, seq_len, packed, k_a, s0=None):
    r, v, kk, k, w, a = pre[:6]
    shared = tuple(_chain_major(x, n_batch, seq_len, packed) for x in (r, kk, k, v))
    n_rep = LANES // H_RWKV
    ka_tab = jnp.tile(k_a.reshape(H_RWKV, HEAD_DIM).T, (1, n_rep))
    y_f, y_b, s_fin = _scan_call(shared, _chain_major_dirs(w, n_batch, seq_len, packed),
                                 _chain_major_dirs(a, n_batch, seq_len, packed), ka_tab, packed, s0)
    if packed:
        half = LANES // 2
        return (_token_major(y_f[0, :, :, :half], n_batch, seq_len),
                _token_major(y_b[0, :, :, half:], n_batch, seq_len), s_fin)
    return _token_major(y_f[0], n_batch, seq_len), _token_major(y_b[1], n_batch, seq_len), s_fin


def kernel(x_prompt, x_sample, cache_k, cache_v, state_rwkv, c, c_ctx, w_mod, b_mod, norm_mix_pre, norm_mix_post, norm_ffn_pre, norm_ffn_post, w_in, w0, w_up, a0, a_up, g_up, k_k, k_a, r_k, gn_w, gn_b, q_norm, k_norm, w_out, ffn_up, conv_w, conv_b, ffn_down):
    n_ctx, l_ctx = x_prompt.shape[0], x_prompt.shape[1]
    n_lat, l_lat = x_sample.shape[0], x_sample.shape[1]
    l_past = cache_k.shape[2]
    layer = 0
    assert n_ctx * H_RWKV == LANES and 2 * n_lat * H_RWKV == LANES

    zeros_w = jnp.zeros((LORA_W, D_RWKV), F32)
    w_lora = jnp.concatenate([
        jnp.concatenate([w_up[layer, 0], w_up[layer, 1], zeros_w, zeros_w], axis=1),
        jnp.concatenate([zeros_w, zeros_w, a_up[layer, 0], a_up[layer, 1]], axis=1)], axis=0)
    vec = jnp.stack([k_k[layer], k_a[layer], r_k[layer].reshape(D_RWKV), jnp.tile(q_norm[layer], H_Q)])
    wts = dict(
        g_pre=norm_mix_pre[layer].reshape(1, D_MODEL),
        g_post=norm_mix_post[layer].reshape(1, D_MODEL),
        g_ffn_pre=norm_ffn_pre[layer].reshape(1, D_MODEL),
        g_ffn_post=norm_ffn_post[layer].reshape(1, D_MODEL),
        w_in=w_in[layer].astype(BF16),
        w_lora=w_lora,
        g_up=g_up[layer].astype(BF16),
        vec=vec,
        wa0=jnp.stack([w0[layer].reshape(2 * D_RWKV), a0[layer].reshape(2 * D_RWKV)]),
        k_norm=jnp.tile(k_norm[layer], H_KV).reshape(1, D_KV),
        gn=jnp.stack([gn_w[layer], gn_b[layer]]),
        w_out=w_out[layer].astype(BF16),
        ffn_up=ffn_up[layer].astype(BF16),
        conv_w=conv_w[layer],
        conv_b=conv_b[layer].reshape(1, 2 * D_FF),
        ffn_down=ffn_down[layer].astype(BF16),
    )

    cvec = jnp.concatenate([c, c_ctx[None, :], jnp.zeros((16 - n_lat - 1, D_MODEL), F32)], axis=0)
    mod = _mod_call(cvec, w_mod[layer], b_mod[layer])
    mod_lat = mod[:n_lat].reshape(n_lat, 1, 6 * D_MODEL)
    mod_ctx = mod[n_lat:n_lat + 1].reshape(1, 1, 6 * D_MODEL)

    xc = x_prompt.reshape(n_ctx * l_ctx, D_MODEL)
    pre_c = _pre_call(xc, mod_ctx, l_ctx, wts, None)
    g_c, bonus_c, q_c, ka_c, va_c = pre_c[6:]
    yf_c, yb_c, s_c = _mixer(pre_c, n_ctx, l_ctx, False, k_a[layer])
    attn_c = _attn_call(q_c, ka_c, va_c, n_ctx, l_ctx, l_ctx)
    x1_c, h2_c = _post_call(xc, mod_ctx, l_ctx, yf_c, yb_c, bonus_c, g_c, attn_c, wts)
    out_c = _ffn_call(h2_c, x1_c, mod_ctx, l_ctx, wts)

    xl = x_sample.reshape(n_lat * l_lat, D_MODEL)
    tabs = _rope_tables(l_lat, H_Q) + _rope_tables(l_lat, H_KV)
    pre_l = _pre_call(xl, mod_lat, l_lat, wts, tabs)
    g_l, bonus_l, q_l, kr_l, va_l = pre_l[6:]
    s0 = state_rwkv[:, layer].transpose(4, 3, 1, 0, 2).reshape(1, HEAD_DIM, HEAD_DIM, LANES)
    yf_l, yb_l, _ = _mixer(pre_l, n_lat, l_lat, True, k_a[layer], s0)
    k_all = jnp.concatenate([kr_l.reshape(n_lat, l_lat, D_KV), cache_k[:, layer].reshape(n_lat, l_past, D_KV)],
                            axis=1).reshape(n_lat * (l_lat + l_past), D_KV)
    v_all = jnp.concatenate([va_l.reshape(n_lat, l_lat, D_KV), cache_v[:, layer].reshape(n_lat, l_past, D_KV)],
                            axis=1).reshape(n_lat * (l_lat + l_past), D_KV)
    attn_l = _attn_call(q_l, k_all, v_all, n_lat, l_lat, l_lat + l_past)
    x1_l, h2_l = _post_call(xl, mod_lat, l_lat, yf_l, yb_l, bonus_l, g_l, attn_l, wts)
    out_l = _ffn_call(h2_l, x1_l, mod_lat, l_lat, wts)

    y_prompt = out_c.reshape(n_ctx, l_ctx, D_MODEL)
    y_sample = out_l.reshape(n_lat, l_lat, D_MODEL)
    new_cache_k = ka_c.reshape(n_ctx, 1, l_ctx, H_KV, HEAD_DIM)
    new_cache_v = va_c.reshape(n_ctx, 1, l_ctx, H_KV, HEAD_DIM)
    new_state = s_c.reshape(2, HEAD_DIM, HEAD_DIM, n_ctx, H_RWKV).transpose(3, 0, 4, 2, 1)[:, None]
    return (y_prompt, y_sample, new_cache_k, new_cache_v, new_state)
```

```python
import functools

import jax
import jax.numpy as jnp
from jax import lax
from jax.experimental import pallas as pl
from jax.experimental.pallas import tpu as pltpu

F32 = jnp.float32
BF16 = jnp.bfloat16

D_MODEL = 1024
HEAD_DIM = 64
D_RWKV = 512
H_RWKV = 8
D_ATTN = 512
H_Q = 8
H_KV = 2
D_KV = 128
LORA_W = 64
LORA_A = 64
LORA_G = 128
D_FF = 2816
D_IN = 2560
GRID_W = 64
ROPE_THETA = 10000.0
RMS_EPS = 1e-6
GN_EPS = 64e-5

LANES = 128
SUBLANES = 8
TOK_TILE = 256
FFN_TOK_TILE = 1024
FFN_COL_TILE = 256
SCAN_STEPS = 32
RELAYOUT_TOK = 128
CHAIN_PITCH = 72
ATTN_Q_TILE = 256
MOD_COL_TILE = 768
VMEM_LIMIT = 56 * 1024 * 1024


def _dot(a, b):
    return jnp.dot(a, b, preferred_element_type=F32)


def _split_bf16(x):
    hi = x.astype(BF16)
    lo = (x - hi.astype(F32)).astype(BF16)
    return hi, lo


def _dot_exact_rhs(x, w_bf16):
    hi, lo = _split_bf16(x)
    return _dot(hi, w_bf16) + _dot(lo, w_bf16)


def _dot3(a, b):
    ah, al = _split_bf16(a)
    bh, bl = _split_bf16(b)
    return _dot(ah, bh) + (_dot(ah, bl) + _dot(al, bh))


def _head_ones(n):
    shift = HEAD_DIM.bit_length() - 1
    r = lax.shift_right_logical(lax.broadcasted_iota(jnp.int32, (n, n), 0), shift)
    c = lax.shift_right_logical(lax.broadcasted_iota(jnp.int32, (n, n), 1), shift)
    return jnp.where(r == c, 1.0, 0.0).astype(BF16)


def _sigmoid(x):
    return 1.0 / (1.0 + jnp.exp(-x))


def _softplus(x):
    return jnp.maximum(x, 0.0) + jnp.log(1.0 + jnp.exp(-jnp.abs(x)))


def _rms(x, g):
    return x * lax.rsqrt(jnp.mean(x * x, axis=-1, keepdims=True) + RMS_EPS) * g


def _rope(x, cos, sgn_sin):
    n = x.shape[-1]
    lane = lax.broadcasted_iota(jnp.int32, x.shape, 1)
    partner = jnp.where((lane & 16) == 0, pltpu.roll(x, n - 16, 1), pltpu.roll(x, 16, 1))
    return x * cos + partner * sgn_sin


def _mod_kernel(c_ref, w_ref, b_ref, o_ref):
    c = c_ref[...]
    o_ref[...] = _dot3(c * _sigmoid(c), w_ref[...]) + b_ref[...]


def _mod_call(cvec, w_mod, b_mod):
    n = w_mod.shape[1]
    rows = cvec.shape[0]
    return pl.pallas_call(
        _mod_kernel,
        grid=(n // MOD_COL_TILE,),
        in_specs=[
            pl.BlockSpec((rows, D_MODEL), lambda j: (0, 0)),
            pl.BlockSpec((D_MODEL, MOD_COL_TILE), lambda j: (0, j)),
            pl.BlockSpec((1, MOD_COL_TILE), lambda j: (0, j)),
        ],
        out_specs=pl.BlockSpec((rows, MOD_COL_TILE), lambda j: (0, j)),
        out_shape=jax.ShapeDtypeStruct((rows, n), F32),
        compiler_params=pltpu.CompilerParams(dimension_semantics=("arbitrary",)),
        name="adaln_mod",
    )(cvec, w_mod, b_mod.reshape(1, n))


_V_KK, _V_KA, _V_RK, _V_QN = range(4)


def _pre_kernel(*refs, rope):
    if rope:
        (x_ref, mod_ref, gpre_ref, win_ref, wlora_ref, gup_ref, vec_ref, wa0_ref, kn_ref,
         cq_ref, sq_ref, ck_ref, sk_ref,
         r_o, v_o, kk_o, k_o, w_o, a_o, g_o, bonus_o, q_o, ka_o, va_o) = refs
    else:
        (x_ref, mod_ref, gpre_ref, win_ref, wlora_ref, gup_ref, vec_ref, wa0_ref, kn_ref,
         r_o, v_o, kk_o, k_o, w_o, a_o, g_o, bonus_o, q_o, ka_o, va_o) = refs

    x = x_ref[...]
    m = mod_ref[0]
    shift1 = m[:, 0:D_MODEL]
    scale1 = m[:, D_MODEL:2 * D_MODEL]
    h = _rms(x, gpre_ref[...]) * (1.0 + scale1) + shift1
    p = _dot(h.astype(BF16), win_ref[...])

    r = p[:, 0:512]
    k = p[:, 512:1024]
    v = p[:, 1024:1536]
    xwa = p[:, 1536:1664]
    xg = p[:, 1664:1792]
    q = p[:, 1792:2304]
    ka = p[:, 2304:2432]
    va = p[:, 2432:2560]

    vec = vec_ref[...]
    ones512 = _head_ones(D_RWKV)

    g_o[...] = _dot(_sigmoid(xg).astype(BF16), gup_ref[...])

    kx = k * vec[_V_KK:_V_KK + 1]
    kk_o[...] = kx * lax.rsqrt(_dot_exact_rhs(kx * kx, ones512) + 1e-12)

    lane = lax.broadcasted_iota(jnp.int32, xwa.shape, 1)
    lora_in = jnp.where(lane < LORA_W, jnp.tanh(xwa), xwa)
    lora = _dot3(lora_in, wlora_ref[...])

    z = wa0_ref[0:1] + lora[:, 0:2 * D_RWKV]
    logw = -_softplus(-z) - 0.5
    w_o[...] = jnp.exp(-jnp.exp(logw))
    a = _sigmoid(wa0_ref[1:2] + lora[:, 2 * D_RWKV:4 * D_RWKV])
    a_o[...] = a

    k_a = vec[_V_KA:_V_KA + 1]
    kd_sum = k * (1.0 + (a[:, 0:D_RWKV] - 1.0) * k_a) + k * (1.0 + (a[:, D_RWKV:2 * D_RWKV] - 1.0) * k_a)
    bonus_o[...] = _dot_exact_rhs(r * kd_sum * vec[_V_RK:_V_RK + 1], ones512) * v

    r_o[...] = r
    v_o[...] = v
    k_o[...] = k

    qn = q * lax.rsqrt(_dot_exact_rhs(q * q, ones512) * (1.0 / HEAD_DIM) + RMS_EPS) * vec[_V_QN:_V_QN + 1]
    kan = ka * lax.rsqrt(_dot_exact_rhs(ka * ka, _head_ones(D_KV)) * (1.0 / HEAD_DIM) + RMS_EPS) * kn_ref[...]
    if rope:
        qn = _rope(qn, cq_ref[...], sq_ref[...])
        kan = _rope(kan, ck_ref[...], sk_ref[...])
    q_o[...] = qn
    ka_o[...] = kan
    va_o[...] = va


def _pre_call(x2d, mod3d, seq_len, wts, rope_tabs):
    n_tok = x2d.shape[0]
    tiles_per_seq = seq_len // TOK_TILE
    n_mod = mod3d.shape[0]
    rope = rope_tabs is not None

    def const(shape):
        return pl.BlockSpec(shape, lambda i: tuple(0 for _ in shape))

    mod_map = (lambda i: (i // tiles_per_seq, 0, 0)) if n_mod > 1 else (lambda i: (0, 0, 0))
    in_specs = [
        pl.BlockSpec((TOK_TILE, D_MODEL), lambda i: (i, 0)),
        pl.BlockSpec((1, 1, 6 * D_MODEL), mod_map),
        const((1, D_MODEL)),
        const((D_MODEL, D_IN)),
        const((LORA_W + LORA_A, 4 * D_RWKV)),
        const((LORA_G, D_RWKV)),
        const((4, D_RWKV)),
        const((2, 2 * D_RWKV)),
        const((1, D_KV)),
    ]
    args = [x2d, mod3d, wts["g_pre"], wts["w_in"], wts["w_lora"], wts["g_up"], wts["vec"], wts["wa0"],
            wts["k_norm"]]
    if rope:
        tab_map = lambda i: (i % tiles_per_seq, 0)
        in_specs += [pl.BlockSpec((TOK_TILE, D_ATTN), tab_map), pl.BlockSpec((TOK_TILE, D_ATTN), tab_map),
                     pl.BlockSpec((TOK_TILE, D_KV), tab_map), pl.BlockSpec((TOK_TILE, D_KV), tab_map)]
        args += list(rope_tabs)

    widths = [512] * 4 + [1024, 1024] + [512] * 3 + [D_KV, D_KV]
    out_specs = [pl.BlockSpec((TOK_TILE, wd), lambda i: (i, 0)) for wd in widths]
    out_shape = [jax.ShapeDtypeStruct((n_tok, wd), F32) for wd in widths]
    return pl.pallas_call(
        functools.partial(_pre_kernel, rope=rope),
        grid=(n_tok // TOK_TILE,),
        in_specs=in_specs,
        out_specs=out_specs,
        out_shape=out_shape,
        compiler_params=pltpu.CompilerParams(dimension_semantics=("arbitrary",),
                                             vmem_limit_bytes=VMEM_LIMIT),
        name="pre_rope" if rope else "pre",
    )(*args)


def _to_chains_kernel(x_ref, o_ref, slab_ref, *, n_batch, dup):
    n_col = x_ref.shape[2] // HEAD_DIM
    n_grp = o_ref.shape[0]

    def stage1(b, carry):
        for jc in range(n_col // 2):
            a_t = x_ref[b, :, LANES * jc:LANES * (jc + 1)].T
            for h2 in range(2):
                row0 = pl.multiple_of(((2 * jc + h2) * n_batch + b) * CHAIN_PITCH, SUBLANES)
                slab_ref[pl.ds(row0, HEAD_DIM), :] = a_t[HEAD_DIM * h2:HEAD_DIM * (h2 + 1), :]
        return carry

    lax.fori_loop(0, n_batch, stage1, 0)

    def stage2(k, carry):
        for d in range(n_grp):
            if dup:
                half = slab_ref[pl.ds(k, LANES // 2, stride=CHAIN_PITCH), :]
                tile = jnp.concatenate([half, half], axis=0)
            else:
                tile = slab_ref[pl.ds(d * LANES * CHAIN_PITCH + k, LANES, stride=CHAIN_PITCH), :]
            o_ref.at[d][pl.ds(k, RELAYOUT_TOK, stride=CHAIN_PITCH), :] = tile.T
        return carry

    lax.fori_loop(0, HEAD_DIM, stage2, 0)

    zero = jnp.zeros((RELAYOUT_TOK, LANES), F32)
    for d in range(n_grp):
        for pad in range(HEAD_DIM, CHAIN_PITCH):
            o_ref.at[d][pl.ds(pad, RELAYOUT_TOK, stride=CHAIN_PITCH), :] = zero


def _to_chains_call(x2d, n_batch, seq_len, dup):
    n_chan = x2d.shape[1]
    n_slab = (n_chan // HEAD_DIM) * n_batch
    n_grp = n_slab * (2 if dup else 1) // LANES
    out = pl.pallas_call(
        functools.partial(_to_chains_kernel, n_batch=n_batch, dup=dup),
        grid=(seq_len // RELAYOUT_TOK,),
        in_specs=[pl.BlockSpec((n_batch, RELAYOUT_TOK, n_chan), lambda i: (0, i, 0))],
        out_specs=pl.BlockSpec((n_grp, RELAYOUT_TOK * CHAIN_PITCH, LANES), lambda i: (0, i, 0)),
        out_shape=jax.ShapeDtypeStruct((n_grp, seq_len * CHAIN_PITCH, LANES), F32),
        scratch_shapes=[pltpu.VMEM((n_slab * CHAIN_PITCH, LANES), F32)],
        compiler_params=pltpu.CompilerParams(dimension_semantics=("arbitrary",),
                                             vmem_limit_bytes=VMEM_LIMIT),
        name="to_chains_dup" if dup else "to_chains",
    )(x2d.reshape(n_batch, seq_len, n_chan))
    return out.reshape(n_grp, seq_len, CHAIN_PITCH, LANES)


def _scan_kernel(*refs, packed, has_init):
    (rF, rB, kkF, kkB, kF, kB, vF, vB, wF, wB, aF, aB, ka_ref), rest = refs[:13], refs[13:]
    if has_init:
        s0_ref, rest = rest[0], rest[1:]
    yF, yB, sfin_ref, s_scr, sa_scr, op_scr = rest
    g = pl.program_id(0)
    j = pl.program_id(1)
    tile = (HEAD_DIM, LANES)

    if packed:
        use_b = lax.broadcasted_iota(jnp.int32, tile, 1) >= LANES // 2
    else:
        use_b = jnp.full(tile, g, jnp.int32) == 1

    def pick(f_ref, b_ref, s):
        return jnp.where(use_b, b_ref[SCAN_STEPS - 1 - s, 0:HEAD_DIM, :], f_ref[s, 0:HEAD_DIM, :])

    @pl.when(j == 0)
    def _():
        if has_init:
            s_scr[...] = s0_ref[...]
        else:
            s_scr[...] = jnp.zeros(s_scr.shape, F32)

    op_scr[4] = pick(kkF, kkB, 0)
    acc = jnp.zeros(tile, F32)
    for k in range(HEAD_DIM):
        acc = acc + s_scr[k] * op_scr[4, k:k + 1, :]
    sa_scr[...] = acc

    ka = ka_ref[...]

    def step(s, carry):
        a = pick(aF, aB, s)
        op_scr[0] = pick(wF, wB, s)
        op_scr[1] = pick(kkF, kkB, s) * a
        op_scr[2] = pick(kF, kB, s) * (1.0 + (a - 1.0) * ka)
        op_scr[3] = pick(rF, rB, s)
        op_scr[4] = pick(kkF, kkB, jnp.minimum(s + 1, SCAN_STEPS - 1))
        vt = pick(vF, vB, s)
        sa = sa_scr[...]

        def kblock(kb, acc):
            y, san = acc
            for ki in range(SUBLANES):
                k = kb * SUBLANES + ki
                wk = op_scr[0, pl.ds(k, 1), :]
                bk = op_scr[1, pl.ds(k, 1), :]
                kdk = op_scr[2, pl.ds(k, 1), :]
                rk = op_scr[3, pl.ds(k, 1), :]
                kn = op_scr[4, pl.ds(k, 1), :]
                sn = s_scr[k] * wk - sa * bk + vt * kdk
                s_scr[k] = sn
                y = y + sn * rk
                san = san + sn * kn
            return y, san

        zero = jnp.zeros(tile, F32)
        y, san = lax.fori_loop(0, HEAD_DIM // SUBLANES, kblock, (zero, zero))
        yF[s] = y
        yB[SCAN_STEPS - 1 - s] = y
        sa_scr[...] = san
        return carry

    lax.fori_loop(0, SCAN_STEPS, step, 0)

    @pl.when(j == pl.num_programs(1) - 1)
    def _():
        sfin_ref[...] = s_scr[...]


def _scan_call(shared, w, a, ka_tab, packed, s0=None):
    n_step = w.shape[1]
    n_grp = 1 if packed else 2
    n_blk = n_step // SCAN_STEPS
    has_init = s0 is not None
    blk = (None, SCAN_STEPS, CHAIN_PITCH, LANES)
    y_blk = (None, SCAN_STEPS, HEAD_DIM, LANES)
    if packed:
        f_map = lambda g, j: (0, j, 0, 0)
        b_map = lambda g, j: (0, n_blk - 1 - j, 0, 0)
        fd_map, bd_map = f_map, b_map
    else:
        f_map = lambda g, j: (0, jnp.where(g == 0, j, 0), 0, 0)
        b_map = lambda g, j: (0, jnp.where(g == 1, n_blk - 1 - j, 0), 0, 0)
        fd_map = lambda g, j: (0, jnp.where(g == 0, j, 0), 0, 0)
        bd_map = lambda g, j: (1, jnp.where(g == 1, n_blk - 1 - j, 0), 0, 0)
    in_specs, args = [], []
    for x in shared:
        in_specs += [pl.BlockSpec(blk, f_map), pl.BlockSpec(blk, b_map)]
        args += [x, x]
    for x in (w, a):
        in_specs += [pl.BlockSpec(blk, fd_map), pl.BlockSpec(blk, bd_map)]
        args += [x, x]
    in_specs.append(pl.BlockSpec((HEAD_DIM, LANES), lambda g, j: (0, 0)))
    args.append(ka_tab)
    state_spec = pl.BlockSpec((None, HEAD_DIM, HEAD_DIM, LANES), lambda g, j: (g, 0, 0, 0))
    if has_init:
        in_specs.append(state_spec)
        args.append(s0)
    y_shape = jax.ShapeDtypeStruct((n_grp, n_step, HEAD_DIM, LANES), F32)
    return pl.pallas_call(
        functools.partial(_scan_kernel, packed=packed, has_init=has_init),
        grid=(n_grp, n_blk),
        in_specs=in_specs,
        out_specs=[pl.BlockSpec(y_blk, lambda g, j: (g, j, 0, 0)),
                   pl.BlockSpec(y_blk, lambda g, j: (g, n_blk - 1 - j, 0, 0)),
                   state_spec],
        out_shape=[y_shape, y_shape, jax.ShapeDtypeStruct((n_grp, HEAD_DIM, HEAD_DIM, LANES), F32)],
        scratch_shapes=[pltpu.VMEM((HEAD_DIM, HEAD_DIM, LANES), F32), pltpu.VMEM((HEAD_DIM, LANES), F32),
                        pltpu.VMEM((5, HEAD_DIM, LANES), F32)],
        compiler_params=pltpu.CompilerParams(dimension_semantics=("arbitrary", "arbitrary"),
                                             vmem_limit_bytes=VMEM_LIMIT),
        name="wkv_scan_packed" if packed else "wkv_scan",
    )(*args)


def _attn_kernel(q_ref, k_ref, v_ref, o_ref):
    q = q_ref[...]
    k = k_ref[...]
    v = v_ref[...]
    group = H_Q // H_KV
    for kvh in range(H_KV):
        kh = k[:, HEAD_DIM * kvh:HEAD_DIM * (kvh + 1)].astype(BF16)
        vh = v[:, HEAD_DIM * kvh:HEAD_DIM * (kvh + 1)].astype(BF16)
        for gq in range(group):
            hq = kvh * group + gq
            qh = (q[:, HEAD_DIM * hq:HEAD_DIM * (hq + 1)] * (HEAD_DIM ** -0.5)).astype(BF16)
            s = lax.dot_general(qh, kh, (((1,), (1,)), ((), ())), preferred_element_type=F32)
            e = jnp.exp(s - jnp.max(s, axis=-1, keepdims=True))
            o = _dot(e.astype(BF16), vh) / jnp.sum(e, axis=-1, keepdims=True)
            o_ref[:, HEAD_DIM * hq:HEAD_DIM * (hq + 1)] = o


def _attn_call(q2d, k2d, v2d, n_batch, lq, lk):
    q_tiles = lq // ATTN_Q_TILE
    return pl.pallas_call(
        _attn_kernel,
        grid=(n_batch, q_tiles),
        in_specs=[
            pl.BlockSpec((ATTN_Q_TILE, D_ATTN), lambda b, i: (b * q_tiles + i, 0)),
            pl.BlockSpec((lk, D_KV), lambda b, i: (b, 0)),
            pl.BlockSpec((lk, D_KV), lambda b, i: (b, 0)),
        ],
        out_specs=pl.BlockSpec((ATTN_Q_TILE, D_ATTN), lambda b, i: (b * q_tiles + i, 0)),
        out_shape=jax.ShapeDtypeStruct((n_batch * lq, D_ATTN), F32),
        compiler_params=pltpu.CompilerParams(dimension_semantics=("arbitrary", "arbitrary"),
                                             vmem_limit_bytes=VMEM_LIMIT),
        name="gqa_attn",
    )(q2d, k2d, v2d)


def _post_kernel(x_ref, mod_ref, yf_ref, yb_ref, bonus_ref, g_ref, attn_ref,
                 gn_ref, wout_ref, gpost_ref, gffn_ref, x1_o, h2_o):
    ones512 = _head_ones(D_RWKV)
    ys = yf_ref[...] + yb_ref[...]
    mu = _dot_exact_rhs(ys, ones512) * (1.0 / HEAD_DIM)
    dlt = ys - mu
    var = _dot_exact_rhs(dlt * dlt, ones512) * (1.0 / HEAD_DIM)
    gn = gn_ref[...]
    yn = dlt * lax.rsqrt(var + GN_EPS) * gn[0:1] + gn[1:2]
    yr = (yn + bonus_ref[...]) * g_ref[...]
    mix = (_dot(yr.astype(BF16), wout_ref[0:D_RWKV, :])
           + _dot(attn_ref[...].astype(BF16), wout_ref[D_RWKV:D_MODEL, :]))
    m = mod_ref[0]
    gate1 = m[:, 2 * D_MODEL:3 * D_MODEL]
    shift2 = m[:, 3 * D_MODEL:4 * D_MODEL]
    scale2 = m[:, 4 * D_MODEL:5 * D_MODEL]
    x1 = x_ref[...] + gate1 * _rms(mix, gpost_ref[...])
    x1_o[...] = x1
    h2_o[...] = (_rms(x1, gffn_ref[...]) * (1.0 + scale2) + shift2).astype(BF16)


def _post_call(x2d, mod3d, seq_len, yf, yb, bonus, g, attn, wts):
    n_tok = x2d.shape[0]
    tiles_per_seq = seq_len // TOK_TILE
    n_mod = mod3d.shape[0]

    def const(shape):
        return pl.BlockSpec(shape, lambda i: tuple(0 for _ in shape))

    mod_map = (lambda i: (i // tiles_per_seq, 0, 0)) if n_mod > 1 else (lambda i: (0, 0, 0))
    tok = lambda wd: pl.BlockSpec((TOK_TILE, wd), lambda i: (i, 0))
    return pl.pallas_call(
        _post_kernel,
        grid=(n_tok // TOK_TILE,),
        in_specs=[tok(D_MODEL), pl.BlockSpec((1, 1, 6 * D_MODEL), mod_map),
                  tok(512), tok(512), tok(512), tok(512), tok(512),
                  const((2, D_RWKV)), const((D_MODEL, D_MODEL)), const((1, D_MODEL)), const((1, D_MODEL))],
        out_specs=[tok(D_MODEL), tok(D_MODEL)],
        out_shape=[jax.ShapeDtypeStruct((n_tok, D_MODEL), F32), jax.ShapeDtypeStruct((n_tok, D_MODEL), BF16)],
        compiler_params=pltpu.CompilerParams(dimension_semantics=("arbitrary",),
                                             vmem_limit_bytes=VMEM_LIMIT),
        name="mix_post",
    )(x2d, mod3d, yf, yb, bonus, g, attn, wts["gn"], wts["w_out"], wts["g_post"], wts["g_ffn_pre"])


def _ffn_kernel(h2_ref, x1_ref, mod_ref, wg_ref, wv_ref, cwg_ref, cwv_ref, cbg_ref, cbv_ref,
                wd_ref, gpost_ref, o_ref, acc_ref, *, seq_len):
    j = pl.program_id(1)

    @pl.when(j == 0)
    def _():
        acc_ref[...] = jnp.zeros(acc_ref.shape, F32)

    h2 = h2_ref[...]
    n_rows = h2.shape[0]
    assert seq_len & (seq_len - 1) == 0
    row = lax.broadcasted_iota(jnp.int32, (n_rows, FFN_COL_TILE), 0) & (seq_len - 1)
    has_prev = row != 0
    has_next = row != seq_len - 1

    def conv(u, cw_ref, cb_ref):
        cw = cw_ref[...]
        prev = jnp.where(has_prev, pltpu.roll(u, 1, 0), 0.0)
        nxt = jnp.where(has_next, pltpu.roll(u, n_rows - 1, 0), 0.0)
        return ((cb_ref[...] + prev * cw[0:1]) + u * cw[1:2]) + nxt * cw[2:3]

    gate = conv(_dot(h2, wg_ref[...]), cwg_ref, cbg_ref)
    val = conv(_dot(h2, wv_ref[...]), cwv_ref, cbv_ref)
    act = (gate * _sigmoid(gate) * val).astype(BF16)
    acc_ref[...] += _dot(act, wd_ref[...])

    @pl.when(j == pl.num_programs(1) - 1)
    def _():
        m = mod_ref[0]
        gate2 = m[:, 5 * D_MODEL:6 * D_MODEL]
        o_ref[...] = x1_ref[...] + gate2 * _rms(acc_ref[...], gpost_ref[...])


def _ffn_call(h2, x1, mod3d, seq_len, wts):
    n_tok = h2.shape[0]
    tiles_per_seq = max(seq_len // FFN_TOK_TILE, 1)
    n_mod = mod3d.shape[0]
    n_col = D_FF // FFN_COL_TILE
    mod_map = (lambda i, j: (i // tiles_per_seq, 0, 0)) if n_mod > 1 else (lambda i, j: (0, 0, 0))
    tok = lambda: pl.BlockSpec((FFN_TOK_TILE, D_MODEL), lambda i, j: (i, 0))
    return pl.pallas_call(
        functools.partial(_ffn_kernel, seq_len=seq_len),
        grid=(n_tok // FFN_TOK_TILE, n_col),
        in_specs=[
            tok(), tok(), pl.BlockSpec((1, 1, 6 * D_MODEL), mod_map),
            pl.BlockSpec((D_MODEL, FFN_COL_TILE), lambda i, j: (0, j)),
            pl.BlockSpec((D_MODEL, FFN_COL_TILE), lambda i, j: (0, n_col + j)),
            pl.BlockSpec((3, FFN_COL_TILE), lambda i, j: (0, j)),
            pl.BlockSpec((3, FFN_COL_TILE), lambda i, j: (0, n_col + j)),
            pl.BlockSpec((1, FFN_COL_TILE), lambda i, j: (0, j)),
            pl.BlockSpec((1, FFN_COL_TILE), lambda i, j: (0, n_col + j)),
            pl.BlockSpec((FFN_COL_TILE, D_MODEL), lambda i, j: (j, 0)),
            pl.BlockSpec((1, D_MODEL), lambda i, j: (0, 0)),
        ],
        out_specs=tok(),
        out_shape=jax.ShapeDtypeStruct((n_tok, D_MODEL), F32),
        scratch_shapes=[pltpu.VMEM((FFN_TOK_TILE, D_MODEL), F32)],
        compiler_params=pltpu.CompilerParams(dimension_semantics=("arbitrary", "arbitrary"),
                                             vmem_limit_bytes=VMEM_LIMIT),
        name="conv_ffn",
    )(h2, x1, mod3d, wts["ffn_up"], wts["ffn_up"], wts["conv_w"], wts["conv_w"],
      wts["conv_b"], wts["conv_b"], wts["ffn_down"], wts["g_ffn_post"])


def _rope_tables(n_tok, n_heads):
    quarter = HEAD_DIM // 4
    pos = jnp.arange(n_tok)
    row = (pos // GRID_W).astype(F32)
    col = (pos % GRID_W).astype(F32)
    inv_freq = ROPE_THETA ** (-jnp.arange(quarter, dtype=F32) / quarter)

    def half(p):
        ang = p[:, None] * inv_freq[None, :]
        c, s = jnp.cos(ang), jnp.sin(ang)
        return jnp.concatenate([c, c], axis=-1), jnp.concatenate([-s, s], axis=-1)

    cr, sr = half(row)
    cc, sc = half(col)
    cos = jnp.concatenate([cr, cc], axis=-1)
    sin = jnp.concatenate([sr, sc], axis=-1)
    return jnp.tile(cos, (1, n_heads)), jnp.tile(sin, (1, n_heads))


def _token_major(y, n_batch, seq_len):
    return y.reshape(seq_len, HEAD_DIM, H_RWKV, n_batch).transpose(3, 0, 2, 1).reshape(
        n_batch * seq_len, D_RWKV)


def _mixer(pre, n_batch, seq_len, packed, k_a, s0=None):
    r, v, kk, k, w, a = pre[:6]
    shared = tuple(_to_chains_call(x, n_batch, seq_len, packed) for x in (r, kk, k, v))
    ka_tab = jnp.repeat(k_a.reshape(H_RWKV, HEAD_DIM).T, n_batch, axis=1)
    ka_tab = jnp.tile(ka_tab, (1, LANES // ka_tab.shape[1]))
    y_f, y_b, s_fin = _scan_call(shared, _to_chains_call(w, n_batch, seq_len, False),
                                 _to_chains_call(a, n_batch, seq_len, False), ka_tab, packed, s0)
    if packed:
        half = LANES // 2
        return (_token_major(y_f[0, :, :, :half], n_batch, seq_len),
                _token_major(y_b[0, :, :, half:], n_batch, seq_len), s_fin)
    return _token_major(y_f[0], n_batch, seq_len), _token_major(y_b[1], n_batch, seq_len), s_fin


def kernel(x_prompt, x_sample, cache_k, cache_v, state_rwkv, c, c_ctx, w_mod, b_mod, norm_mix_pre, norm_mix_post, norm_ffn_pre, norm_ffn_post, w_in, w0, w_up, a0, a_up, g_up, k_k, k_a, r_k, gn_w, gn_b, q_norm, k_norm, w_out, ffn_up, conv_w, conv_b, ffn_down):
    n_ctx, l_ctx = x_prompt.shape[0], x_prompt.shape[1]
    n_lat, l_lat = x_sample.shape[0], x_sample.shape[1]
    l_past = cache_k.shape[2]
    layer = 0
    assert n_ctx * H_RWKV == LANES and 2 * n_lat * H_RWKV == LANES

    zeros_w = jnp.zeros((LORA_W, D_RWKV), F32)
    w_lora = jnp.concatenate([
        jnp.concatenate([w_up[layer, 0], w_up[layer, 1], zeros_w, zeros_w], axis=1),
        jnp.concatenate([zeros_w, zeros_w, a_up[layer, 0], a_up[layer, 1]], axis=1)], axis=0)
    vec = jnp.stack([k_k[layer], k_a[layer], r_k[layer].reshape(D_RWKV), jnp.tile(q_norm[layer], H_Q)])
    wts = dict(
        g_pre=norm_mix_pre[layer].reshape(1, D_MODEL),
        g_post=norm_mix_post[layer].reshape(1, D_MODEL),
        g_ffn_pre=norm_ffn_pre[layer].reshape(1, D_MODEL),
        g_ffn_post=norm_ffn_post[layer].reshape(1, D_MODEL),
        w_in=w_in[layer].astype(BF16),
        w_lora=w_lora,
        g_up=g_up[layer].astype(BF16),
        vec=vec,
        wa0=jnp.stack([w0[layer].reshape(2 * D_RWKV), a0[layer].reshape(2 * D_RWKV)]),
        k_norm=jnp.tile(k_norm[layer], H_KV).reshape(1, D_KV),
        gn=jnp.stack([gn_w[layer], gn_b[layer]]),
        w_out=w_out[layer].astype(BF16),
        ffn_up=ffn_up[layer].astype(BF16),
        conv_w=conv_w[layer],
        conv_b=conv_b[layer].reshape(1, 2 * D_FF),
        ffn_down=ffn_down[layer].astype(BF16),
    )

    cvec = jnp.concatenate([c, c_ctx[None, :], jnp.zeros((16 - n_lat - 1, D_MODEL), F32)], axis=0)
    mod = _mod_call(cvec, w_mod[layer], b_mod[layer])
    mod_lat = mod[:n_lat].reshape(n_lat, 1, 6 * D_MODEL)
    mod_ctx = mod[n_lat:n_lat + 1].reshape(1, 1, 6 * D_MODEL)

    xc = x_prompt.reshape(n_ctx * l_ctx, D_MODEL)
    pre_c = _pre_call(xc, mod_ctx, l_ctx, wts, None)
    g_c, bonus_c, q_c, ka_c, va_c = pre_c[6:]
    yf_c, yb_c, s_c = _mixer(pre_c, n_ctx, l_ctx, False, k_a[layer])
    attn_c = _attn_call(q_c, ka_c, va_c, n_ctx, l_ctx, l_ctx)
    x1_c, h2_c = _post_call(xc, mod_ctx, l_ctx, yf_c, yb_c, bonus_c, g_c, attn_c, wts)
    out_c = _ffn_call(h2_c, x1_c, mod_ctx, l_ctx, wts)

    xl = x_sample.reshape(n_lat * l_lat, D_MODEL)
    tabs = _rope_tables(l_lat, H_Q) + _rope_tables(l_lat, H_KV)
    pre_l = _pre_call(xl, mod_lat, l_lat, wts, tabs)
    g_l, bonus_l, q_l, kr_l, va_l = pre_l[6:]
    s0 = state_rwkv[:, layer].transpose(4, 3, 1, 2, 0).reshape(1, HEAD_DIM, HEAD_DIM, LANES)
    yf_l, yb_l, _ = _mixer(pre_l, n_lat, l_lat, True, k_a[layer], s0)
    k_all = jnp.concatenate([kr_l.reshape(n_lat, l_lat, D_KV), cache_k[:, layer].reshape(n_lat, l_past, D_KV)],
                            axis=1).reshape(n_lat * (l_lat + l_past), D_KV)
    v_all = jnp.concatenate([va_l.reshape(n_lat, l_lat, D_KV), cache_v[:, layer].reshape(n_lat, l_past, D_KV)],
                            axis=1).reshape(n_lat * (l_lat + l_past), D_KV)
    attn_l = _attn_call(q_l, k_all, v_all, n_lat, l_lat, l_lat + l_past)
    x1_l, h2_l = _post_call(xl, mod_lat, l_lat, yf_l, yb_l, bonus_l, g_l, attn_l, wts)
    out_l = _ffn_call(h2_l, x1_l, mod_lat, l_lat, wts)

    y_prompt = out_c.reshape(n_ctx, l_ctx, D_MODEL)
    y_sample = out_l.reshape(n_lat, l_lat, D_MODEL)
    new_cache_k = ka_c.reshape(n_ctx, 1, l_ctx, H_KV, HEAD_DIM)
    new_cache_v = va_c.reshape(n_ctx, 1, l_ctx, H_KV, HEAD_DIM)
    new_state = s_c.reshape(2, HEAD_DIM, HEAD_DIM, H_RWKV, n_ctx).transpose(4, 0, 3, 2, 1)[:, None]
    return (y_prompt, y_sample, new_cache_k, new_cache_v, new_state)
```

```python
import functools

import jax
import jax.numpy as jnp
from jax import lax
from jax.experimental import pallas as pl
from jax.experimental.pallas import tpu as pltpu

F32 = jnp.float32
BF16 = jnp.bfloat16

D_MODEL = 1024
HEAD_DIM = 64
D_RWKV = 512
H_RWKV = 8
D_ATTN = 512
H_Q = 8
H_KV = 2
D_KV = 128
LORA_W = 64
LORA_A = 64
LORA_G = 128
D_FF = 2816
D_IN = 2560
GRID_W = 64
ROPE_THETA = 10000.0
RMS_EPS = 1e-6
GN_EPS = 64e-5

LANES = 128
SUBLANES = 8
TOK_TILE = 256
FFN_TOK_TILE = 1024
FFN_COL_TILE = 256
SCAN_STEPS = 32
SCAN_K_UNROLL = 16
RELAYOUT_TOK = 128
CHAIN_PITCH = 72
ATTN_Q_TILE = 256
MOD_COL_TILE = 768
VMEM_LIMIT = 56 * 1024 * 1024


def _dot(a, b):
    return jnp.dot(a, b, preferred_element_type=F32)


def _split_bf16(x):
    hi = x.astype(BF16)
    lo = (x - hi.astype(F32)).astype(BF16)
    return hi, lo


def _dot_exact_rhs(x, w_bf16):
    return _dot(x.astype(BF16), w_bf16)


def _dot3(a, b):
    ah, al = _split_bf16(a)
    bh, bl = _split_bf16(b)
    return _dot(ah, bh) + (_dot(ah, bl) + _dot(al, bh))


def _head_ones(n):
    shift = HEAD_DIM.bit_length() - 1
    r = lax.shift_right_logical(lax.broadcasted_iota(jnp.int32, (n, n), 0), shift)
    c = lax.shift_right_logical(lax.broadcasted_iota(jnp.int32, (n, n), 1), shift)
    return jnp.where(r == c, 1.0, 0.0).astype(BF16)


def _sigmoid(x):
    return 1.0 / (1.0 + jnp.exp(-x))


def _softplus(x):
    return jnp.maximum(x, 0.0) + jnp.log(1.0 + jnp.exp(-jnp.abs(x)))


def _rms(x, g):
    return x * lax.rsqrt(jnp.mean(x * x, axis=-1, keepdims=True) + RMS_EPS) * g


def _rope(x, cos, sgn_sin):
    n = x.shape[-1]
    lane = lax.broadcasted_iota(jnp.int32, x.shape, 1)
    partner = jnp.where((lane & 16) == 0, pltpu.roll(x, n - 16, 1), pltpu.roll(x, 16, 1))
    return x * cos + partner * sgn_sin


def _mod_kernel(c_ref, w_ref, b_ref, o_ref):
    c = c_ref[...]
    o_ref[...] = _dot3(c * _sigmoid(c), w_ref[...]) + b_ref[...]


def _mod_call(cvec, w_mod, b_mod):
    n = w_mod.shape[1]
    rows = cvec.shape[0]
    return pl.pallas_call(
        _mod_kernel,
        grid=(n // MOD_COL_TILE,),
        in_specs=[
            pl.BlockSpec((rows, D_MODEL), lambda j: (0, 0)),
            pl.BlockSpec((D_MODEL, MOD_COL_TILE), lambda j: (0, j)),
            pl.BlockSpec((1, MOD_COL_TILE), lambda j: (0, j)),
        ],
        out_specs=pl.BlockSpec((rows, MOD_COL_TILE), lambda j: (0, j)),
        out_shape=jax.ShapeDtypeStruct((rows, n), F32),
        compiler_params=pltpu.CompilerParams(dimension_semantics=("arbitrary",)),
        name="adaln_mod",
    )(cvec, w_mod, b_mod.reshape(1, n))


_V_KK, _V_KA, _V_RK, _V_QN = range(4)


def _pre_kernel(*refs, rope):
    if rope:
        (x_ref, mod_ref, gpre_ref, win_ref, wlora_ref, gup_ref, vec_ref, wa0_ref, kn_ref,
         cq_ref, sq_ref, ck_ref, sk_ref,
         r_o, v_o, kk_o, k_o, w_o, a_o, g_o, bonus_o, q_o, ka_o, va_o) = refs
    else:
        (x_ref, mod_ref, gpre_ref, win_ref, wlora_ref, gup_ref, vec_ref, wa0_ref, kn_ref,
         r_o, v_o, kk_o, k_o, w_o, a_o, g_o, bonus_o, q_o, ka_o, va_o) = refs

    x = x_ref[...]
    m = mod_ref[0]
    shift1 = m[:, 0:D_MODEL]
    scale1 = m[:, D_MODEL:2 * D_MODEL]
    h = _rms(x, gpre_ref[...]) * (1.0 + scale1) + shift1
    p = _dot(h.astype(BF16), win_ref[...])

    r = p[:, 0:512]
    k = p[:, 512:1024]
    v = p[:, 1024:1536]
    xwa = p[:, 1536:1664]
    xg = p[:, 1664:1792]
    q = p[:, 1792:2304]
    ka = p[:, 2304:2432]
    va = p[:, 2432:2560]

    vec = vec_ref[...]
    ones512 = _head_ones(D_RWKV)

    g_o[...] = _dot(_sigmoid(xg).astype(BF16), gup_ref[...])

    kx = k * vec[_V_KK:_V_KK + 1]
    kk_o[...] = kx * lax.rsqrt(_dot_exact_rhs(kx * kx, ones512) + 1e-12)

    lane = lax.broadcasted_iota(jnp.int32, xwa.shape, 1)
    lora_in = jnp.where(lane < LORA_W, jnp.tanh(xwa), xwa)
    lora = _dot3(lora_in, wlora_ref[...])

    z = wa0_ref[0:1] + lora[:, 0:2 * D_RWKV]
    logw = -_softplus(-z) - 0.5
    w_o[...] = jnp.exp(-jnp.exp(logw))
    a = _sigmoid(wa0_ref[1:2] + lora[:, 2 * D_RWKV:4 * D_RWKV])
    a_o[...] = a

    k_a = vec[_V_KA:_V_KA + 1]
    kd_sum = k * (1.0 + (a[:, 0:D_RWKV] - 1.0) * k_a) + k * (1.0 + (a[:, D_RWKV:2 * D_RWKV] - 1.0) * k_a)
    bonus_o[...] = _dot_exact_rhs(r * kd_sum * vec[_V_RK:_V_RK + 1], ones512) * v

    r_o[...] = r
    v_o[...] = v
    k_o[...] = k

    qn = q * lax.rsqrt(_dot_exact_rhs(q * q, ones512) * (1.0 / HEAD_DIM) + RMS_EPS) * vec[_V_QN:_V_QN + 1]
    kan = ka * lax.rsqrt(_dot_exact_rhs(ka * ka, _head_ones(D_KV)) * (1.0 / HEAD_DIM) + RMS_EPS) * kn_ref[...]
    if rope:
        qn = _rope(qn, cq_ref[...], sq_ref[...])
        kan = _rope(kan, ck_ref[...], sk_ref[...])
    q_o[...] = qn
    ka_o[...] = kan
    va_o[...] = va


def _pre_call(x2d, mod3d, seq_len, wts, rope_tabs):
    n_tok = x2d.shape[0]
    tiles_per_seq = seq_len // TOK_TILE
    n_mod = mod3d.shape[0]
    rope = rope_tabs is not None

    def const(shape):
        return pl.BlockSpec(shape, lambda i: tuple(0 for _ in shape))

    mod_map = (lambda i: (i // tiles_per_seq, 0, 0)) if n_mod > 1 else (lambda i: (0, 0, 0))
    in_specs = [
        pl.BlockSpec((TOK_TILE, D_MODEL), lambda i: (i, 0)),
        pl.BlockSpec((1, 1, 6 * D_MODEL), mod_map),
        const((1, D_MODEL)),
        const((D_MODEL, D_IN)),
        const((LORA_W + LORA_A, 4 * D_RWKV)),
        const((LORA_G, D_RWKV)),
        const((4, D_RWKV)),
        const((2, 2 * D_RWKV)),
        const((1, D_KV)),
    ]
    args = [x2d, mod3d, wts["g_pre"], wts["w_in"], wts["w_lora"], wts["g_up"], wts["vec"], wts["wa0"],
            wts["k_norm"]]
    if rope:
        tab_map = lambda i: (i % tiles_per_seq, 0)
        in_specs += [pl.BlockSpec((TOK_TILE, D_ATTN), tab_map), pl.BlockSpec((TOK_TILE, D_ATTN), tab_map),
                     pl.BlockSpec((TOK_TILE, D_KV), tab_map), pl.BlockSpec((TOK_TILE, D_KV), tab_map)]
        args += list(rope_tabs)

    widths = [512] * 4 + [1024, 1024] + [512] * 3 + [D_KV, D_KV]
    out_specs = [pl.BlockSpec((TOK_TILE, wd), lambda i: (i, 0)) for wd in widths]
    out_shape = [jax.ShapeDtypeStruct((n_tok, wd), F32) for wd in widths]
    return pl.pallas_call(
        functools.partial(_pre_kernel, rope=rope),
        grid=(n_tok // TOK_TILE,),
        in_specs=in_specs,
        out_specs=out_specs,
        out_shape=out_shape,
        compiler_params=pltpu.CompilerParams(dimension_semantics=("arbitrary",),
                                             vmem_limit_bytes=VMEM_LIMIT),
        name="pre_rope" if rope else "pre",
    )(*args)


def _to_chains_kernel(x_ref, o_ref, slab_ref, *, n_batch, dup):
    n_col = x_ref.shape[2] // HEAD_DIM
    n_grp = o_ref.shape[0]

    def stage1(b, carry):
        for jc in range(n_col // 2):
            a_t = x_ref[b, :, LANES * jc:LANES * (jc + 1)].T
            for h2 in range(2):
                row0 = pl.multiple_of(((2 * jc + h2) * n_batch + b) * CHAIN_PITCH, SUBLANES)
                slab_ref[pl.ds(row0, HEAD_DIM), :] = a_t[HEAD_DIM * h2:HEAD_DIM * (h2 + 1), :]
        return carry

    lax.fori_loop(0, n_batch, stage1, 0)

    def stage2(kb, carry):
        for ki in range(SUBLANES):
            k = kb * SUBLANES + ki
            for d in range(n_grp):
                if dup:
                    half = slab_ref[pl.ds(k, LANES // 2, stride=CHAIN_PITCH), :]
                    tile = jnp.concatenate([half, half], axis=0)
                else:
                    tile = slab_ref[pl.ds(d * LANES * CHAIN_PITCH + k, LANES, stride=CHAIN_PITCH), :]
                o_ref.at[d][pl.ds(k, RELAYOUT_TOK, stride=CHAIN_PITCH), :] = tile.T
        return carry

    lax.fori_loop(0, HEAD_DIM // SUBLANES, stage2, 0)

    zero = jnp.zeros((RELAYOUT_TOK, LANES), F32)
    for d in range(n_grp):
        for pad in range(HEAD_DIM, CHAIN_PITCH):
            o_ref.at[d][pl.ds(pad, RELAYOUT_TOK, stride=CHAIN_PITCH), :] = zero


def _to_chains_call(x2d, n_batch, seq_len, dup):
    n_chan = x2d.shape[1]
    n_slab = (n_chan // HEAD_DIM) * n_batch
    n_grp = n_slab * (2 if dup else 1) // LANES
    out = pl.pallas_call(
        functools.partial(_to_chains_kernel, n_batch=n_batch, dup=dup),
        grid=(seq_len // RELAYOUT_TOK,),
        in_specs=[pl.BlockSpec((n_batch, RELAYOUT_TOK, n_chan), lambda i: (0, i, 0))],
        out_specs=pl.BlockSpec((n_grp, RELAYOUT_TOK * CHAIN_PITCH, LANES), lambda i: (0, i, 0)),
        out_shape=jax.ShapeDtypeStruct((n_grp, seq_len * CHAIN_PITCH, LANES), F32),
        scratch_shapes=[pltpu.VMEM((n_slab * CHAIN_PITCH, LANES), F32)],
        compiler_params=pltpu.CompilerParams(dimension_semantics=("arbitrary",),
                                             vmem_limit_bytes=VMEM_LIMIT),
        name="to_chains_dup" if dup else "to_chains",
    )(x2d.reshape(n_batch, seq_len, n_chan))
    return out.reshape(n_grp, seq_len, CHAIN_PITCH, LANES)


def _scan_kernel(*refs, packed, has_init):
    (rF, rB, kkF, kkB, kF, kB, vF, vB, wF, wB, aF, aB, ka_ref), rest = refs[:13], refs[13:]
    if has_init:
        s0_ref, rest = rest[0], rest[1:]
    yF, yB, sfin_ref, s_scr, sa_scr, op_scr = rest
    g = pl.program_id(0)
    j = pl.program_id(1)
    tile = (HEAD_DIM, LANES)

    if packed:
        use_b = lax.broadcasted_iota(jnp.int32, tile, 1) >= LANES // 2
    else:
        use_b = jnp.full(tile, g, jnp.int32) == 1

    def pick(f_ref, b_ref, s):
        return jnp.where(use_b, b_ref[SCAN_STEPS - 1 - s, 0:HEAD_DIM, :], f_ref[s, 0:HEAD_DIM, :])

    @pl.when(j == 0)
    def _():
        if has_init:
            s_scr[...] = s0_ref[...]
        else:
            s_scr[...] = jnp.zeros(s_scr.shape, F32)

    op_scr[4] = pick(kkF, kkB, 0)
    acc = jnp.zeros(tile, F32)
    for k in range(HEAD_DIM):
        acc = acc + s_scr[k] * op_scr[4, k:k + 1, :]
    sa_scr[...] = acc

    ka = ka_ref[...]

    def step(s, carry):
        a = pick(aF, aB, s)
        op_scr[0] = pick(wF, wB, s)
        op_scr[1] = pick(kkF, kkB, s) * a
        op_scr[2] = pick(kF, kB, s) * (1.0 + (a - 1.0) * ka)
        op_scr[3] = pick(rF, rB, s)
        op_scr[4] = pick(kkF, kkB, jnp.minimum(s + 1, SCAN_STEPS - 1))
        vt = pick(vF, vB, s)
        sa = sa_scr[...]

        def kblock(kb, acc):
            y, san = acc
            for ki in range(SCAN_K_UNROLL):
                k = kb * SCAN_K_UNROLL + ki
                wk = op_scr[0, pl.ds(k, 1), :]
                bk = op_scr[1, pl.ds(k, 1), :]
                kdk = op_scr[2, pl.ds(k, 1), :]
                rk = op_scr[3, pl.ds(k, 1), :]
                kn = op_scr[4, pl.ds(k, 1), :]
                sn = s_scr[k] * wk - sa * bk + vt * kdk
                s_scr[k] = sn
                y = y + sn * rk
                san = san + sn * kn
            return y, san

        zero = jnp.zeros(tile, F32)
        y, san = lax.fori_loop(0, HEAD_DIM // SCAN_K_UNROLL, kblock, (zero, zero))
        yF[s] = y
        yB[SCAN_STEPS - 1 - s] = y
        sa_scr[...] = san
        return carry

    lax.fori_loop(0, SCAN_STEPS, step, 0)

    @pl.when(j == pl.num_programs(1) - 1)
    def _():
        sfin_ref[...] = s_scr[...]


def _scan_call(shared, w, a, ka_tab, packed, s0=None):
    n_step = w.shape[1]
    n_grp = 1 if packed else 2
    n_blk = n_step // SCAN_STEPS
    has_init = s0 is not None
    blk = (None, SCAN_STEPS, CHAIN_PITCH, LANES)
    y_blk = (None, SCAN_STEPS, HEAD_DIM, LANES)
    if packed:
        f_map = lambda g, j: (0, j, 0, 0)
        b_map = lambda g, j: (0, n_blk - 1 - j, 0, 0)
        fd_map, bd_map = f_map, b_map
    else:
        f_map = lambda g, j: (0, jnp.where(g == 0, j, 0), 0, 0)
        b_map = lambda g, j: (0, jnp.where(g == 1, n_blk - 1 - j, 0), 0, 0)
        fd_map = lambda g, j: (0, jnp.where(g == 0, j, 0), 0, 0)
        bd_map = lambda g, j: (1, jnp.where(g == 1, n_blk - 1 - j, 0), 0, 0)
    in_specs, args = [], []
    for x in shared:
        in_specs += [pl.BlockSpec(blk, f_map), pl.BlockSpec(blk, b_map)]
        args += [x, x]
    for x in (w, a):
        in_specs += [pl.BlockSpec(blk, fd_map), pl.BlockSpec(blk, bd_map)]
        args += [x, x]
    in_specs.append(pl.BlockSpec((HEAD_DIM, LANES), lambda g, j: (0, 0)))
    args.append(ka_tab)
    state_spec = pl.BlockSpec((None, HEAD_DIM, HEAD_DIM, LANES), lambda g, j: (g, 0, 0, 0))
    if has_init:
        in_specs.append(state_spec)
        args.append(s0)
    y_shape = jax.ShapeDtypeStruct((n_grp, n_step, HEAD_DIM, LANES), F32)
    return pl.pallas_call(
        functools.partial(_scan_kernel, packed=packed, has_init=has_init),
        grid=(n_grp, n_blk),
        in_specs=in_specs,
        out_specs=[pl.BlockSpec(y_blk, lambda g, j: (g, j, 0, 0)),
                   pl.BlockSpec(y_blk, lambda g, j: (g, n_blk - 1 - j, 0, 0)),
                   state_spec],
        out_shape=[y_shape, y_shape, jax.ShapeDtypeStruct((n_grp, HEAD_DIM, HEAD_DIM, LANES), F32)],
        scratch_shapes=[pltpu.VMEM((HEAD_DIM, HEAD_DIM, LANES), F32), pltpu.VMEM((HEAD_DIM, LANES), F32),
                        pltpu.VMEM((5, HEAD_DIM, LANES), F32)],
        compiler_params=pltpu.CompilerParams(dimension_semantics=("arbitrary", "arbitrary"),
                                             vmem_limit_bytes=VMEM_LIMIT),
        name="wkv_scan_packed" if packed else "wkv_scan",
    )(*args)


def _attn_kernel(q_ref, k_ref, v_ref, o_ref):
    q = q_ref[...]
    k = k_ref[...]
    v = v_ref[...]
    group = H_Q // H_KV
    for kvh in range(H_KV):
        kh = k[:, HEAD_DIM * kvh:HEAD_DIM * (kvh + 1)].astype(BF16)
        vh = v[:, HEAD_DIM * kvh:HEAD_DIM * (kvh + 1)].astype(BF16)
        for gq in range(group):
            hq = kvh * group + gq
            qh = (q[:, HEAD_DIM * hq:HEAD_DIM * (hq + 1)] * (HEAD_DIM ** -0.5)).astype(BF16)
            s = lax.dot_general(qh, kh, (((1,), (1,)), ((), ())), preferred_element_type=F32)
            e = jnp.exp(s - jnp.max(s, axis=-1, keepdims=True))
            o = _dot(e.astype(BF16), vh) / jnp.sum(e, axis=-1, keepdims=True)
            o_ref[:, HEAD_DIM * hq:HEAD_DIM * (hq + 1)] = o


def _attn_call(q2d, k2d, v2d, n_batch, lq, lk):
    q_tiles = lq // ATTN_Q_TILE
    return pl.pallas_call(
        _attn_kernel,
        grid=(n_batch, q_tiles),
        in_specs=[
            pl.BlockSpec((ATTN_Q_TILE, D_ATTN), lambda b, i: (b * q_tiles + i, 0)),
            pl.BlockSpec((lk, D_KV), lambda b, i: (b, 0)),
            pl.BlockSpec((lk, D_KV), lambda b, i: (b, 0)),
        ],
        out_specs=pl.BlockSpec((ATTN_Q_TILE, D_ATTN), lambda b, i: (b * q_tiles + i, 0)),
        out_shape=jax.ShapeDtypeStruct((n_batch * lq, D_ATTN), F32),
        compiler_params=pltpu.CompilerParams(dimension_semantics=("arbitrary", "arbitrary"),
                                             vmem_limit_bytes=VMEM_LIMIT),
        name="gqa_attn",
    )(q2d, k2d, v2d)


def _post_kernel(x_ref, mod_ref, yf_ref, yb_ref, bonus_ref, g_ref, attn_ref,
                 gn_ref, wout_ref, gpost_ref, gffn_ref, x1_o, h2_o):
    ones512 = _head_ones(D_RWKV)
    ys = yf_ref[...] + yb_ref[...]
    ys_hi, ys_lo = _split_bf16(ys)
    mu = (_dot(ys_hi, ones512) + _dot(ys_lo, ones512)) * (1.0 / HEAD_DIM)
    dlt = ys - mu
    var = _dot_exact_rhs(dlt * dlt, ones512) * (1.0 / HEAD_DIM)
    gn = gn_ref[...]
    yn = dlt * lax.rsqrt(var + GN_EPS) * gn[0:1] + gn[1:2]
    yr = (yn + bonus_ref[...]) * g_ref[...]
    mix = (_dot(yr.astype(BF16), wout_ref[0:D_RWKV, :])
           + _dot(attn_ref[...].astype(BF16), wout_ref[D_RWKV:D_MODEL, :]))
    m = mod_ref[0]
    gate1 = m[:, 2 * D_MODEL:3 * D_MODEL]
    shift2 = m[:, 3 * D_MODEL:4 * D_MODEL]
    scale2 = m[:, 4 * D_MODEL:5 * D_MODEL]
    x1 = x_ref[...] + gate1 * _rms(mix, gpost_ref[...])
    x1_o[...] = x1
    h2_o[...] = (_rms(x1, gffn_ref[...]) * (1.0 + scale2) + shift2).astype(BF16)


def _post_call(x2d, mod3d, seq_len, yf, yb, bonus, g, attn, wts):
    n_tok = x2d.shape[0]
    tiles_per_seq = seq_len // TOK_TILE
    n_mod = mod3d.shape[0]

    def const(shape):
        return pl.BlockSpec(shape, lambda i: tuple(0 for _ in shape))

    mod_map = (lambda i: (i // tiles_per_seq, 0, 0)) if n_mod > 1 else (lambda i: (0, 0, 0))
    tok = lambda wd: pl.BlockSpec((TOK_TILE, wd), lambda i: (i, 0))
    return pl.pallas_call(
        _post_kernel,
        grid=(n_tok // TOK_TILE,),
        in_specs=[tok(D_MODEL), pl.BlockSpec((1, 1, 6 * D_MODEL), mod_map),
                  tok(512), tok(512), tok(512), tok(512), tok(512),
                  const((2, D_RWKV)), const((D_MODEL, D_MODEL)), const((1, D_MODEL)), const((1, D_MODEL))],
        out_specs=[tok(D_MODEL), tok(D_MODEL)],
        out_shape=[jax.ShapeDtypeStruct((n_tok, D_MODEL), F32), jax.ShapeDtypeStruct((n_tok, D_MODEL), BF16)],
        compiler_params=pltpu.CompilerParams(dimension_semantics=("arbitrary",),
                                             vmem_limit_bytes=VMEM_LIMIT),
        name="mix_post",
    )(x2d, mod3d, yf, yb, bonus, g, attn, wts["gn"], wts["w_out"], wts["g_post"], wts["g_ffn_pre"])


def _ffn_kernel(h2_ref, x1_ref, mod_ref, wg_ref, wv_ref, cwg_ref, cwv_ref, cbg_ref, cbv_ref,
                wd_ref, gpost_ref, o_ref, acc_ref, *, seq_len):
    j = pl.program_id(1)

    @pl.when(j == 0)
    def _():
        acc_ref[...] = jnp.zeros(acc_ref.shape, F32)

    h2 = h2_ref[...]
    n_rows = h2.shape[0]
    assert seq_len & (seq_len - 1) == 0
    row = lax.broadcasted_iota(jnp.int32, (n_rows, FFN_COL_TILE), 0) & (seq_len - 1)
    has_prev = row != 0
    has_next = row != seq_len - 1

    def conv(u, cw_ref, cb_ref):
        cw = cw_ref[...]
        prev = jnp.where(has_prev, pltpu.roll(u, 1, 0), 0.0)
        nxt = jnp.where(has_next, pltpu.roll(u, n_rows - 1, 0), 0.0)
        return ((cb_ref[...] + prev * cw[0:1]) + u * cw[1:2]) + nxt * cw[2:3]

    gate = conv(_dot(h2, wg_ref[...]), cwg_ref, cbg_ref)
    val = conv(_dot(h2, wv_ref[...]), cwv_ref, cbv_ref)
    act = (gate * _sigmoid(gate) * val).astype(BF16)
    acc_ref[...] += _dot(act, wd_ref[...])

    @pl.when(j == pl.num_programs(1) - 1)
    def _():
        m = mod_ref[0]
        gate2 = m[:, 5 * D_MODEL:6 * D_MODEL]
        o_ref[...] = x1_ref[...] + gate2 * _rms(acc_ref[...], gpost_ref[...])


def _ffn_call(h2, x1, mod3d, seq_len, wts):
    n_tok = h2.shape[0]
    tiles_per_seq = max(seq_len // FFN_TOK_TILE, 1)
    n_mod = mod3d.shape[0]
    n_col = D_FF // FFN_COL_TILE
    mod_map = (lambda i, j: (i // tiles_per_seq, 0, 0)) if n_mod > 1 else (lambda i, j: (0, 0, 0))
    tok = lambda: pl.BlockSpec((FFN_TOK_TILE, D_MODEL), lambda i, j: (i, 0))
    return pl.pallas_call(
        functools.partial(_ffn_kernel, seq_len=seq_len),
        grid=(n_tok // FFN_TOK_TILE, n_col),
        in_specs=[
            tok(), tok(), pl.BlockSpec((1, 1, 6 * D_MODEL), mod_map),
            pl.BlockSpec((D_MODEL, FFN_COL_TILE), lambda i, j: (0, j)),
            pl.BlockSpec((D_MODEL, FFN_COL_TILE), lambda i, j: (0, n_col + j)),
            pl.BlockSpec((3, FFN_COL_TILE), lambda i, j: (0, j)),
            pl.BlockSpec((3, FFN_COL_TILE), lambda i, j: (0, n_col + j)),
            pl.BlockSpec((1, FFN_COL_TILE), lambda i, j: (0, j)),
            pl.BlockSpec((1, FFN_COL_TILE), lambda i, j: (0, n_col + j)),
            pl.BlockSpec((FFN_COL_TILE, D_MODEL), lambda i, j: (j, 0)),
            pl.BlockSpec((1, D_MODEL), lambda i, j: (0, 0)),
        ],
        out_specs=tok(),
        out_shape=jax.ShapeDtypeStruct((n_tok, D_MODEL), F32),
        scratch_shapes=[pltpu.VMEM((FFN_TOK_TILE, D_MODEL), F32)],
        compiler_params=pltpu.CompilerParams(dimension_semantics=("arbitrary", "arbitrary"),
                                             vmem_limit_bytes=VMEM_LIMIT),
        name="conv_ffn",
    )(h2, x1, mod3d, wts["ffn_up"], wts["ffn_up"], wts["conv_w"], wts["conv_w"],
      wts["conv_b"], wts["conv_b"], wts["ffn_down"], wts["g_ffn_post"])


def _rope_tables(n_tok, n_heads):
    quarter = HEAD_DIM // 4
    pos = jnp.arange(n_tok)
    row = (pos // GRID_W).astype(F32)
    col = (pos % GRID_W).astype(F32)
    inv_freq = ROPE_THETA ** (-jnp.arange(quarter, dtype=F32) / quarter)

    def half(p):
        ang = p[:, None] * inv_freq[None, :]
        c, s = jnp.cos(ang), jnp.sin(ang)
        return jnp.concatenate([c, c], axis=-1), jnp.concatenate([-s, s], axis=-1)

    cr, sr = half(row)
    cc, sc = half(col)
    cos = jnp.concatenate([cr, cc], axis=-1)
    sin = jnp.concatenate([sr, sc], axis=-1)
    return jnp.tile(cos, (1, n_heads)), jnp.tile(sin, (1, n_heads))


def _token_major(y, n_batch, seq_len):
    return y.reshape(seq_len, HEAD_DIM, H_RWKV, n_batch).transpose(3, 0, 2, 1).reshape(
        n_batch * seq_len, D_RWKV)


def _mixer(pre, n_batch, seq_len, packed, k_a, s0=None):
    r, v, kk, k, w, a = pre[:6]
    shared = tuple(_to_chains_call(x, n_batch, seq_len, packed) for x in (r, kk, k, v))
    ka_tab = jnp.repeat(k_a.reshape(H_RWKV, HEAD_DIM).T, n_batch, axis=1)
    ka_tab = jnp.tile(ka_tab, (1, LANES // ka_tab.shape[1]))
    y_f, y_b, s_fin = _scan_call(shared, _to_chains_call(w, n_batch, seq_len, False),
                                 _to_chains_call(a, n_batch, seq_len, False), ka_tab, packed, s0)
    if packed:
        half = LANES // 2
        return (_token_major(y_f[0, :, :, :half], n_batch, seq_len),
                _token_major(y_b[0, :, :, half:], n_batch, seq_len), s_fin)
    return _token_major(y_f[0], n_batch, seq_len), _token_major(y_b[1], n_batch, seq_len), s_fin


def kernel(x_prompt, x_sample, cache_k, cache_v, state_rwkv, c, c_ctx, w_mod, b_mod, norm_mix_pre, norm_mix_post, norm_ffn_pre, norm_ffn_post, w_in, w0, w_up, a0, a_up, g_up, k_k, k_a, r_k, gn_w, gn_b, q_norm, k_norm, w_out, ffn_up, conv_w, conv_b, ffn_down):
    n_ctx, l_ctx = x_prompt.shape[0], x_prompt.shape[1]
    n_lat, l_lat = x_sample.shape[0], x_sample.shape[1]
    l_past = cache_k.shape[2]
    layer = 0
    assert n_ctx * H_RWKV == LANES and 2 * n_lat * H_RWKV == LANES

    zeros_w = jnp.zeros((LORA_W, D_RWKV), F32)
    w_lora = jnp.concatenate([
        jnp.concatenate([w_up[layer, 0], w_up[layer, 1], zeros_w, zeros_w], axis=1),
        jnp.concatenate([zeros_w, zeros_w, a_up[layer, 0], a_up[layer, 1]], axis=1)], axis=0)
    vec = jnp.stack([k_k[layer], k_a[layer], r_k[layer].reshape(D_RWKV), jnp.tile(q_norm[layer], H_Q)])
    wts = dict(
        g_pre=norm_mix_pre[layer].reshape(1, D_MODEL),
        g_post=norm_mix_post[layer].reshape(1, D_MODEL),
        g_ffn_pre=norm_ffn_pre[layer].reshape(1, D_MODEL),
        g_ffn_post=norm_ffn_post[layer].reshape(1, D_MODEL),
        w_in=w_in[layer].astype(BF16),
        w_lora=w_lora,
        g_up=g_up[layer].astype(BF16),
        vec=vec,
        wa0=jnp.stack([w0[layer].reshape(2 * D_RWKV), a0[layer].reshape(2 * D_RWKV)]),
        k_norm=jnp.tile(k_norm[layer], H_KV).reshape(1, D_KV),
        gn=jnp.stack([gn_w[layer], gn_b[layer]]),
        w_out=w_out[layer].astype(BF16),
        ffn_up=ffn_up[layer].astype(BF16),
        conv_w=conv_w[layer],
        conv_b=conv_b[layer].reshape(1, 2 * D_FF),
        ffn_down=ffn_down[layer].astype(BF16),
    )

    cvec = jnp.concatenate([c, c_ctx[None, :], jnp.zeros((16 - n_lat - 1, D_MODEL), F32)], axis=0)
    mod = _mod_call(cvec, w_mod[layer], b_mod[layer])
    mod_lat = mod[:n_lat].reshape(n_lat, 1, 6 * D_MODEL)
    mod_ctx = mod[n_lat:n_lat + 1].reshape(1, 1, 6 * D_MODEL)

    xc = x_prompt.reshape(n_ctx * l_ctx, D_MODEL)
    pre_c = _pre_call(xc, mod_ctx, l_ctx, wts, None)
    g_c, bonus_c, q_c, ka_c, va_c = pre_c[6:]
    yf_c, yb_c, s_c = _mixer(pre_c, n_ctx, l_ctx, False, k_a[layer])
    attn_c = _attn_call(q_c, ka_c, va_c, n_ctx, l_ctx, l_ctx)
    x1_c, h2_c = _post_call(xc, mod_ctx, l_ctx, yf_c, yb_c, bonus_c, g_c, attn_c, wts)
    out_c = _ffn_call(h2_c, x1_c, mod_ctx, l_ctx, wts)

    xl = x_sample.reshape(n_lat * l_lat, D_MODEL)
    tabs = _rope_tables(l_lat, H_Q) + _rope_tables(l_lat, H_KV)
    pre_l = _pre_call(xl, mod_lat, l_lat, wts, tabs)
    g_l, bonus_l, q_l, kr_l, va_l = pre_l[6:]
    s0 = state_rwkv[:, layer].transpose(4, 3, 1, 2, 0).reshape(1, HEAD_DIM, HEAD_DIM, LANES)
    yf_l, yb_l, _ = _mixer(pre_l, n_lat, l_lat, True, k_a[layer], s0)
    k_all = jnp.concatenate([kr_l.reshape(n_lat, l_lat, D_KV), cache_k[:, layer].reshape(n_lat, l_past, D_KV)],
                            axis=1).reshape(n_lat * (l_lat + l_past), D_KV)
    v_all = jnp.concatenate([va_l.reshape(n_lat, l_lat, D_KV), cache_v[:, layer].reshape(n_lat, l_past, D_KV)],
                            axis=1).reshape(n_lat * (l_lat + l_past), D_KV)
    attn_l = _attn_call(q_l, k_all, v_all, n_lat, l_lat, l_lat + l_past)
    x1_l, h2_l = _post_call(xl, mod_lat, l_lat, yf_l, yb_l, bonus_l, g_l, attn_l, wts)
    out_l = _ffn_call(h2_l, x1_l, mod_lat, l_lat, wts)

    y_prompt = out_c.reshape(n_ctx, l_ctx, D_MODEL)
    y_sample = out_l.reshape(n_lat, l_lat, D_MODEL)
    new_cache_k = ka_c.reshape(n_ctx, 1, l_ctx, H_KV, HEAD_DIM)
    new_cache_v = va_c.reshape(n_ctx, 1, l_ctx, H_KV, HEAD_DIM)
    new_state = s_c.reshape(2, HEAD_DIM, HEAD_DIM, H_RWKV, n_ctx).transpose(4, 0, 3, 2, 1)[:, None]
    return (y_prompt, y_sample, new_cache_k, new_cache_v, new_state)
```

```python
import functools

import jax
import jax.numpy as jnp
from jax import lax
from jax.experimental import pallas as pl
from jax.experimental.pallas import tpu as pltpu

F32 = jnp.float32
BF16 = jnp.bfloat16

D_MODEL = 1024
HEAD_DIM = 64
D_RWKV = 512
H_RWKV = 8
D_ATTN = 512
H_Q = 8
H_KV = 2
D_KV = 128
LORA_W = 64
LORA_A = 64
LORA_G = 128
D_FF = 2816
D_IN = 2560
GRID_W = 64
ROPE_THETA = 10000.0
RMS_EPS = 1e-6
GN_EPS = 64e-5

LANES = 128
SUBLANES = 8
TOK_TILE = 256
FFN_TOK_TILE = 1024
FFN_COL_TILE = 256
SCAN_STEPS = 32
SCAN_K_UNROLL = 16
RELAYOUT_TOK = 128
CHAIN_PITCH = 72
ATTN_Q_TILE = 256
MOD_COL_TILE = 768
VMEM_LIMIT = 56 * 1024 * 1024


def _dot(a, b):
    return jnp.dot(a, b, preferred_element_type=F32)


def _split_bf16(x):
    hi = x.astype(BF16)
    lo = (x - hi.astype(F32)).astype(BF16)
    return hi, lo


def _dot_exact_rhs(x, w_bf16):
    return _dot(x.astype(BF16), w_bf16)


def _dot3(a, b):
    ah, al = _split_bf16(a)
    bh, bl = _split_bf16(b)
    return _dot(ah, bh) + (_dot(ah, bl) + _dot(al, bh))


def _head_ones(n):
    shift = HEAD_DIM.bit_length() - 1
    r = lax.shift_right_logical(lax.broadcasted_iota(jnp.int32, (n, n), 0), shift)
    c = lax.shift_right_logical(lax.broadcasted_iota(jnp.int32, (n, n), 1), shift)
    return jnp.where(r == c, 1.0, 0.0).astype(BF16)


def _sigmoid(x):
    return 1.0 / (1.0 + jnp.exp(-x))


def _softplus(x):
    return jnp.maximum(x, 0.0) + jnp.log(1.0 + jnp.exp(-jnp.abs(x)))


def _rms(x, g):
    return x * lax.rsqrt(jnp.mean(x * x, axis=-1, keepdims=True) + RMS_EPS) * g


def _rope(x, cos, sgn_sin):
    n = x.shape[-1]
    lane = lax.broadcasted_iota(jnp.int32, x.shape, 1)
    partner = jnp.where((lane & 16) == 0, pltpu.roll(x, n - 16, 1), pltpu.roll(x, 16, 1))
    return x * cos + partner * sgn_sin


def _mod_kernel(c_ref, w_ref, b_ref, o_ref):
    c = c_ref[...]
    o_ref[...] = _dot3(c * _sigmoid(c), w_ref[...]) + b_ref[...]


def _mod_call(cvec, w_mod, b_mod):
    n = w_mod.shape[1]
    rows = cvec.shape[0]
    return pl.pallas_call(
        _mod_kernel,
        grid=(n // MOD_COL_TILE,),
        in_specs=[
            pl.BlockSpec((rows, D_MODEL), lambda j: (0, 0)),
            pl.BlockSpec((D_MODEL, MOD_COL_TILE), lambda j: (0, j)),
            pl.BlockSpec((1, MOD_COL_TILE), lambda j: (0, j)),
        ],
        out_specs=pl.BlockSpec((rows, MOD_COL_TILE), lambda j: (0, j)),
        out_shape=jax.ShapeDtypeStruct((rows, n), F32),
        compiler_params=pltpu.CompilerParams(dimension_semantics=("arbitrary",)),
        name="adaln_mod",
    )(cvec, w_mod, b_mod.reshape(1, n))


_V_KK, _V_KA, _V_RK, _V_QN = range(4)


def _pre_kernel(*refs, rope):
    if rope:
        (x_ref, mod_ref, gpre_ref, win_ref, wlora_ref, gup_ref, vec_ref, wa0_ref, kn_ref,
         cq_ref, sq_ref, ck_ref, sk_ref,
         r_o, v_o, kk_o, k_o, w_o, a_o, g_o, bonus_o, q_o, ka_o, va_o) = refs
    else:
        (x_ref, mod_ref, gpre_ref, win_ref, wlora_ref, gup_ref, vec_ref, wa0_ref, kn_ref,
         r_o, v_o, kk_o, k_o, w_o, a_o, g_o, bonus_o, q_o, ka_o, va_o) = refs

    x = x_ref[...]
    m = mod_ref[0]
    shift1 = m[:, 0:D_MODEL]
    scale1 = m[:, D_MODEL:2 * D_MODEL]
    h = _rms(x, gpre_ref[...]) * (1.0 + scale1) + shift1
    p = _dot(h.astype(BF16), win_ref[...])

    r = p[:, 0:512]
    k = p[:, 512:1024]
    v = p[:, 1024:1536]
    xwa = p[:, 1536:1664]
    xg = p[:, 1664:1792]
    q = p[:, 1792:2304]
    ka = p[:, 2304:2432]
    va = p[:, 2432:2560]

    vec = vec_ref[...]
    ones512 = _head_ones(D_RWKV)

    g_o[...] = _dot(_sigmoid(xg).astype(BF16), gup_ref[...])

    kx = k * vec[_V_KK:_V_KK + 1]
    kk_o[...] = kx * lax.rsqrt(_dot_exact_rhs(kx * kx, ones512) + 1e-12)

    lane = lax.broadcasted_iota(jnp.int32, xwa.shape, 1)
    lora_in = jnp.where(lane < LORA_W, jnp.tanh(xwa), xwa)
    lora = _dot3(lora_in, wlora_ref[...])

    z = wa0_ref[0:1] + lora[:, 0:2 * D_RWKV]
    logw = -_softplus(-z) - 0.5
    w_o[...] = jnp.exp(-jnp.exp(logw))
    a = _sigmoid(wa0_ref[1:2] + lora[:, 2 * D_RWKV:4 * D_RWKV])
    a_o[...] = a

    k_a = vec[_V_KA:_V_KA + 1]
    kd_sum = k * (1.0 + (a[:, 0:D_RWKV] - 1.0) * k_a) + k * (1.0 + (a[:, D_RWKV:2 * D_RWKV] - 1.0) * k_a)
    bonus_o[...] = _dot_exact_rhs(r * kd_sum * vec[_V_RK:_V_RK + 1], ones512) * v

    r_o[...] = r
    v_o[...] = v
    k_o[...] = k

    qn = q * lax.rsqrt(_dot_exact_rhs(q * q, ones512) * (1.0 / HEAD_DIM) + RMS_EPS) * vec[_V_QN:_V_QN + 1]
    kan = ka * lax.rsqrt(_dot_exact_rhs(ka * ka, _head_ones(D_KV)) * (1.0 / HEAD_DIM) + RMS_EPS) * kn_ref[...]
    if rope:
        qn = _rope(qn, cq_ref[...], sq_ref[...])
        kan = _rope(kan, ck_ref[...], sk_ref[...])
    q_o[...] = qn
    ka_o[...] = kan
    va_o[...] = va


def _pre_call(x2d, mod3d, seq_len, wts, rope_tabs):
    n_tok = x2d.shape[0]
    tiles_per_seq = seq_len // TOK_TILE
    n_mod = mod3d.shape[0]
    rope = rope_tabs is not None

    def const(shape):
        return pl.BlockSpec(shape, lambda i: tuple(0 for _ in shape))

    mod_map = (lambda i: (i // tiles_per_seq, 0, 0)) if n_mod > 1 else (lambda i: (0, 0, 0))
    in_specs = [
        pl.BlockSpec((TOK_TILE, D_MODEL), lambda i: (i, 0)),
        pl.BlockSpec((1, 1, 6 * D_MODEL), mod_map),
        const((1, D_MODEL)),
        const((D_MODEL, D_IN)),
        const((LORA_W + LORA_A, 4 * D_RWKV)),
        const((LORA_G, D_RWKV)),
        const((4, D_RWKV)),
        const((2, 2 * D_RWKV)),
        const((1, D_KV)),
    ]
    args = [x2d, mod3d, wts["g_pre"], wts["w_in"], wts["w_lora"], wts["g_up"], wts["vec"], wts["wa0"],
            wts["k_norm"]]
    if rope:
        tab_map = lambda i: (i % tiles_per_seq, 0)
        in_specs += [pl.BlockSpec((TOK_TILE, D_ATTN), tab_map), pl.BlockSpec((TOK_TILE, D_ATTN), tab_map),
                     pl.BlockSpec((TOK_TILE, D_KV), tab_map), pl.BlockSpec((TOK_TILE, D_KV), tab_map)]
        args += list(rope_tabs)

    widths = [512] * 4 + [1024, 1024] + [512] * 3 + [D_KV, D_KV]
    out_specs = [pl.BlockSpec((TOK_TILE, wd), lambda i: (i, 0)) for wd in widths]
    out_shape = [jax.ShapeDtypeStruct((n_tok, wd), F32) for wd in widths]
    return pl.pallas_call(
        functools.partial(_pre_kernel, rope=rope),
        grid=(n_tok // TOK_TILE,),
        in_specs=in_specs,
        out_specs=out_specs,
        out_shape=out_shape,
        compiler_params=pltpu.CompilerParams(dimension_semantics=("arbitrary",),
                                             vmem_limit_bytes=VMEM_LIMIT),
        name="pre_rope" if rope else "pre",
    )(*args)


def _to_chains_kernel(x_ref, o_ref, slab_ref, *, n_batch, dup):
    n_col = x_ref.shape[2] // HEAD_DIM
    n_grp = o_ref.shape[0]

    def stage1(b, carry):
        for jc in range(n_col // 2):
            a_t = x_ref[b, :, LANES * jc:LANES * (jc + 1)].T
            for h2 in range(2):
                row0 = pl.multiple_of(((2 * jc + h2) * n_batch + b) * CHAIN_PITCH, SUBLANES)
                slab_ref[pl.ds(row0, HEAD_DIM), :] = a_t[HEAD_DIM * h2:HEAD_DIM * (h2 + 1), :]
        return carry

    lax.fori_loop(0, n_batch, stage1, 0)

    def stage2(kb, carry):
        for ki in range(SUBLANES):
            k = kb * SUBLANES + ki
            for d in range(n_grp):
                if dup:
                    half = slab_ref[pl.ds(k, LANES // 2, stride=CHAIN_PITCH), :]
                    tile = jnp.concatenate([half, half], axis=0)
                else:
                    tile = slab_ref[pl.ds(d * LANES * CHAIN_PITCH + k, LANES, stride=CHAIN_PITCH), :]
                o_ref.at[d][pl.ds(k, RELAYOUT_TOK, stride=CHAIN_PITCH), :] = tile.T
        return carry

    lax.fori_loop(0, HEAD_DIM // SUBLANES, stage2, 0)

    zero = jnp.zeros((RELAYOUT_TOK, LANES), F32)
    for d in range(n_grp):
        for pad in range(HEAD_DIM, CHAIN_PITCH):
            o_ref.at[d][pl.ds(pad, RELAYOUT_TOK, stride=CHAIN_PITCH), :] = zero


def _to_chains_call(x2d, n_batch, seq_len, dup):
    n_chan = x2d.shape[1]
    n_slab = (n_chan // HEAD_DIM) * n_batch
    n_grp = n_slab * (2 if dup else 1) // LANES
    out = pl.pallas_call(
        functools.partial(_to_chains_kernel, n_batch=n_batch, dup=dup),
        grid=(seq_len // RELAYOUT_TOK,),
        in_specs=[pl.BlockSpec((n_batch, RELAYOUT_TOK, n_chan), lambda i: (0, i, 0))],
        out_specs=pl.BlockSpec((n_grp, RELAYOUT_TOK * CHAIN_PITCH, LANES), lambda i: (0, i, 0)),
        out_shape=jax.ShapeDtypeStruct((n_grp, seq_len * CHAIN_PITCH, LANES), F32),
        scratch_shapes=[pltpu.VMEM((n_slab * CHAIN_PITCH, LANES), F32)],
        compiler_params=pltpu.CompilerParams(dimension_semantics=("arbitrary",),
                                             vmem_limit_bytes=VMEM_LIMIT),
        name="to_chains_dup" if dup else "to_chains",
    )(x2d.reshape(n_batch, seq_len, n_chan))
    return out.reshape(n_grp, seq_len, CHAIN_PITCH, LANES)


def _from_chains_kernel(yf_ref, yb_ref, o_ref, slab_ref, *, n_batch, packed):
    def stage1(vb, carry):
        for vi in range(SUBLANES):
            v = vb * SUBLANES + vi
            tf = yf_ref[pl.ds(v, RELAYOUT_TOK, stride=CHAIN_PITCH), :]
            tb = yb_ref[pl.ds(v, RELAYOUT_TOK, stride=CHAIN_PITCH), :]
            if packed:
                tb = pltpu.roll(tb, LANES // 2, 1)
            slab_ref[pl.ds(v, LANES, stride=CHAIN_PITCH), :] = (tf + tb).T
        return carry

    lax.fori_loop(0, HEAD_DIM // SUBLANES, stage1, 0)

    def stage2(b, carry):
        for jc in range(H_RWKV // 2):
            rows = []
            for h2 in range(2):
                row0 = pl.multiple_of(((2 * jc + h2) * n_batch + b) * CHAIN_PITCH, SUBLANES)
                rows.append(slab_ref[pl.ds(row0, HEAD_DIM), :])
            o_ref[b, :, LANES * jc:LANES * (jc + 1)] = jnp.concatenate(rows, axis=0).T
        return carry

    lax.fori_loop(0, n_batch, stage2, 0)


def _from_chains_call(y_f, y_b, n_batch, seq_len, packed):
    n_rows = seq_len * CHAIN_PITCH
    grp_b = 0 if packed else 1
    blk = (None, RELAYOUT_TOK * CHAIN_PITCH, LANES)
    out = pl.pallas_call(
        functools.partial(_from_chains_kernel, n_batch=n_batch, packed=packed),
        grid=(seq_len // RELAYOUT_TOK,),
        in_specs=[pl.BlockSpec(blk, lambda i: (0, i, 0)), pl.BlockSpec(blk, lambda i: (grp_b, i, 0))],
        out_specs=pl.BlockSpec((n_batch, RELAYOUT_TOK, D_RWKV), lambda i: (0, i, 0)),
        out_shape=jax.ShapeDtypeStruct((n_batch, seq_len, D_RWKV), F32),
        scratch_shapes=[pltpu.VMEM((LANES * CHAIN_PITCH, LANES), F32)],
        compiler_params=pltpu.CompilerParams(dimension_semantics=("arbitrary",),
                                             vmem_limit_bytes=VMEM_LIMIT),
        name="from_chains_packed" if packed else "from_chains",
    )(y_f.reshape(-1, n_rows, LANES), y_b.reshape(-1, n_rows, LANES))
    return out.reshape(n_batch * seq_len, D_RWKV)


def _scan_kernel(*refs, packed, has_init):
    (rF, rB, kkF, kkB, kF, kB, vF, vB, wF, wB, aF, aB, ka_ref), rest = refs[:13], refs[13:]
    if has_init:
        s0_ref, rest = rest[0], rest[1:]
    yF, yB, sfin_ref, s_scr, sa_scr, op_scr = rest
    g = pl.program_id(0)
    j = pl.program_id(1)
    tile = (HEAD_DIM, LANES)

    if packed:
        use_b = lax.broadcasted_iota(jnp.int32, tile, 1) >= LANES // 2
    else:
        use_b = jnp.full(tile, g, jnp.int32) == 1

    def pick(f_ref, b_ref, s):
        return jnp.where(use_b, b_ref[SCAN_STEPS - 1 - s, 0:HEAD_DIM, :], f_ref[s, 0:HEAD_DIM, :])

    @pl.when(j == 0)
    def _():
        if has_init:
            s_scr[...] = s0_ref[...]
        else:
            s_scr[...] = jnp.zeros(s_scr.shape, F32)

    op_scr[4] = pick(kkF, kkB, 0)
    acc = jnp.zeros(tile, F32)
    for k in range(HEAD_DIM):
        acc = acc + s_scr[k] * op_scr[4, k:k + 1, :]
    sa_scr[...] = acc

    ka = ka_ref[...]
    pad = jnp.zeros((SCAN_STEPS, CHAIN_PITCH - HEAD_DIM, LANES), F32)
    yF[:, HEAD_DIM:CHAIN_PITCH, :] = pad
    yB[:, HEAD_DIM:CHAIN_PITCH, :] = pad

    def step(s, carry):
        a = pick(aF, aB, s)
        op_scr[0] = pick(wF, wB, s)
        op_scr[1] = pick(kkF, kkB, s) * a
        op_scr[2] = pick(kF, kB, s) * (1.0 + (a - 1.0) * ka)
        op_scr[3] = pick(rF, rB, s)
        op_scr[4] = pick(kkF, kkB, jnp.minimum(s + 1, SCAN_STEPS - 1))
        vt = pick(vF, vB, s)
        sa = sa_scr[...]

        def kblock(kb, acc):
            y, san = acc
            for ki in range(SCAN_K_UNROLL):
                k = kb * SCAN_K_UNROLL + ki
                wk = op_scr[0, pl.ds(k, 1), :]
                bk = op_scr[1, pl.ds(k, 1), :]
                kdk = op_scr[2, pl.ds(k, 1), :]
                rk = op_scr[3, pl.ds(k, 1), :]
                kn = op_scr[4, pl.ds(k, 1), :]
                sn = s_scr[k] * wk - sa * bk + vt * kdk
                s_scr[k] = sn
                y = y + sn * rk
                san = san + sn * kn
            return y, san

        zero = jnp.zeros(tile, F32)
        y, san = lax.fori_loop(0, HEAD_DIM // SCAN_K_UNROLL, kblock, (zero, zero))
        yF[s, 0:HEAD_DIM, :] = y
        yB[SCAN_STEPS - 1 - s, 0:HEAD_DIM, :] = y
        sa_scr[...] = san
        return carry

    lax.fori_loop(0, SCAN_STEPS, step, 0)

    @pl.when(j == pl.num_programs(1) - 1)
    def _():
        sfin_ref[...] = s_scr[...]


def _scan_call(shared, w, a, ka_tab, packed, s0=None):
    n_step = w.shape[1]
    n_grp = 1 if packed else 2
    n_blk = n_step // SCAN_STEPS
    has_init = s0 is not None
    blk = (None, SCAN_STEPS, CHAIN_PITCH, LANES)
    if packed:
        f_map = lambda g, j: (0, j, 0, 0)
        b_map = lambda g, j: (0, n_blk - 1 - j, 0, 0)
        fd_map, bd_map = f_map, b_map
    else:
        f_map = lambda g, j: (0, jnp.where(g == 0, j, 0), 0, 0)
        b_map = lambda g, j: (0, jnp.where(g == 1, n_blk - 1 - j, 0), 0, 0)
        fd_map = lambda g, j: (0, jnp.where(g == 0, j, 0), 0, 0)
        bd_map = lambda g, j: (1, jnp.where(g == 1, n_blk - 1 - j, 0), 0, 0)
    in_specs, args = [], []
    for x in shared:
        in_specs += [pl.BlockSpec(blk, f_map), pl.BlockSpec(blk, b_map)]
        args += [x, x]
    for x in (w, a):
        in_specs += [pl.BlockSpec(blk, fd_map), pl.BlockSpec(blk, bd_map)]
        args += [x, x]
    in_specs.append(pl.BlockSpec((HEAD_DIM, LANES), lambda g, j: (0, 0)))
    args.append(ka_tab)
    state_spec = pl.BlockSpec((None, HEAD_DIM, HEAD_DIM, LANES), lambda g, j: (g, 0, 0, 0))
    if has_init:
        in_specs.append(state_spec)
        args.append(s0)
    y_shape = jax.ShapeDtypeStruct((n_grp, n_step, CHAIN_PITCH, LANES), F32)
    return pl.pallas_call(
        functools.partial(_scan_kernel, packed=packed, has_init=has_init),
        grid=(n_grp, n_blk),
        in_specs=in_specs,
        out_specs=[pl.BlockSpec(blk, lambda g, j: (g, j, 0, 0)),
                   pl.BlockSpec(blk, lambda g, j: (g, n_blk - 1 - j, 0, 0)),
                   state_spec],
        out_shape=[y_shape, y_shape, jax.ShapeDtypeStruct((n_grp, HEAD_DIM, HEAD_DIM, LANES), F32)],
        scratch_shapes=[pltpu.VMEM((HEAD_DIM, HEAD_DIM, LANES), F32), pltpu.VMEM((HEAD_DIM, LANES), F32),
                        pltpu.VMEM((5, HEAD_DIM, LANES), F32)],
        compiler_params=pltpu.CompilerParams(dimension_semantics=("arbitrary", "arbitrary"),
                                             vmem_limit_bytes=VMEM_LIMIT),
        name="wkv_scan_packed" if packed else "wkv_scan",
    )(*args)


def _attn_kernel(q_ref, k_ref, v_ref, o_ref):
    q = q_ref[...]
    k = k_ref[...]
    v = v_ref[...]
    group = H_Q // H_KV
    for kvh in range(H_KV):
        kh = k[:, HEAD_DIM * kvh:HEAD_DIM * (kvh + 1)].astype(BF16)
        vh = v[:, HEAD_DIM * kvh:HEAD_DIM * (kvh + 1)].astype(BF16)
        for gq in range(group):
            hq = kvh * group + gq
            qh = (q[:, HEAD_DIM * hq:HEAD_DIM * (hq + 1)] * (HEAD_DIM ** -0.5)).astype(BF16)
            s = lax.dot_general(qh, kh, (((1,), (1,)), ((), ())), preferred_element_type=F32)
            e = jnp.exp(s - jnp.max(s, axis=-1, keepdims=True))
            o = _dot(e.astype(BF16), vh) / jnp.sum(e, axis=-1, keepdims=True)
            o_ref[:, HEAD_DIM * hq:HEAD_DIM * (hq + 1)] = o


def _attn_call(q2d, k2d, v2d, n_batch, lq, lk):
    q_tiles = lq // ATTN_Q_TILE
    return pl.pallas_call(
        _attn_kernel,
        grid=(n_batch, q_tiles),
        in_specs=[
            pl.BlockSpec((ATTN_Q_TILE, D_ATTN), lambda b, i: (b * q_tiles + i, 0)),
            pl.BlockSpec((lk, D_KV), lambda b, i: (b, 0)),
            pl.BlockSpec((lk, D_KV), lambda b, i: (b, 0)),
        ],
        out_specs=pl.BlockSpec((ATTN_Q_TILE, D_ATTN), lambda b, i: (b * q_tiles + i, 0)),
        out_shape=jax.ShapeDtypeStruct((n_batch * lq, D_ATTN), F32),
        compiler_params=pltpu.CompilerParams(dimension_semantics=("arbitrary", "arbitrary"),
                                             vmem_limit_bytes=VMEM_LIMIT),
        name="gqa_attn",
    )(q2d, k2d, v2d)


def _post_kernel(x_ref, mod_ref, ys_ref, bonus_ref, g_ref, attn_ref,
                 gn_ref, wout_ref, gpost_ref, gffn_ref, x1_o, h2_o):
    ones512 = _head_ones(D_RWKV)
    ys = ys_ref[...]
    ys_hi, ys_lo = _split_bf16(ys)
    mu = (_dot(ys_hi, ones512) + _dot(ys_lo, ones512)) * (1.0 / HEAD_DIM)
    dlt = ys - mu
    var = _dot_exact_rhs(dlt * dlt, ones512) * (1.0 / HEAD_DIM)
    gn = gn_ref[...]
    yn = dlt * lax.rsqrt(var + GN_EPS) * gn[0:1] + gn[1:2]
    yr = (yn + bonus_ref[...]) * g_ref[...]
    mix = (_dot(yr.astype(BF16), wout_ref[0:D_RWKV, :])
           + _dot(attn_ref[...].astype(BF16), wout_ref[D_RWKV:D_MODEL, :]))
    m = mod_ref[0]
    gate1 = m[:, 2 * D_MODEL:3 * D_MODEL]
    shift2 = m[:, 3 * D_MODEL:4 * D_MODEL]
    scale2 = m[:, 4 * D_MODEL:5 * D_MODEL]
    x1 = x_ref[...] + gate1 * _rms(mix, gpost_ref[...])
    x1_o[...] = x1
    h2_o[...] = (_rms(x1, gffn_ref[...]) * (1.0 + scale2) + shift2).astype(BF16)


def _post_call(x2d, mod3d, seq_len, ys, bonus, g, attn, wts):
    n_tok = x2d.shape[0]
    tiles_per_seq = seq_len // TOK_TILE
    n_mod = mod3d.shape[0]

    def const(shape):
        return pl.BlockSpec(shape, lambda i: tuple(0 for _ in shape))

    mod_map = (lambda i: (i // tiles_per_seq, 0, 0)) if n_mod > 1 else (lambda i: (0, 0, 0))
    tok = lambda wd: pl.BlockSpec((TOK_TILE, wd), lambda i: (i, 0))
    return pl.pallas_call(
        _post_kernel,
        grid=(n_tok // TOK_TILE,),
        in_specs=[tok(D_MODEL), pl.BlockSpec((1, 1, 6 * D_MODEL), mod_map),
                  tok(512), tok(512), tok(512), tok(512),
                  const((2, D_RWKV)), const((D_MODEL, D_MODEL)), const((1, D_MODEL)), const((1, D_MODEL))],
        out_specs=[tok(D_MODEL), tok(D_MODEL)],
        out_shape=[jax.ShapeDtypeStruct((n_tok, D_MODEL), F32), jax.ShapeDtypeStruct((n_tok, D_MODEL), BF16)],
        compiler_params=pltpu.CompilerParams(dimension_semantics=("arbitrary",),
                                             vmem_limit_bytes=VMEM_LIMIT),
        name="mix_post",
    )(x2d, mod3d, ys, bonus, g, attn, wts["gn"], wts["w_out"], wts["g_post"], wts["g_ffn_pre"])


def _ffn_kernel(h2_ref, x1_ref, mod_ref, wg_ref, wv_ref, cwg_ref, cwv_ref, cbg_ref, cbv_ref,
                wd_ref, gpost_ref, o_ref, acc_ref, *, seq_len):
    j = pl.program_id(1)

    @pl.when(j == 0)
    def _():
        acc_ref[...] = jnp.zeros(acc_ref.shape, F32)

    h2 = h2_ref[...]
    n_rows = h2.shape[0]
    assert seq_len & (seq_len - 1) == 0
    row = lax.broadcasted_iota(jnp.int32, (n_rows, FFN_COL_TILE), 0) & (seq_len - 1)
    has_prev = row != 0
    has_next = row != seq_len - 1

    def conv(u, cw_ref, cb_ref):
        cw = cw_ref[...]
        prev = jnp.where(has_prev, pltpu.roll(u, 1, 0), 0.0)
        nxt = jnp.where(has_next, pltpu.roll(u, n_rows - 1, 0), 0.0)
        return ((cb_ref[...] + prev * cw[0:1]) + u * cw[1:2]) + nxt * cw[2:3]

    gate = conv(_dot(h2, wg_ref[...]), cwg_ref, cbg_ref)
    val = conv(_dot(h2, wv_ref[...]), cwv_ref, cbv_ref)
    act = (gate * _sigmoid(gate) * val).astype(BF16)
    acc_ref[...] += _dot(act, wd_ref[...])

    @pl.when(j == pl.num_programs(1) - 1)
    def _():
        m = mod_ref[0]
        gate2 = m[:, 5 * D_MODEL:6 * D_MODEL]
        o_ref[...] = x1_ref[...] + gate2 * _rms(acc_ref[...], gpost_ref[...])


def _ffn_call(h2, x1, mod3d, seq_len, wts):
    n_tok = h2.shape[0]
    tiles_per_seq = max(seq_len // FFN_TOK_TILE, 1)
    n_mod = mod3d.shape[0]
    n_col = D_FF // FFN_COL_TILE
    mod_map = (lambda i, j: (i // tiles_per_seq, 0, 0)) if n_mod > 1 else (lambda i, j: (0, 0, 0))
    tok = lambda: pl.BlockSpec((FFN_TOK_TILE, D_MODEL), lambda i, j: (i, 0))
    return pl.pallas_call(
        functools.partial(_ffn_kernel, seq_len=seq_len),
        grid=(n_tok // FFN_TOK_TILE, n_col),
        in_specs=[
            tok(), tok(), pl.BlockSpec((1, 1, 6 * D_MODEL), mod_map),
            pl.BlockSpec((D_MODEL, FFN_COL_TILE), lambda i, j: (0, j)),
            pl.BlockSpec((D_MODEL, FFN_COL_TILE), lambda i, j: (0, n_col + j)),
            pl.BlockSpec((3, FFN_COL_TILE), lambda i, j: (0, j)),
            pl.BlockSpec((3, FFN_COL_TILE), lambda i, j: (0, n_col + j)),
            pl.BlockSpec((1, FFN_COL_TILE), lambda i, j: (0, j)),
            pl.BlockSpec((1, FFN_COL_TILE), lambda i, j: (0, n_col + j)),
            pl.BlockSpec((FFN_COL_TILE, D_MODEL), lambda i, j: (j, 0)),
            pl.BlockSpec((1, D_MODEL), lambda i, j: (0, 0)),
        ],
        out_specs=tok(),
        out_shape=jax.ShapeDtypeStruct((n_tok, D_MODEL), F32),
        scratch_shapes=[pltpu.VMEM((FFN_TOK_TILE, D_MODEL), F32)],
        compiler_params=pltpu.CompilerParams(dimension_semantics=("arbitrary", "arbitrary"),
                                             vmem_limit_bytes=VMEM_LIMIT),
        name="conv_ffn",
    )(h2, x1, mod3d, wts["ffn_up"], wts["ffn_up"], wts["conv_w"], wts["conv_w"],
      wts["conv_b"], wts["conv_b"], wts["ffn_down"], wts["g_ffn_post"])


def _rope_tables(n_tok, n_heads):
    quarter = HEAD_DIM // 4
    pos = jnp.arange(n_tok)
    row = (pos // GRID_W).astype(F32)
    col = (pos % GRID_W).astype(F32)
    inv_freq = ROPE_THETA ** (-jnp.arange(quarter, dtype=F32) / quarter)

    def half(p):
        ang = p[:, None] * inv_freq[None, :]
        c, s = jnp.cos(ang), jnp.sin(ang)
        return jnp.concatenate([c, c], axis=-1), jnp.concatenate([-s, s], axis=-1)

    cr, sr = half(row)
    cc, sc = half(col)
    cos = jnp.concatenate([cr, cc], axis=-1)
    sin = jnp.concatenate([sr, sc], axis=-1)
    return jnp.tile(cos, (1, n_heads)), jnp.tile(sin, (1, n_heads))


def _mixer(pre, n_batch, seq_len, packed, k_a, s0=None):
    r, v, kk, k, w, a = pre[:6]
    shared = tuple(_to_chains_call(x, n_batch, seq_len, packed) for x in (r, kk, k, v))
    ka_tab = jnp.repeat(k_a.reshape(H_RWKV, HEAD_DIM).T, n_batch, axis=1)
    ka_tab = jnp.tile(ka_tab, (1, LANES // ka_tab.shape[1]))
    y_f, y_b, s_fin = _scan_call(shared, _to_chains_call(w, n_batch, seq_len, False),
                                 _to_chains_call(a, n_batch, seq_len, False), ka_tab, packed, s0)
    return _from_chains_call(y_f, y_b, n_batch, seq_len, packed), s_fin


def kernel(x_prompt, x_sample, cache_k, cache_v, state_rwkv, c, c_ctx, w_mod, b_mod, norm_mix_pre, norm_mix_post, norm_ffn_pre, norm_ffn_post, w_in, w0, w_up, a0, a_up, g_up, k_k, k_a, r_k, gn_w, gn_b, q_norm, k_norm, w_out, ffn_up, conv_w, conv_b, ffn_down):
    n_ctx, l_ctx = x_prompt.shape[0], x_prompt.shape[1]
    n_lat, l_lat = x_sample.shape[0], x_sample.shape[1]
    l_past = cache_k.shape[2]
    layer = 0
    assert n_ctx * H_RWKV == LANES and 2 * n_lat * H_RWKV == LANES

    zeros_w = jnp.zeros((LORA_W, D_RWKV), F32)
    w_lora = jnp.concatenate([
        jnp.concatenate([w_up[layer, 0], w_up[layer, 1], zeros_w, zeros_w], axis=1),
        jnp.concatenate([zeros_w, zeros_w, a_up[layer, 0], a_up[layer, 1]], axis=1)], axis=0)
    vec = jnp.stack([k_k[layer], k_a[layer], r_k[layer].reshape(D_RWKV), jnp.tile(q_norm[layer], H_Q)])
    wts = dict(
        g_pre=norm_mix_pre[layer].reshape(1, D_MODEL),
        g_post=norm_mix_post[layer].reshape(1, D_MODEL),
        g_ffn_pre=norm_ffn_pre[layer].reshape(1, D_MODEL),
        g_ffn_post=norm_ffn_post[layer].reshape(1, D_MODEL),
        w_in=w_in[layer].astype(BF16),
        w_lora=w_lora,
        g_up=g_up[layer].astype(BF16),
        vec=vec,
        wa0=jnp.stack([w0[layer].reshape(2 * D_RWKV), a0[layer].reshape(2 * D_RWKV)]),
        k_norm=jnp.tile(k_norm[layer], H_KV).reshape(1, D_KV),
        gn=jnp.stack([gn_w[layer], gn_b[layer]]),
        w_out=w_out[layer].astype(BF16),
        ffn_up=ffn_up[layer].astype(BF16),
        conv_w=conv_w[layer],
        conv_b=conv_b[layer].reshape(1, 2 * D_FF),
        ffn_down=ffn_down[layer].astype(BF16),
    )

    cvec = jnp.concatenate([c, c_ctx[None, :], jnp.zeros((16 - n_lat - 1, D_MODEL), F32)], axis=0)
    mod = _mod_call(cvec, w_mod[layer], b_mod[layer])
    mod_lat = mod[:n_lat].reshape(n_lat, 1, 6 * D_MODEL)
    mod_ctx = mod[n_lat:n_lat + 1].reshape(1, 1, 6 * D_MODEL)

    xc = x_prompt.reshape(n_ctx * l_ctx, D_MODEL)
    pre_c = _pre_call(xc, mod_ctx, l_ctx, wts, None)
    g_c, bonus_c, q_c, ka_c, va_c = pre_c[6:]
    ys_c, s_c = _mixer(pre_c, n_ctx, l_ctx, False, k_a[layer])
    attn_c = _attn_call(q_c, ka_c, va_c, n_ctx, l_ctx, l_ctx)
    x1_c, h2_c = _post_call(xc, mod_ctx, l_ctx, ys_c, bonus_c, g_c, attn_c, wts)
    out_c = _ffn_call(h2_c, x1_c, mod_ctx, l_ctx, wts)

    xl = x_sample.reshape(n_lat * l_lat, D_MODEL)
    tabs = _rope_tables(l_lat, H_Q) + _rope_tables(l_lat, H_KV)
    pre_l = _pre_call(xl, mod_lat, l_lat, wts, tabs)
    g_l, bonus_l, q_l, kr_l, va_l = pre_l[6:]
    s0 = state_rwkv[:, layer].transpose(4, 3, 1, 2, 0).reshape(1, HEAD_DIM, HEAD_DIM, LANES)
    ys_l, _ = _mixer(pre_l, n_lat, l_lat, True, k_a[layer], s0)
    k_all = jnp.concatenate([kr_l.reshape(n_lat, l_lat, D_KV), cache_k[:, layer].reshape(n_lat, l_past, D_KV)],
                            axis=1).reshape(n_lat * (l_lat + l_past), D_KV)
    v_all = jnp.concatenate([va_l.reshape(n_lat, l_lat, D_KV), cache_v[:, layer].reshape(n_lat, l_past, D_KV)],
                            axis=1).reshape(n_lat * (l_lat + l_past), D_KV)
    attn_l = _attn_call(q_l, k_all, v_all, n_lat, l_lat, l_lat + l_past)
    x1_l, h2_l = _post_call(xl, mod_lat, l_lat, ys_l, bonus_l, g_l, attn_l, wts)
    out_l = _ffn_call(h2_l, x1_l, mod_lat, l_lat, wts)

    y_prompt = out_c.reshape(n_ctx, l_ctx, D_MODEL)
    y_sample = out_l.reshape(n_lat, l_lat, D_MODEL)
    new_cache_k = ka_c.reshape(n_ctx, 1, l_ctx, H_KV, HEAD_DIM)
    new_cache_v = va_c.reshape(n_ctx, 1, l_ctx, H_KV, HEAD_DIM)
    new_state = s_c.reshape(2, HEAD_DIM, HEAD_DIM, H_RWKV, n_ctx).transpose(4, 0, 3, 2, 1)[:, None]
    return (y_prompt, y_sample, new_cache_k, new_cache_v, new_state)
```

```python
import functools

import jax
import jax.numpy as jnp
from jax import lax
from jax.experimental import pallas as pl
from jax.experimental.pallas import tpu as pltpu

F32 = jnp.float32
BF16 = jnp.bfloat16

D_MODEL = 1024
HEAD_DIM = 64
D_RWKV = 512
H_RWKV = 8
D_ATTN = 512
H_Q = 8
H_KV = 2
D_KV = 128
LORA_W = 64
LORA_A = 64
LORA_G = 128
D_FF = 2816
D_IN = 2560
GRID_W = 64
ROPE_THETA = 10000.0
RMS_EPS = 1e-6
GN_EPS = 64e-5

LANES = 128
SUBLANES = 8
TOK_TILE = 256
FFN_TOK_TILE = 1024
FFN_COL_TILE = 512
FFN_SUB_TILE = 256
D_FF_PAD = -(-D_FF // FFN_COL_TILE) * FFN_COL_TILE
SCAN_STEPS = 32
SCAN_K_UNROLL = 16
RELAYOUT_TOK = 128
CHAIN_PITCH = 72
ATTN_Q_TILE = 256
MOD_COL_TILE = 768
VMEM_LIMIT = 56 * 1024 * 1024


def _dot(a, b):
    return jnp.dot(a, b, preferred_element_type=F32)


def _split_bf16(x):
    hi = x.astype(BF16)
    lo = (x - hi.astype(F32)).astype(BF16)
    return hi, lo


def _dot_exact_rhs(x, w_bf16):
    return _dot(x.astype(BF16), w_bf16)


def _dot3(a, b):
    ah, al = _split_bf16(a)
    bh, bl = _split_bf16(b)
    return _dot(ah, bh) + (_dot(ah, bl) + _dot(al, bh))


def _head_ones(n):
    shift = HEAD_DIM.bit_length() - 1
    r = lax.shift_right_logical(lax.broadcasted_iota(jnp.int32, (n, n), 0), shift)
    c = lax.shift_right_logical(lax.broadcasted_iota(jnp.int32, (n, n), 1), shift)
    return jnp.where(r == c, 1.0, 0.0).astype(BF16)


def _sigmoid(x):
    return 1.0 / (1.0 + jnp.exp(-x))


def _softplus(x):
    return jnp.maximum(x, 0.0) + jnp.log(1.0 + jnp.exp(-jnp.abs(x)))


def _rms(x, g):
    return x * lax.rsqrt(jnp.mean(x * x, axis=-1, keepdims=True) + RMS_EPS) * g


def _rope(x, cos, sgn_sin):
    n = x.shape[-1]
    lane = lax.broadcasted_iota(jnp.int32, x.shape, 1)
    partner = jnp.where((lane & 16) == 0, pltpu.roll(x, n - 16, 1), pltpu.roll(x, 16, 1))
    return x * cos + partner * sgn_sin


def _mod_kernel(c_ref, w_ref, b_ref, o_ref):
    c = c_ref[...]
    o_ref[...] = _dot3(c * _sigmoid(c), w_ref[...]) + b_ref[...]


def _mod_call(cvec, w_mod, b_mod):
    n = w_mod.shape[1]
    rows = cvec.shape[0]
    return pl.pallas_call(
        _mod_kernel,
        grid=(n // MOD_COL_TILE,),
        in_specs=[
            pl.BlockSpec((rows, D_MODEL), lambda j: (0, 0)),
            pl.BlockSpec((D_MODEL, MOD_COL_TILE), lambda j: (0, j)),
            pl.BlockSpec((1, MOD_COL_TILE), lambda j: (0, j)),
        ],
        out_specs=pl.BlockSpec((rows, MOD_COL_TILE), lambda j: (0, j)),
        out_shape=jax.ShapeDtypeStruct((rows, n), F32),
        compiler_params=pltpu.CompilerParams(dimension_semantics=("arbitrary",)),
        name="adaln_mod",
    )(cvec, w_mod, b_mod.reshape(1, n))


_V_KK, _V_KA, _V_RK, _V_QN = range(4)


def _pre_kernel(*refs, rope):
    if rope:
        (x_ref, mod_ref, gpre_ref, win_ref, wlora_ref, gup_ref, vec_ref, wa0_ref, kn_ref,
         cq_ref, sq_ref, ck_ref, sk_ref,
         r_o, v_o, kk_o, k_o, w_o, a_o, g_o, bonus_o, q_o, ka_o, va_o) = refs
    else:
        (x_ref, mod_ref, gpre_ref, win_ref, wlora_ref, gup_ref, vec_ref, wa0_ref, kn_ref,
         r_o, v_o, kk_o, k_o, w_o, a_o, g_o, bonus_o, q_o, ka_o, va_o) = refs

    x = x_ref[...]
    m = mod_ref[0]
    shift1 = m[:, 0:D_MODEL]
    scale1 = m[:, D_MODEL:2 * D_MODEL]
    h = _rms(x, gpre_ref[...]) * (1.0 + scale1) + shift1
    p = _dot(h.astype(BF16), win_ref[...])

    r = p[:, 0:512]
    k = p[:, 512:1024]
    v = p[:, 1024:1536]
    xwa = p[:, 1536:1664]
    xg = p[:, 1664:1792]
    q = p[:, 1792:2304]
    ka = p[:, 2304:2432]
    va = p[:, 2432:2560]

    vec = vec_ref[...]
    ones512 = _head_ones(D_RWKV)

    g_o[...] = _dot(_sigmoid(xg).astype(BF16), gup_ref[...])

    kx = k * vec[_V_KK:_V_KK + 1]
    kk_o[...] = kx * lax.rsqrt(_dot_exact_rhs(kx * kx, ones512) + 1e-12)

    lane = lax.broadcasted_iota(jnp.int32, xwa.shape, 1)
    lora_in = jnp.where(lane < LORA_W, jnp.tanh(xwa), xwa)
    lora = _dot3(lora_in, wlora_ref[...])

    z = wa0_ref[0:1] + lora[:, 0:2 * D_RWKV]
    logw = -_softplus(-z) - 0.5
    w_o[...] = jnp.exp(-jnp.exp(logw))
    a = _sigmoid(wa0_ref[1:2] + lora[:, 2 * D_RWKV:4 * D_RWKV])
    a_o[...] = a

    k_a = vec[_V_KA:_V_KA + 1]
    kd_sum = k * (1.0 + (a[:, 0:D_RWKV] - 1.0) * k_a) + k * (1.0 + (a[:, D_RWKV:2 * D_RWKV] - 1.0) * k_a)
    bonus_o[...] = _dot_exact_rhs(r * kd_sum * vec[_V_RK:_V_RK + 1], ones512) * v

    r_o[...] = r
    v_o[...] = v
    k_o[...] = k

    qn = q * lax.rsqrt(_dot_exact_rhs(q * q, ones512) * (1.0 / HEAD_DIM) + RMS_EPS) * vec[_V_QN:_V_QN + 1]
    kan = ka * lax.rsqrt(_dot_exact_rhs(ka * ka, _head_ones(D_KV)) * (1.0 / HEAD_DIM) + RMS_EPS) * kn_ref[...]
    if rope:
        qn = _rope(qn, cq_ref[...], sq_ref[...])
        kan = _rope(kan, ck_ref[...], sk_ref[...])
    q_o[...] = qn
    ka_o[...] = kan
    va_o[...] = va


def _pre_call(x2d, mod3d, seq_len, wts, rope_tabs):
    n_tok = x2d.shape[0]
    tiles_per_seq = seq_len // TOK_TILE
    n_mod = mod3d.shape[0]
    rope = rope_tabs is not None

    def const(shape):
        return pl.BlockSpec(shape, lambda i: tuple(0 for _ in shape))

    mod_map = (lambda i: (i // tiles_per_seq, 0, 0)) if n_mod > 1 else (lambda i: (0, 0, 0))
    in_specs = [
        pl.BlockSpec((TOK_TILE, D_MODEL), lambda i: (i, 0)),
        pl.BlockSpec((1, 1, 6 * D_MODEL), mod_map),
        const((1, D_MODEL)),
        const((D_MODEL, D_IN)),
        const((LORA_W + LORA_A, 4 * D_RWKV)),
        const((LORA_G, D_RWKV)),
        const((4, D_RWKV)),
        const((2, 2 * D_RWKV)),
        const((1, D_KV)),
    ]
    args = [x2d, mod3d, wts["g_pre"], wts["w_in"], wts["w_lora"], wts["g_up"], wts["vec"], wts["wa0"],
            wts["k_norm"]]
    if rope:
        tab_map = lambda i: (i % tiles_per_seq, 0)
        in_specs += [pl.BlockSpec((TOK_TILE, D_ATTN), tab_map), pl.BlockSpec((TOK_TILE, D_ATTN), tab_map),
                     pl.BlockSpec((TOK_TILE, D_KV), tab_map), pl.BlockSpec((TOK_TILE, D_KV), tab_map)]
        args += list(rope_tabs)

    widths = [512] * 4 + [1024, 1024] + [512] * 3 + [D_KV, D_KV]
    out_specs = [pl.BlockSpec((TOK_TILE, wd), lambda i: (i, 0)) for wd in widths]
    out_shape = [jax.ShapeDtypeStruct((n_tok, wd), F32) for wd in widths]
    return pl.pallas_call(
        functools.partial(_pre_kernel, rope=rope),
        grid=(n_tok // TOK_TILE,),
        in_specs=in_specs,
        out_specs=out_specs,
        out_shape=out_shape,
        compiler_params=pltpu.CompilerParams(dimension_semantics=("arbitrary",),
                                             vmem_limit_bytes=VMEM_LIMIT),
        name="pre_rope" if rope else "pre",
    )(*args)


def _to_chains_kernel(x_ref, o_ref, slab_ref, *, n_batch, dup):
    n_col = x_ref.shape[2] // HEAD_DIM
    n_grp = o_ref.shape[0]

    def stage1(b, carry):
        for jc in range(n_col // 2):
            a_t = x_ref[b, :, LANES * jc:LANES * (jc + 1)].T
            for h2 in range(2):
                row0 = pl.multiple_of(((2 * jc + h2) * n_batch + b) * CHAIN_PITCH, SUBLANES)
                slab_ref[pl.ds(row0, HEAD_DIM), :] = a_t[HEAD_DIM * h2:HEAD_DIM * (h2 + 1), :]
        return carry

    lax.fori_loop(0, n_batch, stage1, 0)

    def stage2(kb, carry):
        for ki in range(SUBLANES):
            k = kb * SUBLANES + ki
            for d in range(n_grp):
                if dup:
                    half = slab_ref[pl.ds(k, LANES // 2, stride=CHAIN_PITCH), :]
                    tile = jnp.concatenate([half, half], axis=0)
                else:
                    tile = slab_ref[pl.ds(d * LANES * CHAIN_PITCH + k, LANES, stride=CHAIN_PITCH), :]
                o_ref.at[d][pl.ds(k, RELAYOUT_TOK, stride=CHAIN_PITCH), :] = tile.T
        return carry

    lax.fori_loop(0, HEAD_DIM // SUBLANES, stage2, 0)

    zero = jnp.zeros((RELAYOUT_TOK, LANES), F32)
    for d in range(n_grp):
        for pad in range(HEAD_DIM, CHAIN_PITCH):
            o_ref.at[d][pl.ds(pad, RELAYOUT_TOK, stride=CHAIN_PITCH), :] = zero


def _to_chains_call(x2d, n_batch, seq_len, dup):
    n_chan = x2d.shape[1]
    n_slab = (n_chan // HEAD_DIM) * n_batch
    n_grp = n_slab * (2 if dup else 1) // LANES
    out = pl.pallas_call(
        functools.partial(_to_chains_kernel, n_batch=n_batch, dup=dup),
        grid=(seq_len // RELAYOUT_TOK,),
        in_specs=[pl.BlockSpec((n_batch, RELAYOUT_TOK, n_chan), lambda i: (0, i, 0))],
        out_specs=pl.BlockSpec((n_grp, RELAYOUT_TOK * CHAIN_PITCH, LANES), lambda i: (0, i, 0)),
        out_shape=jax.ShapeDtypeStruct((n_grp, seq_len * CHAIN_PITCH, LANES), F32),
        scratch_shapes=[pltpu.VMEM((n_slab * CHAIN_PITCH, LANES), F32)],
        compiler_params=pltpu.CompilerParams(dimension_semantics=("arbitrary",),
                                             vmem_limit_bytes=VMEM_LIMIT),
        name="to_chains_dup" if dup else "to_chains",
    )(x2d.reshape(n_batch, seq_len, n_chan))
    return out.reshape(n_grp, seq_len, CHAIN_PITCH, LANES)


def _from_chains_kernel(yf_ref, yb_ref, o_ref, slab_ref, *, n_batch, packed):
    def stage1(vb, carry):
        for vi in range(SUBLANES):
            v = vb * SUBLANES + vi
            tf = yf_ref[pl.ds(v, RELAYOUT_TOK, stride=CHAIN_PITCH), :]
            tb = yb_ref[pl.ds(v, RELAYOUT_TOK, stride=CHAIN_PITCH), :]
            if packed:
                tb = pltpu.roll(tb, LANES // 2, 1)
            slab_ref[pl.ds(v, LANES, stride=CHAIN_PITCH), :] = (tf + tb).T
        return carry

    lax.fori_loop(0, HEAD_DIM // SUBLANES, stage1, 0)

    def stage2(b, carry):
        for jc in range(H_RWKV // 2):
            rows = []
            for h2 in range(2):
                row0 = pl.multiple_of(((2 * jc + h2) * n_batch + b) * CHAIN_PITCH, SUBLANES)
                rows.append(slab_ref[pl.ds(row0, HEAD_DIM), :])
            o_ref[b, :, LANES * jc:LANES * (jc + 1)] = jnp.concatenate(rows, axis=0).T
        return carry

    lax.fori_loop(0, n_batch, stage2, 0)


def _from_chains_call(y_f, y_b, n_batch, seq_len, packed):
    n_rows = seq_len * CHAIN_PITCH
    grp_b = 0 if packed else 1
    blk = (None, RELAYOUT_TOK * CHAIN_PITCH, LANES)
    out = pl.pallas_call(
        functools.partial(_from_chains_kernel, n_batch=n_batch, packed=packed),
        grid=(seq_len // RELAYOUT_TOK,),
        in_specs=[pl.BlockSpec(blk, lambda i: (0, i, 0)), pl.BlockSpec(blk, lambda i: (grp_b, i, 0))],
        out_specs=pl.BlockSpec((n_batch, RELAYOUT_TOK, D_RWKV), lambda i: (0, i, 0)),
        out_shape=jax.ShapeDtypeStruct((n_batch, seq_len, D_RWKV), F32),
        scratch_shapes=[pltpu.VMEM((LANES * CHAIN_PITCH, LANES), F32)],
        compiler_params=pltpu.CompilerParams(dimension_semantics=("arbitrary",),
                                             vmem_limit_bytes=VMEM_LIMIT),
        name="from_chains_packed" if packed else "from_chains",
    )(y_f.reshape(-1, n_rows, LANES), y_b.reshape(-1, n_rows, LANES))
    return out.reshape(n_batch * seq_len, D_RWKV)


def _scan_kernel(*refs, packed, has_init):
    (rF, rB, kkF, kkB, kF, kB, vF, vB, wF, wB, aF, aB, ka_ref), rest = refs[:13], refs[13:]
    if has_init:
        s0_ref, rest = rest[0], rest[1:]
    yF, yB, sfin_ref, s_scr, sa_scr, op_scr = rest
    g = pl.program_id(0)
    j = pl.program_id(1)
    tile = (HEAD_DIM, LANES)

    if packed:
        use_b = lax.broadcasted_iota(jnp.int32, tile, 1) >= LANES // 2
    else:
        use_b = jnp.full(tile, g, jnp.int32) == 1

    def pick(f_ref, b_ref, s):
        return jnp.where(use_b, b_ref[SCAN_STEPS - 1 - s, 0:HEAD_DIM, :], f_ref[s, 0:HEAD_DIM, :])

    @pl.when(j == 0)
    def _():
        if has_init:
            s_scr[...] = s0_ref[...]
        else:
            s_scr[...] = jnp.zeros(s_scr.shape, F32)

    op_scr[4] = pick(kkF, kkB, 0)
    acc = jnp.zeros(tile, F32)
    for k in range(HEAD_DIM):
        acc = acc + s_scr[k] * op_scr[4, k:k + 1, :]
    sa_scr[...] = acc

    ka = ka_ref[...]
    pad = jnp.zeros((SCAN_STEPS, CHAIN_PITCH - HEAD_DIM, LANES), F32)
    yF[:, HEAD_DIM:CHAIN_PITCH, :] = pad
    yB[:, HEAD_DIM:CHAIN_PITCH, :] = pad

    def step(s, carry):
        a = pick(aF, aB, s)
        op_scr[0] = pick(wF, wB, s)
        op_scr[1] = pick(kkF, kkB, s) * a
        op_scr[2] = pick(kF, kB, s) * (1.0 + (a - 1.0) * ka)
        op_scr[3] = pick(rF, rB, s)
        op_scr[4] = pick(kkF, kkB, jnp.minimum(s + 1, SCAN_STEPS - 1))
        vt = pick(vF, vB, s)
        sa = sa_scr[...]

        def kblock(kb, acc):
            y, san = acc
            for ki in range(SCAN_K_UNROLL):
                k = kb * SCAN_K_UNROLL + ki
                wk = op_scr[0, pl.ds(k, 1), :]
                bk = op_scr[1, pl.ds(k, 1), :]
                kdk = op_scr[2, pl.ds(k, 1), :]
                rk = op_scr[3, pl.ds(k, 1), :]
                kn = op_scr[4, pl.ds(k, 1), :]
                sn = s_scr[k] * wk - sa * bk + vt * kdk
                s_scr[k] = sn
                y = y + sn * rk
                san = san + sn * kn
            return y, san

        zero = jnp.zeros(tile, F32)
        y, san = lax.fori_loop(0, HEAD_DIM // SCAN_K_UNROLL, kblock, (zero, zero))
        yF[s, 0:HEAD_DIM, :] = y
        yB[SCAN_STEPS - 1 - s, 0:HEAD_DIM, :] = y
        sa_scr[...] = san
        return carry

    lax.fori_loop(0, SCAN_STEPS, step, 0)

    @pl.when(j == pl.num_programs(1) - 1)
    def _():
        sfin_ref[...] = s_scr[...]


def _scan_call(shared, w, a, ka_tab, packed, s0=None):
    n_step = w.shape[1]
    n_grp = 1 if packed else 2
    n_blk = n_step // SCAN_STEPS
    has_init = s0 is not None
    blk = (None, SCAN_STEPS, CHAIN_PITCH, LANES)
    if packed:
        f_map = lambda g, j: (0, j, 0, 0)
        b_map = lambda g, j: (0, n_blk - 1 - j, 0, 0)
        fd_map, bd_map = f_map, b_map
    else:
        f_map = lambda g, j: (0, jnp.where(g == 0, j, 0), 0, 0)
        b_map = lambda g, j: (0, jnp.where(g == 1, n_blk - 1 - j, 0), 0, 0)
        fd_map = lambda g, j: (0, jnp.where(g == 0, j, 0), 0, 0)
        bd_map = lambda g, j: (1, jnp.where(g == 1, n_blk - 1 - j, 0), 0, 0)
    in_specs, args = [], []
    for x in shared:
        in_specs += [pl.BlockSpec(blk, f_map), pl.BlockSpec(blk, b_map)]
        args += [x, x]
    for x in (w, a):
        in_specs += [pl.BlockSpec(blk, fd_map), pl.BlockSpec(blk, bd_map)]
        args += [x, x]
    in_specs.append(pl.BlockSpec((HEAD_DIM, LANES), lambda g, j: (0, 0)))
    args.append(ka_tab)
    state_spec = pl.BlockSpec((None, HEAD_DIM, HEAD_DIM, LANES), lambda g, j: (g, 0, 0, 0))
    if has_init:
        in_specs.append(state_spec)
        args.append(s0)
    y_shape = jax.ShapeDtypeStruct((n_grp, n_step, CHAIN_PITCH, LANES), F32)
    return pl.pallas_call(
        functools.partial(_scan_kernel, packed=packed, has_init=has_init),
        grid=(n_grp, n_blk),
        in_specs=in_specs,
        out_specs=[pl.BlockSpec(blk, lambda g, j: (g, j, 0, 0)),
                   pl.BlockSpec(blk, lambda g, j: (g, n_blk - 1 - j, 0, 0)),
                   state_spec],
        out_shape=[y_shape, y_shape, jax.ShapeDtypeStruct((n_grp, HEAD_DIM, HEAD_DIM, LANES), F32)],
        scratch_shapes=[pltpu.VMEM((HEAD_DIM, HEAD_DIM, LANES), F32), pltpu.VMEM((HEAD_DIM, LANES), F32),
                        pltpu.VMEM((5, HEAD_DIM, LANES), F32)],
        compiler_params=pltpu.CompilerParams(dimension_semantics=("arbitrary", "arbitrary"),
                                             vmem_limit_bytes=VMEM_LIMIT),
        name="wkv_scan_packed" if packed else "wkv_scan",
    )(*args)


def _attn_kernel(q_ref, k_ref, v_ref, o_ref):
    q = q_ref[...]
    k = k_ref[...]
    v = v_ref[...]
    group = H_Q // H_KV
    for kvh in range(H_KV):
        kh = k[:, HEAD_DIM * kvh:HEAD_DIM * (kvh + 1)].astype(BF16)
        vh = v[:, HEAD_DIM * kvh:HEAD_DIM * (kvh + 1)].astype(BF16)
        for gq in range(group):
            hq = kvh * group + gq
            qh = (q[:, HEAD_DIM * hq:HEAD_DIM * (hq + 1)] * (HEAD_DIM ** -0.5)).astype(BF16)
            s = lax.dot_general(qh, kh, (((1,), (1,)), ((), ())), preferred_element_type=F32)
            e = jnp.exp(s - jnp.max(s, axis=-1, keepdims=True))
            o = _dot(e.astype(BF16), vh) / jnp.sum(e, axis=-1, keepdims=True)
            o_ref[:, HEAD_DIM * hq:HEAD_DIM * (hq + 1)] = o


def _attn_call(q2d, k2d, v2d, n_batch, lq, lk):
    q_tiles = lq // ATTN_Q_TILE
    return pl.pallas_call(
        _attn_kernel,
        grid=(n_batch, q_tiles),
        in_specs=[
            pl.BlockSpec((ATTN_Q_TILE, D_ATTN), lambda b, i: (b * q_tiles + i, 0)),
            pl.BlockSpec((lk, D_KV), lambda b, i: (b, 0)),
            pl.BlockSpec((lk, D_KV), lambda b, i: (b, 0)),
        ],
        out_specs=pl.BlockSpec((ATTN_Q_TILE, D_ATTN), lambda b, i: (b * q_tiles + i, 0)),
        out_shape=jax.ShapeDtypeStruct((n_batch * lq, D_ATTN), F32),
        compiler_params=pltpu.CompilerParams(dimension_semantics=("arbitrary", "arbitrary"),
                                             vmem_limit_bytes=VMEM_LIMIT),
        name="gqa_attn",
    )(q2d, k2d, v2d)


def _post_kernel(x_ref, mod_ref, ys_ref, bonus_ref, g_ref, attn_ref,
                 gn_ref, wout_ref, gpost_ref, gffn_ref, x1_o, h2_o):
    ones512 = _head_ones(D_RWKV)
    ys = ys_ref[...]
    ys_hi, ys_lo = _split_bf16(ys)
    mu = (_dot(ys_hi, ones512) + _dot(ys_lo, ones512)) * (1.0 / HEAD_DIM)
    dlt = ys - mu
    var = _dot_exact_rhs(dlt * dlt, ones512) * (1.0 / HEAD_DIM)
    gn = gn_ref[...]
    yn = dlt * lax.rsqrt(var + GN_EPS) * gn[0:1] + gn[1:2]
    yr = (yn + bonus_ref[...]) * g_ref[...]
    mix = (_dot(yr.astype(BF16), wout_ref[0:D_RWKV, :])
           + _dot(attn_ref[...].astype(BF16), wout_ref[D_RWKV:D_MODEL, :]))
    m = mod_ref[0]
    gate1 = m[:, 2 * D_MODEL:3 * D_MODEL]
    shift2 = m[:, 3 * D_MODEL:4 * D_MODEL]
    scale2 = m[:, 4 * D_MODEL:5 * D_MODEL]
    x1 = x_ref[...] + gate1 * _rms(mix, gpost_ref[...])
    x1_o[...] = x1
    h2_o[...] = (_rms(x1, gffn_ref[...]) * (1.0 + scale2) + shift2).astype(BF16)


def _post_call(x2d, mod3d, seq_len, ys, bonus, g, attn, wts):
    n_tok = x2d.shape[0]
    tiles_per_seq = seq_len // TOK_TILE
    n_mod = mod3d.shape[0]

    def const(shape):
        return pl.BlockSpec(shape, lambda i: tuple(0 for _ in shape))

    mod_map = (lambda i: (i // tiles_per_seq, 0, 0)) if n_mod > 1 else (lambda i: (0, 0, 0))
    tok = lambda wd: pl.BlockSpec((TOK_TILE, wd), lambda i: (i, 0))
    return pl.pallas_call(
        _post_kernel,
        grid=(n_tok // TOK_TILE,),
        in_specs=[tok(D_MODEL), pl.BlockSpec((1, 1, 6 * D_MODEL), mod_map),
                  tok(512), tok(512), tok(512), tok(512),
                  const((2, D_RWKV)), const((D_MODEL, D_MODEL)), const((1, D_MODEL)), const((1, D_MODEL))],
        out_specs=[tok(D_MODEL), tok(D_MODEL)],
        out_shape=[jax.ShapeDtypeStruct((n_tok, D_MODEL), F32), jax.ShapeDtypeStruct((n_tok, D_MODEL), BF16)],
        compiler_params=pltpu.CompilerParams(dimension_semantics=("arbitrary",),
                                             vmem_limit_bytes=VMEM_LIMIT),
        name="mix_post",
    )(x2d, mod3d, ys, bonus, g, attn, wts["gn"], wts["w_out"], wts["g_post"], wts["g_ffn_pre"])


def _ffn_kernel(h2_ref, x1_ref, mod_ref, wg_ref, wv_ref, cwg_ref, cwv_ref, cbg_ref, cbv_ref,
                wd_ref, gpost_ref, o_ref, acc_ref, *, seq_len):
    j = pl.program_id(1)

    @pl.when(j == 0)
    def _():
        acc_ref[...] = jnp.zeros(acc_ref.shape, F32)

    h2 = h2_ref[...]
    n_rows = h2.shape[0]
    assert seq_len & (seq_len - 1) == 0
    row = lax.broadcasted_iota(jnp.int32, (n_rows, FFN_SUB_TILE), 0) & (seq_len - 1)
    has_prev = row != 0
    has_next = row != seq_len - 1

    def conv(u, cw, cb):
        prev = jnp.where(has_prev, pltpu.roll(u, 1, 0), 0.0)
        nxt = jnp.where(has_next, pltpu.roll(u, n_rows - 1, 0), 0.0)
        return ((cb + prev * cw[0:1]) + u * cw[1:2]) + nxt * cw[2:3]

    down = None
    for c0 in range(0, FFN_COL_TILE, FFN_SUB_TILE):
        cols = slice(c0, c0 + FFN_SUB_TILE)
        gate = conv(_dot(h2, wg_ref[:, cols]), cwg_ref[:, cols], cbg_ref[:, cols])
        val = conv(_dot(h2, wv_ref[:, cols]), cwv_ref[:, cols], cbv_ref[:, cols])
        act = (gate * _sigmoid(gate) * val).astype(BF16)
        part = _dot(act, wd_ref[cols, :])
        down = part if down is None else down + part
    acc_ref[...] += down

    @pl.when(j == pl.num_programs(1) - 1)
    def _():
        m = mod_ref[0]
        gate2 = m[:, 5 * D_MODEL:6 * D_MODEL]
        o_ref[...] = x1_ref[...] + gate2 * _rms(acc_ref[...], gpost_ref[...])


def _ffn_call(h2, x1, mod3d, seq_len, wts):
    n_tok = h2.shape[0]
    tiles_per_seq = max(seq_len // FFN_TOK_TILE, 1)
    n_mod = mod3d.shape[0]
    n_col = D_FF_PAD // FFN_COL_TILE
    mod_map = (lambda i, j: (i // tiles_per_seq, 0, 0)) if n_mod > 1 else (lambda i, j: (0, 0, 0))
    tok = lambda: pl.BlockSpec((FFN_TOK_TILE, D_MODEL), lambda i, j: (i, 0))
    return pl.pallas_call(
        functools.partial(_ffn_kernel, seq_len=seq_len),
        grid=(n_tok // FFN_TOK_TILE, n_col),
        in_specs=[
            tok(), tok(), pl.BlockSpec((1, 1, 6 * D_MODEL), mod_map),
            pl.BlockSpec((D_MODEL, FFN_COL_TILE), lambda i, j: (0, j)),
            pl.BlockSpec((D_MODEL, FFN_COL_TILE), lambda i, j: (0, n_col + j)),
            pl.BlockSpec((3, FFN_COL_TILE), lambda i, j: (0, j)),
            pl.BlockSpec((3, FFN_COL_TILE), lambda i, j: (0, n_col + j)),
            pl.BlockSpec((1, FFN_COL_TILE), lambda i, j: (0, j)),
            pl.BlockSpec((1, FFN_COL_TILE), lambda i, j: (0, n_col + j)),
            pl.BlockSpec((FFN_COL_TILE, D_MODEL), lambda i, j: (j, 0)),
            pl.BlockSpec((1, D_MODEL), lambda i, j: (0, 0)),
        ],
        out_specs=tok(),
        out_shape=jax.ShapeDtypeStruct((n_tok, D_MODEL), F32),
        scratch_shapes=[pltpu.VMEM((FFN_TOK_TILE, D_MODEL), F32)],
        compiler_params=pltpu.CompilerParams(dimension_semantics=("arbitrary", "arbitrary"),
                                             vmem_limit_bytes=VMEM_LIMIT),
        name="conv_ffn",
    )(h2, x1, mod3d, wts["ffn_up"], wts["ffn_up"], wts["conv_w"], wts["conv_w"],
      wts["conv_b"], wts["conv_b"], wts["ffn_down"], wts["g_ffn_post"])


def _rope_tables(n_tok, n_heads):
    quarter = HEAD_DIM // 4
    pos = jnp.arange(n_tok)
    row = (pos // GRID_W).astype(F32)
    col = (pos % GRID_W).astype(F32)
    inv_freq = ROPE_THETA ** (-jnp.arange(quarter, dtype=F32) / quarter)

    def half(p):
        ang = p[:, None] * inv_freq[None, :]
        c, s = jnp.cos(ang), jnp.sin(ang)
        return jnp.concatenate([c, c], axis=-1), jnp.concatenate([-s, s], axis=-1)

    cr, sr = half(row)
    cc, sc = half(col)
    cos = jnp.concatenate([cr, cc], axis=-1)
    sin = jnp.concatenate([sr, sc], axis=-1)
    return jnp.tile(cos, (1, n_heads)), jnp.tile(sin, (1, n_heads))


def _pad_ff_cols(x):
    pad = ((0, 0), (0, D_FF_PAD - D_FF))
    return jnp.concatenate([jnp.pad(x[:, :D_FF], pad), jnp.pad(x[:, D_FF:], pad)], axis=1)


def _mixer(pre, n_batch, seq_len, packed, k_a, s0=None):
    r, v, kk, k, w, a = pre[:6]
    shared = tuple(_to_chains_call(x, n_batch, seq_len, packed) for x in (r, kk, k, v))
    ka_tab = jnp.repeat(k_a.reshape(H_RWKV, HEAD_DIM).T, n_batch, axis=1)
    ka_tab = jnp.tile(ka_tab, (1, LANES // ka_tab.shape[1]))
    y_f, y_b, s_fin = _scan_call(shared, _to_chains_call(w, n_batch, seq_len, False),
                                 _to_chains_call(a, n_batch, seq_len, False), ka_tab, packed, s0)
    return _from_chains_call(y_f, y_b, n_batch, seq_len, packed), s_fin


def kernel(x_prompt, x_sample, cache_k, cache_v, state_rwkv, c, c_ctx, w_mod, b_mod, norm_mix_pre, norm_mix_post, norm_ffn_pre, norm_ffn_post, w_in, w0, w_up, a0, a_up, g_up, k_k, k_a, r_k, gn_w, gn_b, q_norm, k_norm, w_out, ffn_up, conv_w, conv_b, ffn_down):
    n_ctx, l_ctx = x_prompt.shape[0], x_prompt.shape[1]
    n_lat, l_lat = x_sample.shape[0], x_sample.shape[1]
    l_past = cache_k.shape[2]
    layer = 0
    assert n_ctx * H_RWKV == LANES and 2 * n_lat * H_RWKV == LANES

    zeros_w = jnp.zeros((LORA_W, D_RWKV), F32)
    w_lora = jnp.concatenate([
        jnp.concatenate([w_up[layer, 0], w_up[layer, 1], zeros_w, zeros_w], axis=1),
        jnp.concatenate([zeros_w, zeros_w, a_up[layer, 0], a_up[layer, 1]], axis=1)], axis=0)
    vec = jnp.stack([k_k[layer], k_a[layer], r_k[layer].reshape(D_RWKV), jnp.tile(q_norm[layer], H_Q)])
    wts = dict(
        g_pre=norm_mix_pre[layer].reshape(1, D_MODEL),
        g_post=norm_mix_post[layer].reshape(1, D_MODEL),
        g_ffn_pre=norm_ffn_pre[layer].reshape(1, D_MODEL),
        g_ffn_post=norm_ffn_post[layer].reshape(1, D_MODEL),
        w_in=w_in[layer].astype(BF16),
        w_lora=w_lora,
        g_up=g_up[layer].astype(BF16),
        vec=vec,
        wa0=jnp.stack([w0[layer].reshape(2 * D_RWKV), a0[layer].reshape(2 * D_RWKV)]),
        k_norm=jnp.tile(k_norm[layer], H_KV).reshape(1, D_KV),
        gn=jnp.stack([gn_w[layer], gn_b[layer]]),
        w_out=w_out[layer].astype(BF16),
        ffn_up=_pad_ff_cols(ffn_up[layer].astype(BF16)),
        conv_w=_pad_ff_cols(conv_w[layer]),
        conv_b=_pad_ff_cols(conv_b[layer].reshape(1, 2 * D_FF)),
        ffn_down=jnp.pad(ffn_down[layer].astype(BF16), ((0, D_FF_PAD - D_FF), (0, 0))),
    )

    cvec = jnp.concatenate([c, c_ctx[None, :], jnp.zeros((16 - n_lat - 1, D_MODEL), F32)], axis=0)
    mod = _mod_call(cvec, w_mod[layer], b_mod[layer])
    mod_lat = mod[:n_lat].reshape(n_lat, 1, 6 * D_MODEL)
    mod_ctx = mod[n_lat:n_lat + 1].reshape(1, 1, 6 * D_MODEL)

    xc = x_prompt.reshape(n_ctx * l_ctx, D_MODEL)
    pre_c = _pre_call(xc, mod_ctx, l_ctx, wts, None)
    g_c, bonus_c, q_c, ka_c, va_c = pre_c[6:]
    ys_c, s_c = _mixer(pre_c, n_ctx, l_ctx, False, k_a[layer])
    attn_c = _attn_call(q_c, ka_c, va_c, n_ctx, l_ctx, l_ctx)
    x1_c, h2_c = _post_call(xc, mod_ctx, l_ctx, ys_c, bonus_c, g_c, attn_c, wts)
    out_c = _ffn_call(h2_c, x1_c, mod_ctx, l_ctx, wts)

    xl = x_sample.reshape(n_lat * l_lat, D_MODEL)
    tabs = _rope_tables(l_lat, H_Q) + _rope_tables(l_lat, H_KV)
    pre_l = _pre_call(xl, mod_lat, l_lat, wts, tabs)
    g_l, bonus_l, q_l, kr_l, va_l = pre_l[6:]
    s0 = state_rwkv[:, layer].transpose(4, 3, 1, 2, 0).reshape(1, HEAD_DIM, HEAD_DIM, LANES)
    ys_l, _ = _mixer(pre_l, n_lat, l_lat, True, k_a[layer], s0)
    k_all = jnp.concatenate([kr_l.reshape(n_lat, l_lat, D_KV), cache_k[:, layer].reshape(n_lat, l_past, D_KV)],
                            axis=1).reshape(n_lat * (l_lat + l_past), D_KV)
    v_all = jnp.concatenate([va_l.reshape(n_lat, l_lat, D_KV), cache_v[:, layer].reshape(n_lat, l_past, D_KV)],
                            axis=1).reshape(n_lat * (l_lat + l_past), D_KV)
    attn_l = _attn_call(q_l, k_all, v_all, n_lat, l_lat, l_lat + l_past)
    x1_l, h2_l = _post_call(xl, mod_lat, l_lat, ys_l, bonus_l, g_l, attn_l, wts)
    out_l = _ffn_call(h2_l, x1_l, mod_lat, l_lat, wts)

    y_prompt = out_c.reshape(n_ctx, l_ctx, D_MODEL)
    y_sample = out_l.reshape(n_lat, l_lat, D_MODEL)
    new_cache_k = ka_c.reshape(n_ctx, 1, l_ctx, H_KV, HEAD_DIM)
    new_cache_v = va_c.reshape(n_ctx, 1, l_ctx, H_KV, HEAD_DIM)
    new_state = s_c.reshape(2, HEAD_DIM, HEAD_DIM, H_RWKV, n_ctx).transpose(4, 0, 3, 2, 1)[:, None]
    return (y_prompt, y_sample, new_cache_k, new_cache_v, new_state)
```

```python
import functools

import jax
import jax.numpy as jnp
from jax import lax
from jax.experimental import pallas as pl
from jax.experimental.pallas import tpu as pltpu

F32 = jnp.float32
BF16 = jnp.bfloat16

D_MODEL = 1024
HEAD_DIM = 64
D_RWKV = 512
H_RWKV = 8
D_ATTN = 512
H_Q = 8
H_KV = 2
D_KV = 128
LORA_W = 64
LORA_A = 64
LORA_G = 128
D_FF = 2816
D_IN = 2560
GRID_W = 64
ROPE_THETA = 10000.0
RMS_EPS = 1e-6
GN_EPS = 64e-5

LANES = 128
SUBLANES = 8
TOK_TILE = 512
FFN_TOK_TILE = 1024
FFN_COL_TILE = 512
FFN_SUB_TILE = 256
D_FF_PAD = -(-D_FF // FFN_COL_TILE) * FFN_COL_TILE
SCAN_STEPS = 32
SCAN_K_UNROLL = 16
RELAYOUT_TOK = 128
CHAIN_PITCH = 72
ATTN_Q_TILE = 512
MOD_COL_TILE = 768
VMEM_LIMIT = 56 * 1024 * 1024


def _dot(a, b):
    return jnp.dot(a, b, preferred_element_type=F32)


def _split_bf16(x):
    hi = x.astype(BF16)
    lo = (x - hi.astype(F32)).astype(BF16)
    return hi, lo


def _dot_exact_rhs(x, w_bf16):
    return _dot(x.astype(BF16), w_bf16)


def _dot3(a, b):
    ah, al = _split_bf16(a)
    bh, bl = _split_bf16(b)
    return _dot(ah, bh) + (_dot(ah, bl) + _dot(al, bh))


def _head_ones(n):
    shift = HEAD_DIM.bit_length() - 1
    r = lax.shift_right_logical(lax.broadcasted_iota(jnp.int32, (n, n), 0), shift)
    c = lax.shift_right_logical(lax.broadcasted_iota(jnp.int32, (n, n), 1), shift)
    return jnp.where(r == c, 1.0, 0.0).astype(BF16)


def _sigmoid(x):
    return 1.0 / (1.0 + jnp.exp(-x))


def _softplus(x):
    return jnp.maximum(x, 0.0) + jnp.log(1.0 + jnp.exp(-jnp.abs(x)))


def _rms(x, g):
    return x * lax.rsqrt(jnp.mean(x * x, axis=-1, keepdims=True) + RMS_EPS) * g


def _rope(x, cos, sgn_sin):
    n = x.shape[-1]
    lane = lax.broadcasted_iota(jnp.int32, x.shape, 1)
    partner = jnp.where((lane & 16) == 0, pltpu.roll(x, n - 16, 1), pltpu.roll(x, 16, 1))
    return x * cos + partner * sgn_sin


def _mod_kernel(c_ref, w_ref, b_ref, o_ref):
    c = c_ref[...]
    o_ref[...] = _dot3(c * _sigmoid(c), w_ref[...]) + b_ref[...]


def _mod_call(cvec, w_mod, b_mod):
    n = w_mod.shape[1]
    rows = cvec.shape[0]
    return pl.pallas_call(
        _mod_kernel,
        grid=(n // MOD_COL_TILE,),
        in_specs=[
            pl.BlockSpec((rows, D_MODEL), lambda j: (0, 0)),
            pl.BlockSpec((D_MODEL, MOD_COL_TILE), lambda j: (0, j)),
            pl.BlockSpec((1, MOD_COL_TILE), lambda j: (0, j)),
        ],
        out_specs=pl.BlockSpec((rows, MOD_COL_TILE), lambda j: (0, j)),
        out_shape=jax.ShapeDtypeStruct((rows, n), F32),
        compiler_params=pltpu.CompilerParams(dimension_semantics=("arbitrary",)),
        name="adaln_mod",
    )(cvec, w_mod, b_mod.reshape(1, n))


_V_KK, _V_KA, _V_RK, _V_QN = range(4)


def _pre_kernel(*refs, rope):
    if rope:
        (x_ref, mod_ref, gpre_ref, win_ref, wlora_ref, gup_ref, vec_ref, wa0_ref, kn_ref,
         cq_ref, sq_ref, ck_ref, sk_ref,
         r_o, v_o, kk_o, k_o, w_o, a_o, g_o, bonus_o, q_o, ka_o, va_o) = refs
    else:
        (x_ref, mod_ref, gpre_ref, win_ref, wlora_ref, gup_ref, vec_ref, wa0_ref, kn_ref,
         r_o, v_o, kk_o, k_o, w_o, a_o, g_o, bonus_o, q_o, ka_o, va_o) = refs

    x = x_ref[...]
    m = mod_ref[0]
    shift1 = m[:, 0:D_MODEL]
    scale1 = m[:, D_MODEL:2 * D_MODEL]
    h = _rms(x, gpre_ref[...]) * (1.0 + scale1) + shift1
    p = _dot(h.astype(BF16), win_ref[...])

    r = p[:, 0:512]
    k = p[:, 512:1024]
    v = p[:, 1024:1536]
    xwa = p[:, 1536:1664]
    xg = p[:, 1664:1792]
    q = p[:, 1792:2304]
    ka = p[:, 2304:2432]
    va = p[:, 2432:2560]

    vec = vec_ref[...]
    ones512 = _head_ones(D_RWKV)

    g_o[...] = _dot(_sigmoid(xg).astype(BF16), gup_ref[...])

    kx = k * vec[_V_KK:_V_KK + 1]
    kk_o[...] = kx * lax.rsqrt(_dot_exact_rhs(kx * kx, ones512) + 1e-12)

    lane = lax.broadcasted_iota(jnp.int32, xwa.shape, 1)
    lora_in = jnp.where(lane < LORA_W, jnp.tanh(xwa), xwa)
    lora = _dot3(lora_in, wlora_ref[...])

    z = wa0_ref[0:1] + lora[:, 0:2 * D_RWKV]
    logw = -_softplus(-z) - 0.5
    w_o[...] = jnp.exp(-jnp.exp(logw))
    a = _sigmoid(wa0_ref[1:2] + lora[:, 2 * D_RWKV:4 * D_RWKV])
    a_o[...] = a

    k_a = vec[_V_KA:_V_KA + 1]
    kd_sum = k * (1.0 + (a[:, 0:D_RWKV] - 1.0) * k_a) + k * (1.0 + (a[:, D_RWKV:2 * D_RWKV] - 1.0) * k_a)
    bonus_o[...] = _dot_exact_rhs(r * kd_sum * vec[_V_RK:_V_RK + 1], ones512) * v

    r_o[...] = r
    v_o[...] = v
    k_o[...] = k

    qn = q * lax.rsqrt(_dot_exact_rhs(q * q, ones512) * (1.0 / HEAD_DIM) + RMS_EPS) * vec[_V_QN:_V_QN + 1]
    kan = ka * lax.rsqrt(_dot_exact_rhs(ka * ka, _head_ones(D_KV)) * (1.0 / HEAD_DIM) + RMS_EPS) * kn_ref[...]
    if rope:
        qn = _rope(qn, cq_ref[...], sq_ref[...])
        kan = _rope(kan, ck_ref[...], sk_ref[...])
    q_o[...] = qn
    ka_o[...] = kan
    va_o[...] = va


def _pre_call(x2d, mod3d, seq_len, wts, rope_tabs):
    n_tok = x2d.shape[0]
    tiles_per_seq = seq_len // TOK_TILE
    n_mod = mod3d.shape[0]
    rope = rope_tabs is not None

    def const(shape):
        return pl.BlockSpec(shape, lambda i: tuple(0 for _ in shape))

    mod_map = (lambda i: (i // tiles_per_seq, 0, 0)) if n_mod > 1 else (lambda i: (0, 0, 0))
    in_specs = [
        pl.BlockSpec((TOK_TILE, D_MODEL), lambda i: (i, 0)),
        pl.BlockSpec((1, 1, 6 * D_MODEL), mod_map),
        const((1, D_MODEL)),
        const((D_MODEL, D_IN)),
        const((LORA_W + LORA_A, 4 * D_RWKV)),
        const((LORA_G, D_RWKV)),
        const((4, D_RWKV)),
        const((2, 2 * D_RWKV)),
        const((1, D_KV)),
    ]
    args = [x2d, mod3d, wts["g_pre"], wts["w_in"], wts["w_lora"], wts["g_up"], wts["vec"], wts["wa0"],
            wts["k_norm"]]
    if rope:
        tab_map = lambda i: (i % tiles_per_seq, 0)
        in_specs += [pl.BlockSpec((TOK_TILE, D_ATTN), tab_map), pl.BlockSpec((TOK_TILE, D_ATTN), tab_map),
                     pl.BlockSpec((TOK_TILE, D_KV), tab_map), pl.BlockSpec((TOK_TILE, D_KV), tab_map)]
        args += list(rope_tabs)

    widths = [512] * 4 + [1024, 1024] + [512] * 3 + [D_KV, D_KV]
    out_specs = [pl.BlockSpec((TOK_TILE, wd), lambda i: (i, 0)) for wd in widths]
    out_shape = [jax.ShapeDtypeStruct((n_tok, wd), F32) for wd in widths]
    return pl.pallas_call(
        functools.partial(_pre_kernel, rope=rope),
        grid=(n_tok // TOK_TILE,),
        in_specs=in_specs,
        out_specs=out_specs,
        out_shape=out_shape,
        compiler_params=pltpu.CompilerParams(dimension_semantics=("arbitrary",),
                                             vmem_limit_bytes=VMEM_LIMIT),
        name="pre_rope" if rope else "pre",
    )(*args)


def _to_chains_kernel(x_ref, o_ref, slab_ref, *, n_batch, dup):
    n_col = x_ref.shape[2] // HEAD_DIM
    n_grp = o_ref.shape[0]

    def stage1(b, carry):
        for jc in range(n_col // 2):
            a_t = x_ref[b, :, LANES * jc:LANES * (jc + 1)].T
            for h2 in range(2):
                row0 = pl.multiple_of(((2 * jc + h2) * n_batch + b) * CHAIN_PITCH, SUBLANES)
                slab_ref[pl.ds(row0, HEAD_DIM), :] = a_t[HEAD_DIM * h2:HEAD_DIM * (h2 + 1), :]
        return carry

    lax.fori_loop(0, n_batch, stage1, 0)

    def stage2(kb, carry):
        for ki in range(SUBLANES):
            k = kb * SUBLANES + ki
            for d in range(n_grp):
                if dup:
                    half = slab_ref[pl.ds(k, LANES // 2, stride=CHAIN_PITCH), :]
                    tile = jnp.concatenate([half, half], axis=0)
                else:
                    tile = slab_ref[pl.ds(d * LANES * CHAIN_PITCH + k, LANES, stride=CHAIN_PITCH), :]
                o_ref.at[d][pl.ds(k, RELAYOUT_TOK, stride=CHAIN_PITCH), :] = tile.T
        return carry

    lax.fori_loop(0, HEAD_DIM // SUBLANES, stage2, 0)

    zero = jnp.zeros((RELAYOUT_TOK, LANES), F32)
    for d in range(n_grp):
        for pad in range(HEAD_DIM, CHAIN_PITCH):
            o_ref.at[d][pl.ds(pad, RELAYOUT_TOK, stride=CHAIN_PITCH), :] = zero


def _to_chains_call(x2d, n_batch, seq_len, dup):
    n_chan = x2d.shape[1]
    n_slab = (n_chan // HEAD_DIM) * n_batch
    n_grp = n_slab * (2 if dup else 1) // LANES
    out = pl.pallas_call(
        functools.partial(_to_chains_kernel, n_batch=n_batch, dup=dup),
        grid=(seq_len // RELAYOUT_TOK,),
        in_specs=[pl.BlockSpec((n_batch, RELAYOUT_TOK, n_chan), lambda i: (0, i, 0))],
        out_specs=pl.BlockSpec((n_grp, RELAYOUT_TOK * CHAIN_PITCH, LANES), lambda i: (0, i, 0)),
        out_shape=jax.ShapeDtypeStruct((n_grp, seq_len * CHAIN_PITCH, LANES), F32),
        scratch_shapes=[pltpu.VMEM((n_slab * CHAIN_PITCH, LANES), F32)],
        compiler_params=pltpu.CompilerParams(dimension_semantics=("arbitrary",),
                                             vmem_limit_bytes=VMEM_LIMIT),
        name="to_chains_dup" if dup else "to_chains",
    )(x2d.reshape(n_batch, seq_len, n_chan))
    return out.reshape(n_grp, seq_len, CHAIN_PITCH, LANES)


def _from_chains_kernel(yf_ref, yb_ref, o_ref, slab_ref, *, n_batch, packed):
    def stage1(vb, carry):
        for vi in range(SUBLANES):
            v = vb * SUBLANES + vi
            tf = yf_ref[pl.ds(v, RELAYOUT_TOK, stride=CHAIN_PITCH), :]
            tb = yb_ref[pl.ds(v, RELAYOUT_TOK, stride=CHAIN_PITCH), :]
            if packed:
                tb = pltpu.roll(tb, LANES // 2, 1)
            slab_ref[pl.ds(v, LANES, stride=CHAIN_PITCH), :] = (tf + tb).T
        return carry

    lax.fori_loop(0, HEAD_DIM // SUBLANES, stage1, 0)

    def stage2(b, carry):
        for jc in range(H_RWKV // 2):
            rows = []
            for h2 in range(2):
                row0 = pl.multiple_of(((2 * jc + h2) * n_batch + b) * CHAIN_PITCH, SUBLANES)
                rows.append(slab_ref[pl.ds(row0, HEAD_DIM), :])
            o_ref[b, :, LANES * jc:LANES * (jc + 1)] = jnp.concatenate(rows, axis=0).T
        return carry

    lax.fori_loop(0, n_batch, stage2, 0)


def _from_chains_call(y_f, y_b, n_batch, seq_len, packed):
    n_rows = seq_len * CHAIN_PITCH
    grp_b = 0 if packed else 1
    blk = (None, RELAYOUT_TOK * CHAIN_PITCH, LANES)
    out = pl.pallas_call(
        functools.partial(_from_chains_kernel, n_batch=n_batch, packed=packed),
        grid=(seq_len // RELAYOUT_TOK,),
        in_specs=[pl.BlockSpec(blk, lambda i: (0, i, 0)), pl.BlockSpec(blk, lambda i: (grp_b, i, 0))],
        out_specs=pl.BlockSpec((n_batch, RELAYOUT_TOK, D_RWKV), lambda i: (0, i, 0)),
        out_shape=jax.ShapeDtypeStruct((n_batch, seq_len, D_RWKV), F32),
        scratch_shapes=[pltpu.VMEM((LANES * CHAIN_PITCH, LANES), F32)],
        compiler_params=pltpu.CompilerParams(dimension_semantics=("arbitrary",),
                                             vmem_limit_bytes=VMEM_LIMIT),
        name="from_chains_packed" if packed else "from_chains",
    )(y_f.reshape(-1, n_rows, LANES), y_b.reshape(-1, n_rows, LANES))
    return out.reshape(n_batch * seq_len, D_RWKV)


def _scan_kernel(*refs, packed, has_init):
    (rF, rB, kkF, kkB, kF, kB, vF, vB, wF, wB, aF, aB, ka_ref), rest = refs[:13], refs[13:]
    if has_init:
        s0_ref, rest = rest[0], rest[1:]
    yF, yB, sfin_ref, s_scr, sa_scr, op_scr = rest
    g = pl.program_id(0)
    j = pl.program_id(1)
    tile = (HEAD_DIM, LANES)

    if packed:
        use_b = lax.broadcasted_iota(jnp.int32, tile, 1) >= LANES // 2
    else:
        use_b = jnp.full(tile, g, jnp.int32) == 1

    def pick(f_ref, b_ref, s):
        return jnp.where(use_b, b_ref[SCAN_STEPS - 1 - s, 0:HEAD_DIM, :], f_ref[s, 0:HEAD_DIM, :])

    @pl.when(j == 0)
    def _():
        if has_init:
            s_scr[...] = s0_ref[...]
        else:
            s_scr[...] = jnp.zeros(s_scr.shape, F32)

    op_scr[4] = pick(kkF, kkB, 0)
    acc = jnp.zeros(tile, F32)
    for k in range(HEAD_DIM):
        acc = acc + s_scr[k] * op_scr[4, k:k + 1, :]
    sa_scr[...] = acc

    ka = ka_ref[...]
    pad = jnp.zeros((SCAN_STEPS, CHAIN_PITCH - HEAD_DIM, LANES), F32)
    yF[:, HEAD_DIM:CHAIN_PITCH, :] = pad
    yB[:, HEAD_DIM:CHAIN_PITCH, :] = pad

    def step(s, carry):
        a = pick(aF, aB, s)
        op_scr[0] = pick(wF, wB, s)
        op_scr[1] = pick(kkF, kkB, s) * a
        op_scr[2] = pick(kF, kB, s) * (1.0 + (a - 1.0) * ka)
        op_scr[3] = pick(rF, rB, s)
        op_scr[4] = pick(kkF, kkB, jnp.minimum(s + 1, SCAN_STEPS - 1))
        vt = pick(vF, vB, s)
        sa = sa_scr[...]

        def kblock(kb, acc):
            y, san = acc
            for ki in range(SCAN_K_UNROLL):
                k = kb * SCAN_K_UNROLL + ki
                wk = op_scr[0, pl.ds(k, 1), :]
                bk = op_scr[1, pl.ds(k, 1), :]
                kdk = op_scr[2, pl.ds(k, 1), :]
                rk = op_scr[3, pl.ds(k, 1), :]
                kn = op_scr[4, pl.ds(k, 1), :]
                sn = s_scr[k] * wk - sa * bk + vt * kdk
                s_scr[k] = sn
                y = y + sn * rk
                san = san + sn * kn
            return y, san

        zero = jnp.zeros(tile, F32)
        y, san = lax.fori_loop(0, HEAD_DIM // SCAN_K_UNROLL, kblock, (zero, zero))
        yF[s, 0:HEAD_DIM, :] = y
        yB[SCAN_STEPS - 1 - s, 0:HEAD_DIM, :] = y
        sa_scr[...] = san
        return carry

    lax.fori_loop(0, SCAN_STEPS, step, 0)

    @pl.when(j == pl.num_programs(1) - 1)
    def _():
        sfin_ref[...] = s_scr[...]


def _scan_call(shared, w, a, ka_tab, packed, s0=None):
    n_step = w.shape[1]
    n_grp = 1 if packed else 2
    n_blk = n_step // SCAN_STEPS
    has_init = s0 is not None
    blk = (None, SCAN_STEPS, CHAIN_PITCH, LANES)
    if packed:
        f_map = lambda g, j: (0, j, 0, 0)
        b_map = lambda g, j: (0, n_blk - 1 - j, 0, 0)
        fd_map, bd_map = f_map, b_map
    else:
        f_map = lambda g, j: (0, jnp.where(g == 0, j, 0), 0, 0)
        b_map = lambda g, j: (0, jnp.where(g == 1, n_blk - 1 - j, 0), 0, 0)
        fd_map = lambda g, j: (0, jnp.where(g == 0, j, 0), 0, 0)
        bd_map = lambda g, j: (1, jnp.where(g == 1, n_blk - 1 - j, 0), 0, 0)
    in_specs, args = [], []
    for x in shared:
        in_specs += [pl.BlockSpec(blk, f_map), pl.BlockSpec(blk, b_map)]
        args += [x, x]
    for x in (w, a):
        in_specs += [pl.BlockSpec(blk, fd_map), pl.BlockSpec(blk, bd_map)]
        args += [x, x]
    in_specs.append(pl.BlockSpec((HEAD_DIM, LANES), lambda g, j: (0, 0)))
    args.append(ka_tab)
    state_spec = pl.BlockSpec((None, HEAD_DIM, HEAD_DIM, LANES), lambda g, j: (g, 0, 0, 0))
    if has_init:
        in_specs.append(state_spec)
        args.append(s0)
    y_shape = jax.ShapeDtypeStruct((n_grp, n_step, CHAIN_PITCH, LANES), F32)
    return pl.pallas_call(
        functools.partial(_scan_kernel, packed=packed, has_init=has_init),
        grid=(n_grp, n_blk),
        in_specs=in_specs,
        out_specs=[pl.BlockSpec(blk, lambda g, j: (g, j, 0, 0)),
                   pl.BlockSpec(blk, lambda g, j: (g, n_blk - 1 - j, 0, 0)),
                   state_spec],
        out_shape=[y_shape, y_shape, jax.ShapeDtypeStruct((n_grp, HEAD_DIM, HEAD_DIM, LANES), F32)],
        scratch_shapes=[pltpu.VMEM((HEAD_DIM, HEAD_DIM, LANES), F32), pltpu.VMEM((HEAD_DIM, LANES), F32),
                        pltpu.VMEM((5, HEAD_DIM, LANES), F32)],
        compiler_params=pltpu.CompilerParams(dimension_semantics=("arbitrary", "arbitrary"),
                                             vmem_limit_bytes=VMEM_LIMIT),
        name="wkv_scan_packed" if packed else "wkv_scan",
    )(*args)


def _attn_kernel(q_ref, k_ref, v_ref, o_ref):
    q = q_ref[...]
    k = k_ref[...]
    v = v_ref[...]
    group = H_Q // H_KV
    for kvh in range(H_KV):
        kh = k[:, HEAD_DIM * kvh:HEAD_DIM * (kvh + 1)].astype(BF16)
        vh = v[:, HEAD_DIM * kvh:HEAD_DIM * (kvh + 1)].astype(BF16)
        for gq in range(group):
            hq = kvh * group + gq
            qh = (q[:, HEAD_DIM * hq:HEAD_DIM * (hq + 1)] * (HEAD_DIM ** -0.5)).astype(BF16)
            s = lax.dot_general(qh, kh, (((1,), (1,)), ((), ())), preferred_element_type=F32)
            e = jnp.exp(s - jnp.max(s, axis=-1, keepdims=True))
            o = _dot(e.astype(BF16), vh) / jnp.sum(e, axis=-1, keepdims=True)
            o_ref[:, HEAD_DIM * hq:HEAD_DIM * (hq + 1)] = o


def _attn_call(q2d, k2d, v2d, n_batch, lq, lk):
    q_tile = min(ATTN_Q_TILE, lq)
    q_tiles = lq // q_tile
    return pl.pallas_call(
        _attn_kernel,
        grid=(n_batch, q_tiles),
        in_specs=[
            pl.BlockSpec((q_tile, D_ATTN), lambda b, i: (b * q_tiles + i, 0)),
            pl.BlockSpec((lk, D_KV), lambda b, i: (b, 0)),
            pl.BlockSpec((lk, D_KV), lambda b, i: (b, 0)),
        ],
        out_specs=pl.BlockSpec((q_tile, D_ATTN), lambda b, i: (b * q_tiles + i, 0)),
        out_shape=jax.ShapeDtypeStruct((n_batch * lq, D_ATTN), F32),
        compiler_params=pltpu.CompilerParams(dimension_semantics=("arbitrary", "arbitrary"),
                                             vmem_limit_bytes=VMEM_LIMIT),
        name="gqa_attn",
    )(q2d, k2d, v2d)


def _post_kernel(x_ref, mod_ref, ys_ref, bonus_ref, g_ref, attn_ref,
                 gn_ref, wout_ref, gpost_ref, gffn_ref, x1_o, h2_o):
    ones512 = _head_ones(D_RWKV)
    ys = ys_ref[...]
    ys_hi, ys_lo = _split_bf16(ys)
    mu = (_dot(ys_hi, ones512) + _dot(ys_lo, ones512)) * (1.0 / HEAD_DIM)
    dlt = ys - mu
    var = _dot_exact_rhs(dlt * dlt, ones512) * (1.0 / HEAD_DIM)
    gn = gn_ref[...]
    yn = dlt * lax.rsqrt(var + GN_EPS) * gn[0:1] + gn[1:2]
    yr = (yn + bonus_ref[...]) * g_ref[...]
    mix = (_dot(yr.astype(BF16), wout_ref[0:D_RWKV, :])
           + _dot(attn_ref[...].astype(BF16), wout_ref[D_RWKV:D_MODEL, :]))
    m = mod_ref[0]
    gate1 = m[:, 2 * D_MODEL:3 * D_MODEL]
    shift2 = m[:, 3 * D_MODEL:4 * D_MODEL]
    scale2 = m[:, 4 * D_MODEL:5 * D_MODEL]
    x1 = x_ref[...] + gate1 * _rms(mix, gpost_ref[...])
    x1_o[...] = x1
    h2_o[...] = (_rms(x1, gffn_ref[...]) * (1.0 + scale2) + shift2).astype(BF16)


def _post_call(x2d, mod3d, seq_len, ys, bonus, g, attn, wts):
    n_tok = x2d.shape[0]
    tiles_per_seq = seq_len // TOK_TILE
    n_mod = mod3d.shape[0]

    def const(shape):
        return pl.BlockSpec(shape, lambda i: tuple(0 for _ in shape))

    mod_map = (lambda i: (i // tiles_per_seq, 0, 0)) if n_mod > 1 else (lambda i: (0, 0, 0))
    tok = lambda wd: pl.BlockSpec((TOK_TILE, wd), lambda i: (i, 0))
    return pl.pallas_call(
        _post_kernel,
        grid=(n_tok // TOK_TILE,),
        in_specs=[tok(D_MODEL), pl.BlockSpec((1, 1, 6 * D_MODEL), mod_map),
                  tok(512), tok(512), tok(512), tok(512),
                  const((2, D_RWKV)), const((D_MODEL, D_MODEL)), const((1, D_MODEL)), const((1, D_MODEL))],
        out_specs=[tok(D_MODEL), tok(D_MODEL)],
        out_shape=[jax.ShapeDtypeStruct((n_tok, D_MODEL), F32), jax.ShapeDtypeStruct((n_tok, D_MODEL), BF16)],
        compiler_params=pltpu.CompilerParams(dimension_semantics=("arbitrary",),
                                             vmem_limit_bytes=VMEM_LIMIT),
        name="mix_post",
    )(x2d, mod3d, ys, bonus, g, attn, wts["gn"], wts["w_out"], wts["g_post"], wts["g_ffn_pre"])


def _ffn_kernel(h2_ref, x1_ref, mod_ref, wg_ref, wv_ref, cwg_ref, cwv_ref, cbg_ref, cbv_ref,
                wd_ref, gpost_ref, o_ref, acc_ref, *, seq_len):
    j = pl.program_id(1)

    @pl.when(j == 0)
    def _():
        acc_ref[...] = jnp.zeros(acc_ref.shape, F32)

    h2 = h2_ref[...]
    n_rows = h2.shape[0]
    assert seq_len & (seq_len - 1) == 0
    row = lax.broadcasted_iota(jnp.int32, (n_rows, FFN_SUB_TILE), 0) & (seq_len - 1)
    has_prev = row != 0
    has_next = row != seq_len - 1

    def conv(u, cw, cb):
        prev = jnp.where(has_prev, pltpu.roll(u, 1, 0), 0.0)
        nxt = jnp.where(has_next, pltpu.roll(u, n_rows - 1, 0), 0.0)
        return ((cb + prev * cw[0:1]) + u * cw[1:2]) + nxt * cw[2:3]

    down = None
    for c0 in range(0, FFN_COL_TILE, FFN_SUB_TILE):
        cols = slice(c0, c0 + FFN_SUB_TILE)
        gate = conv(_dot(h2, wg_ref[:, cols]), cwg_ref[:, cols], cbg_ref[:, cols])
        val = conv(_dot(h2, wv_ref[:, cols]), cwv_ref[:, cols], cbv_ref[:, cols])
        act = (gate * _sigmoid(gate) * val).astype(BF16)
        part = _dot(act, wd_ref[cols, :])
        down = part if down is None else down + part
    acc_ref[...] += down

    @pl.when(j == pl.num_programs(1) - 1)
    def _():
        m = mod_ref[0]
        gate2 = m[:, 5 * D_MODEL:6 * D_MODEL]
        o_ref[...] = x1_ref[...] + gate2 * _rms(acc_ref[...], gpost_ref[...])


def _ffn_call(h2, x1, mod3d, seq_len, wts):
    n_tok = h2.shape[0]
    tiles_per_seq = max(seq_len // FFN_TOK_TILE, 1)
    n_mod = mod3d.shape[0]
    n_col = D_FF_PAD // FFN_COL_TILE
    mod_map = (lambda i, j: (i // tiles_per_seq, 0, 0)) if n_mod > 1 else (lambda i, j: (0, 0, 0))
    tok = lambda: pl.BlockSpec((FFN_TOK_TILE, D_MODEL), lambda i, j: (i, 0))
    return pl.pallas_call(
        functools.partial(_ffn_kernel, seq_len=seq_len),
        grid=(n_tok // FFN_TOK_TILE, n_col),
        in_specs=[
            tok(), tok(), pl.BlockSpec((1, 1, 6 * D_MODEL), mod_map),
            pl.BlockSpec((D_MODEL, FFN_COL_TILE), lambda i, j: (0, j)),
            pl.BlockSpec((D_MODEL, FFN_COL_TILE), lambda i, j: (0, n_col + j)),
            pl.BlockSpec((3, FFN_COL_TILE), lambda i, j: (0, j)),
            pl.BlockSpec((3, FFN_COL_TILE), lambda i, j: (0, n_col + j)),
            pl.BlockSpec((1, FFN_COL_TILE), lambda i, j: (0, j)),
            pl.BlockSpec((1, FFN_COL_TILE), lambda i, j: (0, n_col + j)),
            pl.BlockSpec((FFN_COL_TILE, D_MODEL), lambda i, j: (j, 0)),
            pl.BlockSpec((1, D_MODEL), lambda i, j: (0, 0)),
        ],
        out_specs=tok(),
        out_shape=jax.ShapeDtypeStruct((n_tok, D_MODEL), F32),
        scratch_shapes=[pltpu.VMEM((FFN_TOK_TILE, D_MODEL), F32)],
        compiler_params=pltpu.CompilerParams(dimension_semantics=("arbitrary", "arbitrary"),
                                             vmem_limit_bytes=VMEM_LIMIT),
        name="conv_ffn",
    )(h2, x1, mod3d, wts["ffn_up"], wts["ffn_up"], wts["conv_w"], wts["conv_w"],
      wts["conv_b"], wts["conv_b"], wts["ffn_down"], wts["g_ffn_post"])


def _rope_tables(n_tok, n_heads):
    quarter = HEAD_DIM // 4
    pos = jnp.arange(n_tok)
    row = (pos // GRID_W).astype(F32)
    col = (pos % GRID_W).astype(F32)
    inv_freq = ROPE_THETA ** (-jnp.arange(quarter, dtype=F32) / quarter)

    def half(p):
        ang = p[:, None] * inv_freq[None, :]
        c, s = jnp.cos(ang), jnp.sin(ang)
        return jnp.concatenate([c, c], axis=-1), jnp.concatenate([-s, s], axis=-1)

    cr, sr = half(row)
    cc, sc = half(col)
    cos = jnp.concatenate([cr, cc], axis=-1)
    sin = jnp.concatenate([sr, sc], axis=-1)
    return jnp.tile(cos, (1, n_heads)), jnp.tile(sin, (1, n_heads))


def _pad_ff_cols(x):
    pad = ((0, 0), (0, D_FF_PAD - D_FF))
    return jnp.concatenate([jnp.pad(x[:, :D_FF], pad), jnp.pad(x[:, D_FF:], pad)], axis=1)


def _mixer(pre, n_batch, seq_len, packed, k_a, s0=None):
    r, v, kk, k, w, a = pre[:6]
    shared = tuple(_to_chains_call(x, n_batch, seq_len, packed) for x in (r, kk, k, v))
    ka_tab = jnp.repeat(k_a.reshape(H_RWKV, HEAD_DIM).T, n_batch, axis=1)
    ka_tab = jnp.tile(ka_tab, (1, LANES // ka_tab.shape[1]))
    y_f, y_b, s_fin = _scan_call(shared, _to_chains_call(w, n_batch, seq_len, False),
                                 _to_chains_call(a, n_batch, seq_len, False), ka_tab, packed, s0)
    return _from_chains_call(y_f, y_b, n_batch, seq_len, packed), s_fin


def kernel(x_prompt, x_sample, cache_k, cache_v, state_rwkv, c, c_ctx, w_mod, b_mod, norm_mix_pre, norm_mix_post, norm_ffn_pre, norm_ffn_post, w_in, w0, w_up, a0, a_up, g_up, k_k, k_a, r_k, gn_w, gn_b, q_norm, k_norm, w_out, ffn_up, conv_w, conv_b, ffn_down):
    n_ctx, l_ctx = x_prompt.shape[0], x_prompt.shape[1]
    n_lat, l_lat = x_sample.shape[0], x_sample.shape[1]
    l_past = cache_k.shape[2]
    layer = 0
    assert n_ctx * H_RWKV == LANES and 2 * n_lat * H_RWKV == LANES

    zeros_w = jnp.zeros((LORA_W, D_RWKV), F32)
    w_lora = jnp.concatenate([
        jnp.concatenate([w_up[layer, 0], w_up[layer, 1], zeros_w, zeros_w], axis=1),
        jnp.concatenate([zeros_w, zeros_w, a_up[layer, 0], a_up[layer, 1]], axis=1)], axis=0)
    vec = jnp.stack([k_k[layer], k_a[layer], r_k[layer].reshape(D_RWKV), jnp.tile(q_norm[layer], H_Q)])
    wts = dict(
        g_pre=norm_mix_pre[layer].reshape(1, D_MODEL),
        g_post=norm_mix_post[layer].reshape(1, D_MODEL),
        g_ffn_pre=norm_ffn_pre[layer].reshape(1, D_MODEL),
        g_ffn_post=norm_ffn_post[layer].reshape(1, D_MODEL),
        w_in=w_in[layer].astype(BF16),
        w_lora=w_lora,
        g_up=g_up[layer].astype(BF16),
        vec=vec,
        wa0=jnp.stack([w0[layer].reshape(2 * D_RWKV), a0[layer].reshape(2 * D_RWKV)]),
        k_norm=jnp.tile(k_norm[layer], H_KV).reshape(1, D_KV),
        gn=jnp.stack([gn_w[layer], gn_b[layer]]),
        w_out=w_out[layer].astype(BF16),
        ffn_up=_pad_ff_cols(ffn_up[layer].astype(BF16)),
        conv_w=_pad_ff_cols(conv_w[layer]),
        conv_b=_pad_ff_cols(conv_b[layer].reshape(1, 2 * D_FF)),
        ffn_down=jnp.pad(ffn_down[layer].astype(BF16), ((0, D_FF_PAD - D_FF), (0, 0))),
    )

    cvec = jnp.concatenate([c, c_ctx[None, :], jnp.zeros((16 - n_lat - 1, D_MODEL), F32)], axis=0)
    mod = _mod_call(cvec, w_mod[layer], b_mod[layer])
    mod_lat = mod[:n_lat].reshape(n_lat, 1, 6 * D_MODEL)
    mod_ctx = mod[n_lat:n_lat + 1].reshape(1, 1, 6 * D_MODEL)

    xc = x_prompt.reshape(n_ctx * l_ctx, D_MODEL)
    pre_c = _pre_call(xc, mod_ctx, l_ctx, wts, None)
    g_c, bonus_c, q_c, ka_c, va_c = pre_c[6:]
    ys_c, s_c = _mixer(pre_c, n_ctx, l_ctx, False, k_a[layer])
    attn_c = _attn_call(q_c, ka_c, va_c, n_ctx, l_ctx, l_ctx)
    x1_c, h2_c = _post_call(xc, mod_ctx, l_ctx, ys_c, bonus_c, g_c, attn_c, wts)
    out_c = _ffn_call(h2_c, x1_c, mod_ctx, l_ctx, wts)

    xl = x_sample.reshape(n_lat * l_lat, D_MODEL)
    tabs = _rope_tables(l_lat, H_Q) + _rope_tables(l_lat, H_KV)
    pre_l = _pre_call(xl, mod_lat, l_lat, wts, tabs)
    g_l, bonus_l, q_l, kr_l, va_l = pre_l[6:]
    s0 = state_rwkv[:, layer].transpose(4, 3, 1, 2, 0).reshape(1, HEAD_DIM, HEAD_DIM, LANES)
    ys_l, _ = _mixer(pre_l, n_lat, l_lat, True, k_a[layer], s0)
    k_all = jnp.concatenate([kr_l.reshape(n_lat, l_lat, D_KV), cache_k[:, layer].reshape(n_lat, l_past, D_KV)],
                            axis=1).reshape(n_lat * (l_lat + l_past), D_KV)
    v_all = jnp.concatenate([va_l.reshape(n_lat, l_lat, D_KV), cache_v[:, layer].reshape(n_lat, l_past, D_KV)],
                            axis=1).reshape(n_lat * (l_lat + l_past), D_KV)
    attn_l = _attn_call(q_l, k_all, v_all, n_lat, l_lat, l_lat + l_past)
    x1_l, h2_l = _post_call(xl, mod_lat, l_lat, ys_l, bonus_l, g_l, attn_l, wts)
    out_l = _ffn_call(h2_l, x1_l, mod_lat, l_lat, wts)

    y_prompt = out_c.reshape(n_ctx, l_ctx, D_MODEL)
    y_sample = out_l.reshape(n_lat, l_lat, D_MODEL)
    new_cache_k = ka_c.reshape(n_ctx, 1, l_ctx, H_KV, HEAD_DIM)
    new_cache_v = va_c.reshape(n_ctx, 1, l_ctx, H_KV, HEAD_DIM)
    new_state = s_c.reshape(2, HEAD_DIM, HEAD_DIM, H_RWKV, n_ctx).transpose(4, 0, 3, 2, 1)[:, None]
    return (y_prompt, y_sample, new_cache_k, new_cache_v, new_state)
```

```python
import functools

import jax
import jax.numpy as jnp
from jax import lax
from jax.experimental import pallas as pl
from jax.experimental.pallas import tpu as pltpu

F32 = jnp.float32
BF16 = jnp.bfloat16

D_MODEL = 1024
HEAD_DIM = 64
D_RWKV = 512
H_RWKV = 8
D_ATTN = 512
H_Q = 8
H_KV = 2
D_KV = 128
LORA_W = 64
LORA_A = 64
LORA_G = 128
D_FF = 2816
D_IN = 2560
GRID_W = 64
ROPE_THETA = 10000.0
RMS_EPS = 1e-6
GN_EPS = 64e-5

LANES = 128
SUBLANES = 8
TOK_TILE = 512
FFN_TOK_TILE = 1024
FFN_COL_TILE = 512
FFN_SUB_TILE = 256
D_FF_PAD = -(-D_FF // FFN_COL_TILE) * FFN_COL_TILE
SCAN_STEPS = 32
SCAN_K_UNROLL = 16
RELAYOUT_TOK = 128
RELAYOUT_BATCH_UNROLL = 4
CHAIN_PITCH = 72
ATTN_Q_TILE = 512
MOD_COL_TILE = 768
VMEM_LIMIT = 56 * 1024 * 1024


def _dot(a, b):
    return jnp.dot(a, b, preferred_element_type=F32)


def _split_bf16(x):
    hi = x.astype(BF16)
    lo = (x - hi.astype(F32)).astype(BF16)
    return hi, lo


def _dot_exact_rhs(x, w_bf16):
    return _dot(x.astype(BF16), w_bf16)


def _dot3(a, b):
    ah, al = _split_bf16(a)
    bh, bl = _split_bf16(b)
    return _dot(ah, bh) + (_dot(ah, bl) + _dot(al, bh))


def _head_ones(n):
    shift = HEAD_DIM.bit_length() - 1
    r = lax.shift_right_logical(lax.broadcasted_iota(jnp.int32, (n, n), 0), shift)
    c = lax.shift_right_logical(lax.broadcasted_iota(jnp.int32, (n, n), 1), shift)
    return jnp.where(r == c, 1.0, 0.0).astype(BF16)


def _sigmoid(x):
    return 1.0 / (1.0 + jnp.exp(-x))


def _softplus(x):
    return jnp.maximum(x, 0.0) + jnp.log(1.0 + jnp.exp(-jnp.abs(x)))


def _rms(x, g):
    return x * lax.rsqrt(jnp.mean(x * x, axis=-1, keepdims=True) + RMS_EPS) * g


def _rope(x, cos, sgn_sin):
    n = x.shape[-1]
    lane = lax.broadcasted_iota(jnp.int32, x.shape, 1)
    partner = jnp.where((lane & 16) == 0, pltpu.roll(x, n - 16, 1), pltpu.roll(x, 16, 1))
    return x * cos + partner * sgn_sin


def _mod_kernel(c_ref, w_ref, b_ref, o_ref):
    c = c_ref[...]
    o_ref[...] = _dot3(c * _sigmoid(c), w_ref[...]) + b_ref[...]


def _mod_call(cvec, w_mod, b_mod):
    n = w_mod.shape[1]
    rows = cvec.shape[0]
    return pl.pallas_call(
        _mod_kernel,
        grid=(n // MOD_COL_TILE,),
        in_specs=[
            pl.BlockSpec((rows, D_MODEL), lambda j: (0, 0)),
            pl.BlockSpec((D_MODEL, MOD_COL_TILE), lambda j: (0, j)),
            pl.BlockSpec((1, MOD_COL_TILE), lambda j: (0, j)),
        ],
        out_specs=pl.BlockSpec((rows, MOD_COL_TILE), lambda j: (0, j)),
        out_shape=jax.ShapeDtypeStruct((rows, n), F32),
        compiler_params=pltpu.CompilerParams(dimension_semantics=("arbitrary",)),
        name="adaln_mod",
    )(cvec, w_mod, b_mod.reshape(1, n))


_V_KK, _V_KA, _V_RK, _V_QN = range(4)


def _pre_kernel(*refs, rope):
    if rope:
        (x_ref, mod_ref, gpre_ref, win_ref, wlora_ref, gup_ref, vec_ref, wa0_ref, kn_ref,
         cq_ref, sq_ref, ck_ref, sk_ref,
         r_o, v_o, kk_o, k_o, w_o, a_o, g_o, bonus_o, q_o, ka_o, va_o) = refs
    else:
        (x_ref, mod_ref, gpre_ref, win_ref, wlora_ref, gup_ref, vec_ref, wa0_ref, kn_ref,
         r_o, v_o, kk_o, k_o, w_o, a_o, g_o, bonus_o, q_o, ka_o, va_o) = refs

    x = x_ref[...]
    m = mod_ref[0]
    shift1 = m[:, 0:D_MODEL]
    scale1 = m[:, D_MODEL:2 * D_MODEL]
    h = _rms(x, gpre_ref[...]) * (1.0 + scale1) + shift1
    p = _dot(h.astype(BF16), win_ref[...])

    r = p[:, 0:512]
    k = p[:, 512:1024]
    v = p[:, 1024:1536]
    xwa = p[:, 1536:1664]
    xg = p[:, 1664:1792]
    q = p[:, 1792:2304]
    ka = p[:, 2304:2432]
    va = p[:, 2432:2560]

    vec = vec_ref[...]
    ones512 = _head_ones(D_RWKV)

    g_o[...] = _dot(_sigmoid(xg).astype(BF16), gup_ref[...])

    kx = k * vec[_V_KK:_V_KK + 1]
    kk_o[...] = kx * lax.rsqrt(_dot_exact_rhs(kx * kx, ones512) + 1e-12)

    lane = lax.broadcasted_iota(jnp.int32, xwa.shape, 1)
    lora_in = jnp.where(lane < LORA_W, jnp.tanh(xwa), xwa)
    lora = _dot3(lora_in, wlora_ref[...])

    z = wa0_ref[0:1] + lora[:, 0:2 * D_RWKV]
    logw = -_softplus(-z) - 0.5
    w_o[...] = jnp.exp(-jnp.exp(logw))
    a = _sigmoid(wa0_ref[1:2] + lora[:, 2 * D_RWKV:4 * D_RWKV])
    a_o[...] = a

    k_a = vec[_V_KA:_V_KA + 1]
    kd_sum = k * (1.0 + (a[:, 0:D_RWKV] - 1.0) * k_a) + k * (1.0 + (a[:, D_RWKV:2 * D_RWKV] - 1.0) * k_a)
    bonus_o[...] = _dot_exact_rhs(r * kd_sum * vec[_V_RK:_V_RK + 1], ones512) * v

    r_o[...] = r
    v_o[...] = v
    k_o[...] = k

    qn = q * lax.rsqrt(_dot_exact_rhs(q * q, ones512) * (1.0 / HEAD_DIM) + RMS_EPS) * vec[_V_QN:_V_QN + 1]
    kan = ka * lax.rsqrt(_dot_exact_rhs(ka * ka, _head_ones(D_KV)) * (1.0 / HEAD_DIM) + RMS_EPS) * kn_ref[...]
    if rope:
        qn = _rope(qn, cq_ref[...], sq_ref[...])
        kan = _rope(kan, ck_ref[...], sk_ref[...])
    q_o[...] = qn
    ka_o[...] = kan
    va_o[...] = va


def _pre_call(x2d, mod3d, seq_len, wts, rope_tabs):
    n_tok = x2d.shape[0]
    tiles_per_seq = seq_len // TOK_TILE
    n_mod = mod3d.shape[0]
    rope = rope_tabs is not None

    def const(shape):
        return pl.BlockSpec(shape, lambda i: tuple(0 for _ in shape))

    mod_map = (lambda i: (i // tiles_per_seq, 0, 0)) if n_mod > 1 else (lambda i: (0, 0, 0))
    in_specs = [
        pl.BlockSpec((TOK_TILE, D_MODEL), lambda i: (i, 0)),
        pl.BlockSpec((1, 1, 6 * D_MODEL), mod_map),
        const((1, D_MODEL)),
        const((D_MODEL, D_IN)),
        const((LORA_W + LORA_A, 4 * D_RWKV)),
        const((LORA_G, D_RWKV)),
        const((4, D_RWKV)),
        const((2, 2 * D_RWKV)),
        const((1, D_KV)),
    ]
    args = [x2d, mod3d, wts["g_pre"], wts["w_in"], wts["w_lora"], wts["g_up"], wts["vec"], wts["wa0"],
            wts["k_norm"]]
    if rope:
        tab_map = lambda i: (i % tiles_per_seq, 0)
        in_specs += [pl.BlockSpec((TOK_TILE, D_ATTN), tab_map), pl.BlockSpec((TOK_TILE, D_ATTN), tab_map),
                     pl.BlockSpec((TOK_TILE, D_KV), tab_map), pl.BlockSpec((TOK_TILE, D_KV), tab_map)]
        args += list(rope_tabs)

    widths = [512] * 4 + [1024, 1024] + [512] * 3 + [D_KV, D_KV]
    out_specs = [pl.BlockSpec((TOK_TILE, wd), lambda i: (i, 0)) for wd in widths]
    out_shape = [jax.ShapeDtypeStruct((n_tok, wd), F32) for wd in widths]
    return pl.pallas_call(
        functools.partial(_pre_kernel, rope=rope),
        grid=(n_tok // TOK_TILE,),
        in_specs=in_specs,
        out_specs=out_specs,
        out_shape=out_shape,
        compiler_params=pltpu.CompilerParams(dimension_semantics=("arbitrary",),
                                             vmem_limit_bytes=VMEM_LIMIT),
        name="pre_rope" if rope else "pre",
    )(*args)


def _to_chains_kernel(x_ref, o_ref, slab_ref, *, n_batch, dup):
    n_col = x_ref.shape[2] // HEAD_DIM
    n_grp = o_ref.shape[0]

    def stage1(b, carry):
        for jc in range(n_col // 2):
            a_t = x_ref[b, :, LANES * jc:LANES * (jc + 1)].T
            for h2 in range(2):
                row0 = pl.multiple_of(((2 * jc + h2) * n_batch + b) * CHAIN_PITCH, SUBLANES)
                slab_ref[pl.ds(row0, HEAD_DIM), :] = a_t[HEAD_DIM * h2:HEAD_DIM * (h2 + 1), :]
        return carry

    lax.fori_loop(0, n_batch, stage1, 0, unroll=RELAYOUT_BATCH_UNROLL)

    def stage2(kb, carry):
        for ki in range(SUBLANES):
            k = kb * SUBLANES + ki
            for d in range(n_grp):
                if dup:
                    half = slab_ref[pl.ds(k, LANES // 2, stride=CHAIN_PITCH), :]
                    tile = jnp.concatenate([half, half], axis=0)
                else:
                    tile = slab_ref[pl.ds(d * LANES * CHAIN_PITCH + k, LANES, stride=CHAIN_PITCH), :]
                o_ref.at[d][pl.ds(k, RELAYOUT_TOK, stride=CHAIN_PITCH), :] = tile.T
        return carry

    lax.fori_loop(0, HEAD_DIM // SUBLANES, stage2, 0)

    zero = jnp.zeros((RELAYOUT_TOK, LANES), F32)
    for d in range(n_grp):
        for pad in range(HEAD_DIM, CHAIN_PITCH):
            o_ref.at[d][pl.ds(pad, RELAYOUT_TOK, stride=CHAIN_PITCH), :] = zero


def _to_chains_call(x2d, n_batch, seq_len, dup):
    n_chan = x2d.shape[1]
    n_slab = (n_chan // HEAD_DIM) * n_batch
    n_grp = n_slab * (2 if dup else 1) // LANES
    out = pl.pallas_call(
        functools.partial(_to_chains_kernel, n_batch=n_batch, dup=dup),
        grid=(seq_len // RELAYOUT_TOK,),
        in_specs=[pl.BlockSpec((n_batch, RELAYOUT_TOK, n_chan), lambda i: (0, i, 0))],
        out_specs=pl.BlockSpec((n_grp, RELAYOUT_TOK * CHAIN_PITCH, LANES), lambda i: (0, i, 0)),
        out_shape=jax.ShapeDtypeStruct((n_grp, seq_len * CHAIN_PITCH, LANES), F32),
        scratch_shapes=[pltpu.VMEM((n_slab * CHAIN_PITCH, LANES), F32)],
        compiler_params=pltpu.CompilerParams(dimension_semantics=("arbitrary",),
                                             vmem_limit_bytes=VMEM_LIMIT),
        name="to_chains_dup" if dup else "to_chains",
    )(x2d.reshape(n_batch, seq_len, n_chan))
    return out.reshape(n_grp, seq_len, CHAIN_PITCH, LANES)


def _from_chains_kernel(yf_ref, yb_ref, o_ref, slab_ref, *, n_batch, packed):
    def stage1(vb, carry):
        for vi in range(SUBLANES):
            v = vb * SUBLANES + vi
            tf = yf_ref[pl.ds(v, RELAYOUT_TOK, stride=CHAIN_PITCH), :]
            tb = yb_ref[pl.ds(v, RELAYOUT_TOK, stride=CHAIN_PITCH), :]
            if packed:
                tb = pltpu.roll(tb, LANES // 2, 1)
            slab_ref[pl.ds(v, LANES, stride=CHAIN_PITCH), :] = (tf + tb).T
        return carry

    lax.fori_loop(0, HEAD_DIM // SUBLANES, stage1, 0)

    def stage2(b, carry):
        for jc in range(H_RWKV // 2):
            rows = []
            for h2 in range(2):
                row0 = pl.multiple_of(((2 * jc + h2) * n_batch + b) * CHAIN_PITCH, SUBLANES)
                rows.append(slab_ref[pl.ds(row0, HEAD_DIM), :])
            o_ref[b, :, LANES * jc:LANES * (jc + 1)] = jnp.concatenate(rows, axis=0).T
        return carry

    lax.fori_loop(0, n_batch, stage2, 0, unroll=RELAYOUT_BATCH_UNROLL)


def _from_chains_call(y_f, y_b, n_batch, seq_len, packed):
    n_rows = seq_len * CHAIN_PITCH
    grp_b = 0 if packed else 1
    blk = (None, RELAYOUT_TOK * CHAIN_PITCH, LANES)
    out = pl.pallas_call(
        functools.partial(_from_chains_kernel, n_batch=n_batch, packed=packed),
        grid=(seq_len // RELAYOUT_TOK,),
        in_specs=[pl.BlockSpec(blk, lambda i: (0, i, 0)), pl.BlockSpec(blk, lambda i: (grp_b, i, 0))],
        out_specs=pl.BlockSpec((n_batch, RELAYOUT_TOK, D_RWKV), lambda i: (0, i, 0)),
        out_shape=jax.ShapeDtypeStruct((n_batch, seq_len, D_RWKV), F32),
        scratch_shapes=[pltpu.VMEM((LANES * CHAIN_PITCH, LANES), F32)],
        compiler_params=pltpu.CompilerParams(dimension_semantics=("arbitrary",),
                                             vmem_limit_bytes=VMEM_LIMIT),
        name="from_chains_packed" if packed else "from_chains",
    )(y_f.reshape(-1, n_rows, LANES), y_b.reshape(-1, n_rows, LANES))
    return out.reshape(n_batch * seq_len, D_RWKV)


def _scan_kernel(*refs, packed, has_init):
    (rF, rB, kkF, kkB, kF, kB, vF, vB, wF, wB, aF, aB, ka_ref), rest = refs[:13], refs[13:]
    if has_init:
        s0_ref, rest = rest[0], rest[1:]
    yF, yB, sfin_ref, s_scr, sa_scr, op_scr = rest
    g = pl.program_id(0)
    j = pl.program_id(1)
    tile = (HEAD_DIM, LANES)

    if packed:
        use_b = lax.broadcasted_iota(jnp.int32, tile, 1) >= LANES // 2
    else:
        use_b = jnp.full(tile, g, jnp.int32) == 1

    def pick(f_ref, b_ref, s):
        return jnp.where(use_b, b_ref[SCAN_STEPS - 1 - s, 0:HEAD_DIM, :], f_ref[s, 0:HEAD_DIM, :])

    @pl.when(j == 0)
    def _():
        if has_init:
            s_scr[...] = s0_ref[...]
        else:
            s_scr[...] = jnp.zeros(s_scr.shape, F32)

    op_scr[4] = pick(kkF, kkB, 0)
    acc = jnp.zeros(tile, F32)
    for k in range(HEAD_DIM):
        acc = acc + s_scr[k] * op_scr[4, k:k + 1, :]
    sa_scr[...] = acc

    ka = ka_ref[...]
    pad = jnp.zeros((SCAN_STEPS, CHAIN_PITCH - HEAD_DIM, LANES), F32)
    yF[:, HEAD_DIM:CHAIN_PITCH, :] = pad
    yB[:, HEAD_DIM:CHAIN_PITCH, :] = pad

    def step(s, carry):
        a = pick(aF, aB, s)
        op_scr[0] = pick(wF, wB, s)
        op_scr[1] = pick(kkF, kkB, s) * a
        op_scr[2] = pick(kF, kB, s) * (1.0 + (a - 1.0) * ka)
        op_scr[3] = pick(rF, rB, s)
        op_scr[4] = pick(kkF, kkB, jnp.minimum(s + 1, SCAN_STEPS - 1))
        vt = pick(vF, vB, s)
        sa = sa_scr[...]

        def kblock(kb, acc):
            y, san = acc
            for ki in range(SCAN_K_UNROLL):
                k = kb * SCAN_K_UNROLL + ki
                wk = op_scr[0, pl.ds(k, 1), :]
                bk = op_scr[1, pl.ds(k, 1), :]
                kdk = op_scr[2, pl.ds(k, 1), :]
                rk = op_scr[3, pl.ds(k, 1), :]
                kn = op_scr[4, pl.ds(k, 1), :]
                sn = s_scr[k] * wk - sa * bk + vt * kdk
                s_scr[k] = sn
                y = y + sn * rk
                san = san + sn * kn
            return y, san

        zero = jnp.zeros(tile, F32)
        y, san = lax.fori_loop(0, HEAD_DIM // SCAN_K_UNROLL, kblock, (zero, zero))
        yF[s, 0:HEAD_DIM, :] = y
        yB[SCAN_STEPS - 1 - s, 0:HEAD_DIM, :] = y
        sa_scr[...] = san
        return carry

    lax.fori_loop(0, SCAN_STEPS, step, 0)

    @pl.when(j == pl.num_programs(1) - 1)
    def _():
        sfin_ref[...] = s_scr[...]


def _scan_call(shared, w, a, ka_tab, packed, s0=None):
    n_step = w.shape[1]
    n_grp = 1 if packed else 2
    n_blk = n_step // SCAN_STEPS
    has_init = s0 is not None
    blk = (None, SCAN_STEPS, CHAIN_PITCH, LANES)
    if packed:
        f_map = lambda g, j: (0, j, 0, 0)
        b_map = lambda g, j: (0, n_blk - 1 - j, 0, 0)
        fd_map, bd_map = f_map, b_map
    else:
        f_map = lambda g, j: (0, jnp.where(g == 0, j, 0), 0, 0)
        b_map = lambda g, j: (0, jnp.where(g == 1, n_blk - 1 - j, 0), 0, 0)
        fd_map = lambda g, j: (0, jnp.where(g == 0, j, 0), 0, 0)
        bd_map = lambda g, j: (1, jnp.where(g == 1, n_blk - 1 - j, 0), 0, 0)
    in_specs, args = [], []
    for x in shared:
        in_specs += [pl.BlockSpec(blk, f_map), pl.BlockSpec(blk, b_map)]
        args += [x, x]
    for x in (w, a):
        in_specs += [pl.BlockSpec(blk, fd_map), pl.BlockSpec(blk, bd_map)]
        args += [x, x]
    in_specs.append(pl.BlockSpec((HEAD_DIM, LANES), lambda g, j: (0, 0)))
    args.append(ka_tab)
    state_spec = pl.BlockSpec((None, HEAD_DIM, HEAD_DIM, LANES), lambda g, j: (g, 0, 0, 0))
    if has_init:
        in_specs.append(state_spec)
        args.append(s0)
    y_shape = jax.ShapeDtypeStruct((n_grp, n_step, CHAIN_PITCH, LANES), F32)
    return pl.pallas_call(
        functools.partial(_scan_kernel, packed=packed, has_init=has_init),
        grid=(n_grp, n_blk),
        in_specs=in_specs,
        out_specs=[pl.BlockSpec(blk, lambda g, j: (g, j, 0, 0)),
                   pl.BlockSpec(blk, lambda g, j: (g, n_blk - 1 - j, 0, 0)),
                   state_spec],
        out_shape=[y_shape, y_shape, jax.ShapeDtypeStruct((n_grp, HEAD_DIM, HEAD_DIM, LANES), F32)],
        scratch_shapes=[pltpu.VMEM((HEAD_DIM, HEAD_DIM, LANES), F32), pltpu.VMEM((HEAD_DIM, LANES), F32),
                        pltpu.VMEM((5, HEAD_DIM, LANES), F32)],
        compiler_params=pltpu.CompilerParams(dimension_semantics=("arbitrary", "arbitrary"),
                                             vmem_limit_bytes=VMEM_LIMIT),
        name="wkv_scan_packed" if packed else "wkv_scan",
    )(*args)


def _attn_kernel(q_ref, k_ref, v_ref, o_ref):
    q = q_ref[...]
    k = k_ref[...]
    v = v_ref[...]
    group = H_Q // H_KV
    for kvh in range(H_KV):
        kh = k[:, HEAD_DIM * kvh:HEAD_DIM * (kvh + 1)].astype(BF16)
        vh = v[:, HEAD_DIM * kvh:HEAD_DIM * (kvh + 1)].astype(BF16)
        for gq in range(group):
            hq = kvh * group + gq
            qh = (q[:, HEAD_DIM * hq:HEAD_DIM * (hq + 1)] * (HEAD_DIM ** -0.5)).astype(BF16)
            s = lax.dot_general(qh, kh, (((1,), (1,)), ((), ())), preferred_element_type=F32)
            e = jnp.exp(s - jnp.max(s, axis=-1, keepdims=True))
            o = _dot(e.astype(BF16), vh) / jnp.sum(e, axis=-1, keepdims=True)
            o_ref[:, HEAD_DIM * hq:HEAD_DIM * (hq + 1)] = o


def _attn_call(q2d, k2d, v2d, n_batch, lq, lk):
    q_tile = min(ATTN_Q_TILE, lq)
    q_tiles = lq // q_tile
    return pl.pallas_call(
        _attn_kernel,
        grid=(n_batch, q_tiles),
        in_specs=[
            pl.BlockSpec((q_tile, D_ATTN), lambda b, i: (b * q_tiles + i, 0)),
            pl.BlockSpec((lk, D_KV), lambda b, i: (b, 0)),
            pl.BlockSpec((lk, D_KV), lambda b, i: (b, 0)),
        ],
        out_specs=pl.BlockSpec((q_tile, D_ATTN), lambda b, i: (b * q_tiles + i, 0)),
        out_shape=jax.ShapeDtypeStruct((n_batch * lq, D_ATTN), F32),
        compiler_params=pltpu.CompilerParams(dimension_semantics=("arbitrary", "arbitrary"),
                                             vmem_limit_bytes=VMEM_LIMIT),
        name="gqa_attn",
    )(q2d, k2d, v2d)


def _post_kernel(x_ref, mod_ref, ys_ref, bonus_ref, g_ref, attn_ref,
                 gn_ref, wout_ref, gpost_ref, gffn_ref, x1_o, h2_o):
    ones512 = _head_ones(D_RWKV)
    ys = ys_ref[...]
    ys_hi, ys_lo = _split_bf16(ys)
    mu = (_dot(ys_hi, ones512) + _dot(ys_lo, ones512)) * (1.0 / HEAD_DIM)
    dlt = ys - mu
    var = _dot_exact_rhs(dlt * dlt, ones512) * (1.0 / HEAD_DIM)
    gn = gn_ref[...]
    yn = dlt * lax.rsqrt(var + GN_EPS) * gn[0:1] + gn[1:2]
    yr = (yn + bonus_ref[...]) * g_ref[...]
    mix = (_dot(yr.astype(BF16), wout_ref[0:D_RWKV, :])
           + _dot(attn_ref[...].astype(BF16), wout_ref[D_RWKV:D_MODEL, :]))
    m = mod_ref[0]
    gate1 = m[:, 2 * D_MODEL:3 * D_MODEL]
    shift2 = m[:, 3 * D_MODEL:4 * D_MODEL]
    scale2 = m[:, 4 * D_MODEL:5 * D_MODEL]
    x1 = x_ref[...] + gate1 * _rms(mix, gpost_ref[...])
    x1_o[...] = x1
    h2_o[...] = (_rms(x1, gffn_ref[...]) * (1.0 + scale2) + shift2).astype(BF16)


def _post_call(x2d, mod3d, seq_len, ys, bonus, g, attn, wts):
    n_tok = x2d.shape[0]
    tiles_per_seq = seq_len // TOK_TILE
    n_mod = mod3d.shape[0]

    def const(shape):
        return pl.BlockSpec(shape, lambda i: tuple(0 for _ in shape))

    mod_map = (lambda i: (i // tiles_per_seq, 0, 0)) if n_mod > 1 else (lambda i: (0, 0, 0))
    tok = lambda wd: pl.BlockSpec((TOK_TILE, wd), lambda i: (i, 0))
    return pl.pallas_call(
        _post_kernel,
        grid=(n_tok // TOK_TILE,),
        in_specs=[tok(D_MODEL), pl.BlockSpec((1, 1, 6 * D_MODEL), mod_map),
                  tok(512), tok(512), tok(512), tok(512),
                  const((2, D_RWKV)), const((D_MODEL, D_MODEL)), const((1, D_MODEL)), const((1, D_MODEL))],
        out_specs=[tok(D_MODEL), tok(D_MODEL)],
        out_shape=[jax.ShapeDtypeStruct((n_tok, D_MODEL), F32), jax.ShapeDtypeStruct((n_tok, D_MODEL), BF16)],
        compiler_params=pltpu.CompilerParams(dimension_semantics=("arbitrary",),
                                             vmem_limit_bytes=VMEM_LIMIT),
        name="mix_post",
    )(x2d, mod3d, ys, bonus, g, attn, wts["gn"], wts["w_out"], wts["g_post"], wts["g_ffn_pre"])


def _ffn_kernel(h2_ref, x1_ref, mod_ref, wg_ref, wv_ref, cwg_ref, cwv_ref, cbg_ref, cbv_ref,
                wd_ref, gpost_ref, o_ref, acc_ref, *, seq_len):
    j = pl.program_id(1)

    @pl.when(j == 0)
    def _():
        acc_ref[...] = jnp.zeros(acc_ref.shape, F32)

    h2 = h2_ref[...]
    n_rows = h2.shape[0]
    assert seq_len & (seq_len - 1) == 0
    row = lax.broadcasted_iota(jnp.int32, (n_rows, FFN_SUB_TILE), 0) & (seq_len - 1)
    has_prev = row != 0
    has_next = row != seq_len - 1

    def conv(u, cw, cb):
        prev = jnp.where(has_prev, pltpu.roll(u, 1, 0), 0.0)
        nxt = jnp.where(has_next, pltpu.roll(u, n_rows - 1, 0), 0.0)
        return ((cb + prev * cw[0:1]) + u * cw[1:2]) + nxt * cw[2:3]

    down = None
    for c0 in range(0, FFN_COL_TILE, FFN_SUB_TILE):
        cols = slice(c0, c0 + FFN_SUB_TILE)
        gate = conv(_dot(h2, wg_ref[:, cols]), cwg_ref[:, cols], cbg_ref[:, cols])
        val = conv(_dot(h2, wv_ref[:, cols]), cwv_ref[:, cols], cbv_ref[:, cols])
        act = (gate * _sigmoid(gate) * val).astype(BF16)
        part = _dot(act, wd_ref[cols, :])
        down = part if down is None else down + part
    acc_ref[...] += down

    @pl.when(j == pl.num_programs(1) - 1)
    def _():
        m = mod_ref[0]
        gate2 = m[:, 5 * D_MODEL:6 * D_MODEL]
        o_ref[...] = x1_ref[...] + gate2 * _rms(acc_ref[...], gpost_ref[...])


def _ffn_call(h2, x1, mod3d, seq_len, wts):
    n_tok = h2.shape[0]
    tiles_per_seq = max(seq_len // FFN_TOK_TILE, 1)
    n_mod = mod3d.shape[0]
    n_col = D_FF_PAD // FFN_COL_TILE
    mod_map = (lambda i, j: (i // tiles_per_seq, 0, 0)) if n_mod > 1 else (lambda i, j: (0, 0, 0))
    tok = lambda: pl.BlockSpec((FFN_TOK_TILE, D_MODEL), lambda i, j: (i, 0))
    return pl.pallas_call(
        functools.partial(_ffn_kernel, seq_len=seq_len),
        grid=(n_tok // FFN_TOK_TILE, n_col),
        in_specs=[
            tok(), tok(), pl.BlockSpec((1, 1, 6 * D_MODEL), mod_map),
            pl.BlockSpec((D_MODEL, FFN_COL_TILE), lambda i, j: (0, j)),
            pl.BlockSpec((D_MODEL, FFN_COL_TILE), lambda i, j: (0, n_col + j)),
            pl.BlockSpec((3, FFN_COL_TILE), lambda i, j: (0, j)),
            pl.BlockSpec((3, FFN_COL_TILE), lambda i, j: (0, n_col + j)),
            pl.BlockSpec((1, FFN_COL_TILE), lambda i, j: (0, j)),
            pl.BlockSpec((1, FFN_COL_TILE), lambda i, j: (0, n_col + j)),
            pl.BlockSpec((FFN_COL_TILE, D_MODEL), lambda i, j: (j, 0)),
            pl.BlockSpec((1, D_MODEL), lambda i, j: (0, 0)),
        ],
        out_specs=tok(),
        out_shape=jax.ShapeDtypeStruct((n_tok, D_MODEL), F32),
        scratch_shapes=[pltpu.VMEM((FFN_TOK_TILE, D_MODEL), F32)],
        compiler_params=pltpu.CompilerParams(dimension_semantics=("arbitrary", "arbitrary"),
                                             vmem_limit_bytes=VMEM_LIMIT),
        name="conv_ffn",
    )(h2, x1, mod3d, wts["ffn_up"], wts["ffn_up"], wts["conv_w"], wts["conv_w"],
      wts["conv_b"], wts["conv_b"], wts["ffn_down"], wts["g_ffn_post"])


def _rope_tables(n_tok, n_heads):
    quarter = HEAD_DIM // 4
    pos = jnp.arange(n_tok)
    row = (pos // GRID_W).astype(F32)
    col = (pos % GRID_W).astype(F32)
    inv_freq = ROPE_THETA ** (-jnp.arange(quarter, dtype=F32) / quarter)

    def half(p):
        ang = p[:, None] * inv_freq[None, :]
        c, s = jnp.cos(ang), jnp.sin(ang)
        return jnp.concatenate([c, c], axis=-1), jnp.concatenate([-s, s], axis=-1)

    cr, sr = half(row)
    cc, sc = half(col)
    cos = jnp.concatenate([cr, cc], axis=-1)
    sin = jnp.concatenate([sr, sc], axis=-1)
    return jnp.tile(cos, (1, n_heads)), jnp.tile(sin, (1, n_heads))


def _pad_ff_cols(x):
    pad = ((0, 0), (0, D_FF_PAD - D_FF))
    return jnp.concatenate([jnp.pad(x[:, :D_FF], pad), jnp.pad(x[:, D_FF:], pad)], axis=1)


def _mixer(pre, n_batch, seq_len, packed, k_a, s0=None):
    r, v, kk, k, w, a = pre[:6]
    shared = tuple(_to_chains_call(x, n_batch, seq_len, packed) for x in (r, kk, k, v))
    ka_tab = jnp.repeat(k_a.reshape(H_RWKV, HEAD_DIM).T, n_batch, axis=1)
    ka_tab = jnp.tile(ka_tab, (1, LANES // ka_tab.shape[1]))
    y_f, y_b, s_fin = _scan_call(shared, _to_chains_call(w, n_batch, seq_len, False),
                                 _to_chains_call(a, n_batch, seq_len, False), ka_tab, packed, s0)
    return _from_chains_call(y_f, y_b, n_batch, seq_len, packed), s_fin


def kernel(x_prompt, x_sample, cache_k, cache_v, state_rwkv, c, c_ctx, w_mod, b_mod, norm_mix_pre, norm_mix_post, norm_ffn_pre, norm_ffn_post, w_in, w0, w_up, a0, a_up, g_up, k_k, k_a, r_k, gn_w, gn_b, q_norm, k_norm, w_out, ffn_up, conv_w, conv_b, ffn_down):
    n_ctx, l_ctx = x_prompt.shape[0], x_prompt.shape[1]
    n_lat, l_lat = x_sample.shape[0], x_sample.shape[1]
    l_past = cache_k.shape[2]
    layer = 0
    assert n_ctx * H_RWKV == LANES and 2 * n_lat * H_RWKV == LANES

    zeros_w = jnp.zeros((LORA_W, D_RWKV), F32)
    w_lora = jnp.concatenate([
        jnp.concatenate([w_up[layer, 0], w_up[layer, 1], zeros_w, zeros_w], axis=1),
        jnp.concatenate([zeros_w, zeros_w, a_up[layer, 0], a_up[layer, 1]], axis=1)], axis=0)
    vec = jnp.stack([k_k[layer], k_a[layer], r_k[layer].reshape(D_RWKV), jnp.tile(q_norm[layer], H_Q)])
    wts = dict(
        g_pre=norm_mix_pre[layer].reshape(1, D_MODEL),
        g_post=norm_mix_post[layer].reshape(1, D_MODEL),
        g_ffn_pre=norm_ffn_pre[layer].reshape(1, D_MODEL),
        g_ffn_post=norm_ffn_post[layer].reshape(1, D_MODEL),
        w_in=w_in[layer].astype(BF16),
        w_lora=w_lora,
        g_up=g_up[layer].astype(BF16),
        vec=vec,
        wa0=jnp.stack([w0[layer].reshape(2 * D_RWKV), a0[layer].reshape(2 * D_RWKV)]),
        k_norm=jnp.tile(k_norm[layer], H_KV).reshape(1, D_KV),
        gn=jnp.stack([gn_w[layer], gn_b[layer]]),
        w_out=w_out[layer].astype(BF16),
        ffn_up=_pad_ff_cols(ffn_up[layer].astype(BF16)),
        conv_w=_pad_ff_cols(conv_w[layer]),
        conv_b=_pad_ff_cols(conv_b[layer].reshape(1, 2 * D_FF)),
        ffn_down=jnp.pad(ffn_down[layer].astype(BF16), ((0, D_FF_PAD - D_FF), (0, 0))),
    )

    cvec = jnp.concatenate([c, c_ctx[None, :], jnp.zeros((16 - n_lat - 1, D_MODEL), F32)], axis=0)
    mod = _mod_call(cvec, w_mod[layer], b_mod[layer])
    mod_lat = mod[:n_lat].reshape(n_lat, 1, 6 * D_MODEL)
    mod_ctx = mod[n_lat:n_lat + 1].reshape(1, 1, 6 * D_MODEL)

    xc = x_prompt.reshape(n_ctx * l_ctx, D_MODEL)
    pre_c = _pre_call(xc, mod_ctx, l_ctx, wts, None)
    g_c, bonus_c, q_c, ka_c, va_c = pre_c[6:]
    ys_c, s_c = _mixer(pre_c, n_ctx, l_ctx, False, k_a[layer])
    attn_c = _attn_call(q_c, ka_c, va_c, n_ctx, l_ctx, l_ctx)
    x1_c, h2_c = _post_call(xc, mod_ctx, l_ctx, ys_c, bonus_c, g_c, attn_c, wts)
    out_c = _ffn_call(h2_c, x1_c, mod_ctx, l_ctx, wts)

    xl = x_sample.reshape(n_lat * l_lat, D_MODEL)
    tabs = _rope_tables(l_lat, H_Q) + _rope_tables(l_lat, H_KV)
    pre_l = _pre_call(xl, mod_lat, l_lat, wts, tabs)
    g_l, bonus_l, q_l, kr_l, va_l = pre_l[6:]
    s0 = state_rwkv[:, layer].transpose(4, 3, 1, 2, 0).reshape(1, HEAD_DIM, HEAD_DIM, LANES)
    ys_l, _ = _mixer(pre_l, n_lat, l_lat, True, k_a[layer], s0)
    k_all = jnp.concatenate([kr_l.reshape(n_lat, l_lat, D_KV), cache_k[:, layer].reshape(n_lat, l_past, D_KV)],
                            axis=1).reshape(n_lat * (l_lat + l_past), D_KV)
    v_all = jnp.concatenate([va_l.reshape(n_lat, l_lat, D_KV), cache_v[:, layer].reshape(n_lat, l_past, D_KV)],
                            axis=1).reshape(n_lat * (l_lat + l_past), D_KV)
    attn_l = _attn_call(q_l, k_all, v_all, n_lat, l_lat, l_lat + l_past)
    x1_l, h2_l = _post_call(xl, mod_lat, l_lat, ys_l, bonus_l, g_l, attn_l, wts)
    out_l = _ffn_call(h2_l, x1_l, mod_lat, l_lat, wts)

    y_prompt = out_c.reshape(n_ctx, l_ctx, D_MODEL)
    y_sample = out_l.reshape(n_lat, l_lat, D_MODEL)
    new_cache_k = ka_c.reshape(n_ctx, 1, l_ctx, H_KV, HEAD_DIM)
    new_cache_v = va_c.reshape(n_ctx, 1, l_ctx, H_KV, HEAD_DIM)
    new_state = s_c.reshape(2, HEAD_DIM, HEAD_DIM, H_RWKV, n_ctx).transpose(4, 0, 3, 2, 1)[:, None]
    return (y_prompt, y_sample, new_cache_k, new_cache_v, new_state)
```

```python
import functools

import jax
import jax.numpy as jnp
from jax import lax
from jax.experimental import pallas as pl
from jax.experimental.pallas import tpu as pltpu

F32 = jnp.float32
BF16 = jnp.bfloat16

D_MODEL = 1024
HEAD_DIM = 64
D_RWKV = 512
H_RWKV = 8
D_ATTN = 512
H_Q = 8
H_KV = 2
D_KV = 128
LORA_W = 64
LORA_A = 64
LORA_G = 128
D_FF = 2816
D_IN = 2560
GRID_W = 64
ROPE_THETA = 10000.0
RMS_EPS = 1e-6
GN_EPS = 64e-5

LANES = 128
SUBLANES = 8
TOK_TILE = 512
FFN_TOK_TILE = 1024
FFN_COL_TILE = 512
FFN_SUB_TILE = 256
D_FF_PAD = -(-D_FF // FFN_COL_TILE) * FFN_COL_TILE
SCAN_STEPS = 32
SCAN_K_UNROLL = 16
RELAYOUT_TOK = 128
RELAYOUT_BATCH_UNROLL = 4
CHAIN_PITCH = 72
ATTN_Q_TILE = 512
MOD_COL_TILE = 768
VMEM_LIMIT = 56 * 1024 * 1024


def _dot(a, b):
    return jnp.dot(a, b, preferred_element_type=F32)


def _split_bf16(x):
    hi = x.astype(BF16)
    lo = (x - hi.astype(F32)).astype(BF16)
    return hi, lo


def _dot_exact_rhs(x, w_bf16):
    return _dot(x.astype(BF16), w_bf16)


def _dot3(a, b):
    ah, al = _split_bf16(a)
    bh, bl = _split_bf16(b)
    return _dot(ah, bh) + (_dot(ah, bl) + _dot(al, bh))


def _head_ones(n):
    shift = HEAD_DIM.bit_length() - 1
    r = lax.shift_right_logical(lax.broadcasted_iota(jnp.int32, (n, n), 0), shift)
    c = lax.shift_right_logical(lax.broadcasted_iota(jnp.int32, (n, n), 1), shift)
    return jnp.where(r == c, 1.0, 0.0).astype(BF16)


def _sigmoid(x):
    return 1.0 / (1.0 + jnp.exp(-x))


def _softplus(x):
    return jnp.maximum(x, 0.0) + jnp.log(1.0 + jnp.exp(-jnp.abs(x)))


def _rms(x, g):
    return x * lax.rsqrt(jnp.mean(x * x, axis=-1, keepdims=True) + RMS_EPS) * g


def _rope(x, cos, sgn_sin):
    n = x.shape[-1]
    lane = lax.broadcasted_iota(jnp.int32, x.shape, 1)
    partner = jnp.where((lane & 16) == 0, pltpu.roll(x, n - 16, 1), pltpu.roll(x, 16, 1))
    return x * cos + partner * sgn_sin


def _mod_kernel(c_ref, w_ref, b_ref, o_ref):
    c = c_ref[...]
    o_ref[...] = _dot3(c * _sigmoid(c), w_ref[...]) + b_ref[...]


def _mod_call(cvec, w_mod, b_mod):
    n = w_mod.shape[1]
    rows = cvec.shape[0]
    return pl.pallas_call(
        _mod_kernel,
        grid=(n // MOD_COL_TILE,),
        in_specs=[
            pl.BlockSpec((rows, D_MODEL), lambda j: (0, 0)),
            pl.BlockSpec((D_MODEL, MOD_COL_TILE), lambda j: (0, j)),
            pl.BlockSpec((1, MOD_COL_TILE), lambda j: (0, j)),
        ],
        out_specs=pl.BlockSpec((rows, MOD_COL_TILE), lambda j: (0, j)),
        out_shape=jax.ShapeDtypeStruct((rows, n), F32),
        compiler_params=pltpu.CompilerParams(dimension_semantics=("arbitrary",)),
        name="adaln_mod",
    )(cvec, w_mod, b_mod.reshape(1, n))


_V_KK, _V_KA, _V_RK, _V_QN = range(4)


def _pre_kernel(*refs, rope):
    if rope:
        (x_ref, mod_ref, gpre_ref, win_ref, wlora_ref, gup_ref, vec_ref, wa0_ref, kn_ref,
         cq_ref, sq_ref, ck_ref, sk_ref,
         r_o, v_o, kk_o, k_o, w_o, a_o, g_o, bonus_o, q_o, ka_o, va_o) = refs
    else:
        (x_ref, mod_ref, gpre_ref, win_ref, wlora_ref, gup_ref, vec_ref, wa0_ref, kn_ref,
         r_o, v_o, kk_o, k_o, w_o, a_o, g_o, bonus_o, q_o, ka_o, va_o) = refs

    x = x_ref[...]
    m = mod_ref[0]
    shift1 = m[:, 0:D_MODEL]
    scale1 = m[:, D_MODEL:2 * D_MODEL]
    h = _rms(x, gpre_ref[...]) * (1.0 + scale1) + shift1
    p = _dot(h.astype(BF16), win_ref[...])

    r = p[:, 0:512]
    k = p[:, 512:1024]
    v = p[:, 1024:1536]
    xwa = p[:, 1536:1664]
    xg = p[:, 1664:1792]
    q = p[:, 1792:2304]
    ka = p[:, 2304:2432]
    va = p[:, 2432:2560]

    vec = vec_ref[...]
    ones512 = _head_ones(D_RWKV)

    g_o[...] = _dot(_sigmoid(xg).astype(BF16), gup_ref[...])

    kx = k * vec[_V_KK:_V_KK + 1]
    kk_o[...] = kx * lax.rsqrt(_dot_exact_rhs(kx * kx, ones512) + 1e-12)

    lane = lax.broadcasted_iota(jnp.int32, xwa.shape, 1)
    lora_in = jnp.where(lane < LORA_W, jnp.tanh(xwa), xwa)
    lora = _dot3(lora_in, wlora_ref[...])

    z = wa0_ref[0:1] + lora[:, 0:2 * D_RWKV]
    logw = -_softplus(-z) - 0.5
    w_o[...] = jnp.exp(-jnp.exp(logw))
    a = _sigmoid(wa0_ref[1:2] + lora[:, 2 * D_RWKV:4 * D_RWKV])
    a_o[...] = a

    k_a = vec[_V_KA:_V_KA + 1]
    kd_sum = k * (1.0 + (a[:, 0:D_RWKV] - 1.0) * k_a) + k * (1.0 + (a[:, D_RWKV:2 * D_RWKV] - 1.0) * k_a)
    bonus_o[...] = _dot_exact_rhs(r * kd_sum * vec[_V_RK:_V_RK + 1], ones512) * v

    r_o[...] = r
    v_o[...] = v
    k_o[...] = k

    qn = q * lax.rsqrt(_dot_exact_rhs(q * q, ones512) * (1.0 / HEAD_DIM) + RMS_EPS) * vec[_V_QN:_V_QN + 1]
    kan = ka * lax.rsqrt(_dot_exact_rhs(ka * ka, _head_ones(D_KV)) * (1.0 / HEAD_DIM) + RMS_EPS) * kn_ref[...]
    if rope:
        qn = _rope(qn, cq_ref[...], sq_ref[...])
        kan = _rope(kan, ck_ref[...], sk_ref[...])
    q_o[...] = qn
    ka_o[...] = kan
    va_o[...] = va


def _pre_call(x2d, mod3d, seq_len, wts, rope_tabs):
    n_tok = x2d.shape[0]
    tiles_per_seq = seq_len // TOK_TILE
    n_mod = mod3d.shape[0]
    rope = rope_tabs is not None

    def const(shape):
        return pl.BlockSpec(shape, lambda i: tuple(0 for _ in shape))

    mod_map = (lambda i: (i // tiles_per_seq, 0, 0)) if n_mod > 1 else (lambda i: (0, 0, 0))
    in_specs = [
        pl.BlockSpec((TOK_TILE, D_MODEL), lambda i: (i, 0)),
        pl.BlockSpec((1, 1, 6 * D_MODEL), mod_map),
        const((1, D_MODEL)),
        const((D_MODEL, D_IN)),
        const((LORA_W + LORA_A, 4 * D_RWKV)),
        const((LORA_G, D_RWKV)),
        const((4, D_RWKV)),
        const((2, 2 * D_RWKV)),
        const((1, D_KV)),
    ]
    args = [x2d, mod3d, wts["g_pre"], wts["w_in"], wts["w_lora"], wts["g_up"], wts["vec"], wts["wa0"],
            wts["k_norm"]]
    if rope:
        tab_map = lambda i: (i % tiles_per_seq, 0)
        in_specs += [pl.BlockSpec((TOK_TILE, D_ATTN), tab_map), pl.BlockSpec((TOK_TILE, D_ATTN), tab_map),
                     pl.BlockSpec((TOK_TILE, D_KV), tab_map), pl.BlockSpec((TOK_TILE, D_KV), tab_map)]
        args += list(rope_tabs)

    widths = [512] * 4 + [1024, 1024] + [512] * 3 + [D_KV, D_KV]
    out_specs = [pl.BlockSpec((TOK_TILE, wd), lambda i: (i, 0)) for wd in widths]
    out_shape = [jax.ShapeDtypeStruct((n_tok, wd), F32) for wd in widths]
    return pl.pallas_call(
        functools.partial(_pre_kernel, rope=rope),
        grid=(n_tok // TOK_TILE,),
        in_specs=in_specs,
        out_specs=out_specs,
        out_shape=out_shape,
        compiler_params=pltpu.CompilerParams(dimension_semantics=("arbitrary",),
                                             vmem_limit_bytes=VMEM_LIMIT),
        name="pre_rope" if rope else "pre",
    )(*args)


def _to_chains_kernel(x_ref, o_ref, slab_ref, *, n_batch, dup):
    n_col = x_ref.shape[2] // HEAD_DIM
    n_grp = o_ref.shape[0]

    def stage1(b, carry):
        for jc in range(n_col // 2):
            a_t = x_ref[b, :, LANES * jc:LANES * (jc + 1)].T
            for h2 in range(2):
                row0 = pl.multiple_of(((2 * jc + h2) * n_batch + b) * CHAIN_PITCH, SUBLANES)
                slab_ref[pl.ds(row0, HEAD_DIM), :] = a_t[HEAD_DIM * h2:HEAD_DIM * (h2 + 1), :]
        return carry

    lax.fori_loop(0, n_batch, stage1, 0, unroll=RELAYOUT_BATCH_UNROLL)

    def stage2(kb, carry):
        for ki in range(SUBLANES):
            k = kb * SUBLANES + ki
            for d in range(n_grp):
                if dup:
                    half = slab_ref[pl.ds(k, LANES // 2, stride=CHAIN_PITCH), :]
                    tile = jnp.concatenate([half, half], axis=0)
                else:
                    tile = slab_ref[pl.ds(d * LANES * CHAIN_PITCH + k, LANES, stride=CHAIN_PITCH), :]
                o_ref.at[d][pl.ds(k, RELAYOUT_TOK, stride=CHAIN_PITCH), :] = tile.T
        return carry

    lax.fori_loop(0, HEAD_DIM // SUBLANES, stage2, 0)

    zero = jnp.zeros((RELAYOUT_TOK, LANES), F32)
    for d in range(n_grp):
        for pad in range(HEAD_DIM, CHAIN_PITCH):
            o_ref.at[d][pl.ds(pad, RELAYOUT_TOK, stride=CHAIN_PITCH), :] = zero


def _to_chains_call(x2d, n_batch, seq_len, dup):
    n_chan = x2d.shape[1]
    n_slab = (n_chan // HEAD_DIM) * n_batch
    n_grp = n_slab * (2 if dup else 1) // LANES
    out = pl.pallas_call(
        functools.partial(_to_chains_kernel, n_batch=n_batch, dup=dup),
        grid=(seq_len // RELAYOUT_TOK,),
        in_specs=[pl.BlockSpec((n_batch, RELAYOUT_TOK, n_chan), lambda i: (0, i, 0))],
        out_specs=pl.BlockSpec((n_grp, RELAYOUT_TOK * CHAIN_PITCH, LANES), lambda i: (0, i, 0)),
        out_shape=jax.ShapeDtypeStruct((n_grp, seq_len * CHAIN_PITCH, LANES), F32),
        scratch_shapes=[pltpu.VMEM((n_slab * CHAIN_PITCH, LANES), F32)],
        compiler_params=pltpu.CompilerParams(dimension_semantics=("arbitrary",),
                                             vmem_limit_bytes=VMEM_LIMIT),
        name="to_chains_dup" if dup else "to_chains",
    )(x2d.reshape(n_batch, seq_len, n_chan))
    return out.reshape(n_grp, seq_len, CHAIN_PITCH, LANES)


def _from_chains_kernel(yf_ref, yb_ref, o_ref, slab_ref, *, n_batch, packed):
    def stage1(vb, carry):
        for vi in range(SUBLANES):
            v = vb * SUBLANES + vi
            tf = yf_ref[pl.ds(v, RELAYOUT_TOK, stride=CHAIN_PITCH), :]
            tb = yb_ref[pl.ds(v, RELAYOUT_TOK, stride=CHAIN_PITCH), :]
            if packed:
                tb = pltpu.roll(tb, LANES // 2, 1)
            slab_ref[pl.ds(v, LANES, stride=CHAIN_PITCH), :] = (tf + tb).T
        return carry

    lax.fori_loop(0, HEAD_DIM // SUBLANES, stage1, 0)

    def stage2(b, carry):
        for jc in range(H_RWKV // 2):
            rows = []
            for h2 in range(2):
                row0 = pl.multiple_of(((2 * jc + h2) * n_batch + b) * CHAIN_PITCH, SUBLANES)
                rows.append(slab_ref[pl.ds(row0, HEAD_DIM), :])
            o_ref[b, :, LANES * jc:LANES * (jc + 1)] = jnp.concatenate(rows, axis=0).T
        return carry

    lax.fori_loop(0, n_batch, stage2, 0, unroll=RELAYOUT_BATCH_UNROLL)


def _from_chains_call(y_f, y_b, n_batch, seq_len, packed):
    n_rows = seq_len * CHAIN_PITCH
    grp_b = 0 if packed else 1
    blk = (None, RELAYOUT_TOK * CHAIN_PITCH, LANES)
    out = pl.pallas_call(
        functools.partial(_from_chains_kernel, n_batch=n_batch, packed=packed),
        grid=(seq_len // RELAYOUT_TOK,),
        in_specs=[pl.BlockSpec(blk, lambda i: (0, i, 0)), pl.BlockSpec(blk, lambda i: (grp_b, i, 0))],
        out_specs=pl.BlockSpec((n_batch, RELAYOUT_TOK, D_RWKV), lambda i: (0, i, 0)),
        out_shape=jax.ShapeDtypeStruct((n_batch, seq_len, D_RWKV), F32),
        scratch_shapes=[pltpu.VMEM((LANES * CHAIN_PITCH, LANES), F32)],
        compiler_params=pltpu.CompilerParams(dimension_semantics=("arbitrary",),
                                             vmem_limit_bytes=VMEM_LIMIT),
        name="from_chains_packed" if packed else "from_chains",
    )(y_f.reshape(-1, n_rows, LANES), y_b.reshape(-1, n_rows, LANES))
    return out.reshape(n_batch * seq_len, D_RWKV)


def _scan_kernel(*refs, packed, has_init):
    (rF, rB, kkF, kkB, kF, kB, vF, vB, wF, wB, aF, aB, ka_ref), rest = refs[:13], refs[13:]
    if has_init:
        s0_ref, rest = rest[0], rest[1:]
    yF, yB, sfin_ref, s_scr, sa_scr, op_scr, p_scr = rest
    g = pl.program_id(0)
    j = pl.program_id(1)
    tile = (HEAD_DIM, LANES)

    if packed:
        use_b = lax.broadcasted_iota(jnp.int32, tile, 1) >= LANES // 2
    else:
        use_b = jnp.full(tile, g, jnp.int32) == 1

    def pick(f_ref, b_ref, s):
        return jnp.where(use_b, b_ref[SCAN_STEPS - 1 - s, 0:HEAD_DIM, :], f_ref[s, 0:HEAD_DIM, :])

    @pl.when(j == 0)
    def _():
        if has_init:
            s_scr[...] = s0_ref[...]
        else:
            s_scr[...] = jnp.zeros(s_scr.shape, F32)

    op_scr[3] = pick(kkF, kkB, 0)
    acc = jnp.zeros(tile, F32)
    for k in range(HEAD_DIM):
        acc = acc + s_scr[k] * op_scr[3, k:k + 1, :]
    sa_scr[...] = acc

    ka = ka_ref[...]
    pad = jnp.zeros((SCAN_STEPS, CHAIN_PITCH - HEAD_DIM, LANES), F32)
    yF[:, HEAD_DIM:CHAIN_PITCH, :] = pad
    yB[:, HEAD_DIM:CHAIN_PITCH, :] = pad

    p_scr[...] = jnp.ones(tile, F32)

    def step(s, carry):
        a = pick(aF, aB, s)
        p = p_scr[...] * pick(wF, wB, s)
        p_scr[...] = p
        inv_p = 1.0 / p
        op_scr[0] = pick(kkF, kkB, s) * a * inv_p
        op_scr[1] = pick(kF, kB, s) * (1.0 + (a - 1.0) * ka) * inv_p
        op_scr[2] = pick(rF, rB, s) * p
        op_scr[3] = pick(kkF, kkB, jnp.minimum(s + 1, SCAN_STEPS - 1)) * p
        vt = pick(vF, vB, s)
        sa = sa_scr[...]

        def kblock(kb, acc):
            y, san = acc
            for ki in range(SCAN_K_UNROLL):
                k = kb * SCAN_K_UNROLL + ki
                bk = op_scr[0, pl.ds(k, 1), :]
                kdk = op_scr[1, pl.ds(k, 1), :]
                rk = op_scr[2, pl.ds(k, 1), :]
                kn = op_scr[3, pl.ds(k, 1), :]
                sn = s_scr[k] - sa * bk + vt * kdk
                s_scr[k] = sn
                y = y + sn * rk
                san = san + sn * kn
            return y, san

        zero = jnp.zeros(tile, F32)
        y, san = lax.fori_loop(0, HEAD_DIM // SCAN_K_UNROLL, kblock, (zero, zero))
        yF[s, 0:HEAD_DIM, :] = y
        yB[SCAN_STEPS - 1 - s, 0:HEAD_DIM, :] = y
        sa_scr[...] = san
        return carry

    lax.fori_loop(0, SCAN_STEPS, step, 0)

    for k in range(HEAD_DIM):
        s_scr[k] = s_scr[k] * p_scr[k:k + 1, :]

    @pl.when(j == pl.num_programs(1) - 1)
    def _():
        sfin_ref[...] = s_scr[...]


def _scan_call(shared, w, a, ka_tab, packed, s0=None):
    n_step = w.shape[1]
    n_grp = 1 if packed else 2
    n_blk = n_step // SCAN_STEPS
    has_init = s0 is not None
    blk = (None, SCAN_STEPS, CHAIN_PITCH, LANES)
    if packed:
        f_map = lambda g, j: (0, j, 0, 0)
        b_map = lambda g, j: (0, n_blk - 1 - j, 0, 0)
        fd_map, bd_map = f_map, b_map
    else:
        f_map = lambda g, j: (0, jnp.where(g == 0, j, 0), 0, 0)
        b_map = lambda g, j: (0, jnp.where(g == 1, n_blk - 1 - j, 0), 0, 0)
        fd_map = lambda g, j: (0, jnp.where(g == 0, j, 0), 0, 0)
        bd_map = lambda g, j: (1, jnp.where(g == 1, n_blk - 1 - j, 0), 0, 0)
    in_specs, args = [], []
    for x in shared:
        in_specs += [pl.BlockSpec(blk, f_map), pl.BlockSpec(blk, b_map)]
        args += [x, x]
    for x in (w, a):
        in_specs += [pl.BlockSpec(blk, fd_map), pl.BlockSpec(blk, bd_map)]
        args += [x, x]
    in_specs.append(pl.BlockSpec((HEAD_DIM, LANES), lambda g, j: (0, 0)))
    args.append(ka_tab)
    state_spec = pl.BlockSpec((None, HEAD_DIM, HEAD_DIM, LANES), lambda g, j: (g, 0, 0, 0))
    if has_init:
        in_specs.append(state_spec)
        args.append(s0)
    y_shape = jax.ShapeDtypeStruct((n_grp, n_step, CHAIN_PITCH, LANES), F32)
    return pl.pallas_call(
        functools.partial(_scan_kernel, packed=packed, has_init=has_init),
        grid=(n_grp, n_blk),
        in_specs=in_specs,
        out_specs=[pl.BlockSpec(blk, lambda g, j: (g, j, 0, 0)),
                   pl.BlockSpec(blk, lambda g, j: (g, n_blk - 1 - j, 0, 0)),
                   state_spec],
        out_shape=[y_shape, y_shape, jax.ShapeDtypeStruct((n_grp, HEAD_DIM, HEAD_DIM, LANES), F32)],
        scratch_shapes=[pltpu.VMEM((HEAD_DIM, HEAD_DIM, LANES), F32), pltpu.VMEM((HEAD_DIM, LANES), F32),
                        pltpu.VMEM((4, HEAD_DIM, LANES), F32), pltpu.VMEM((HEAD_DIM, LANES), F32)],
        compiler_params=pltpu.CompilerParams(dimension_semantics=("arbitrary", "arbitrary"),
                                             vmem_limit_bytes=VMEM_LIMIT),
        name="wkv_scan_packed" if packed else "wkv_scan",
    )(*args)


def _attn_kernel(q_ref, k_ref, v_ref, o_ref):
    q = q_ref[...]
    k = k_ref[...]
    v = v_ref[...]
    group = H_Q // H_KV
    for kvh in range(H_KV):
        kh = k[:, HEAD_DIM * kvh:HEAD_DIM * (kvh + 1)].astype(BF16)
        vh = v[:, HEAD_DIM * kvh:HEAD_DIM * (kvh + 1)].astype(BF16)
        for gq in range(group):
            hq = kvh * group + gq
            qh = (q[:, HEAD_DIM * hq:HEAD_DIM * (hq + 1)] * (HEAD_DIM ** -0.5)).astype(BF16)
            s = lax.dot_general(qh, kh, (((1,), (1,)), ((), ())), preferred_element_type=F32)
            e = jnp.exp(s - jnp.max(s, axis=-1, keepdims=True))
            o = _dot(e.astype(BF16), vh) / jnp.sum(e, axis=-1, keepdims=True)
            o_ref[:, HEAD_DIM * hq:HEAD_DIM * (hq + 1)] = o


def _attn_call(q2d, k2d, v2d, n_batch, lq, lk):
    q_tile = min(ATTN_Q_TILE, lq)
    q_tiles = lq // q_tile
    return pl.pallas_call(
        _attn_kernel,
        grid=(n_batch, q_tiles),
        in_specs=[
            pl.BlockSpec((q_tile, D_ATTN), lambda b, i: (b * q_tiles + i, 0)),
            pl.BlockSpec((lk, D_KV), lambda b, i: (b, 0)),
            pl.BlockSpec((lk, D_KV), lambda b, i: (b, 0)),
        ],
        out_specs=pl.BlockSpec((q_tile, D_ATTN), lambda b, i: (b * q_tiles + i, 0)),
        out_shape=jax.ShapeDtypeStruct((n_batch * lq, D_ATTN), F32),
        compiler_params=pltpu.CompilerParams(dimension_semantics=("arbitrary", "arbitrary"),
                                             vmem_limit_bytes=VMEM_LIMIT),
        name="gqa_attn",
    )(q2d, k2d, v2d)


def _post_kernel(x_ref, mod_ref, ys_ref, bonus_ref, g_ref, attn_ref,
                 gn_ref, wout_ref, gpost_ref, gffn_ref, x1_o, h2_o):
    ones512 = _head_ones(D_RWKV)
    ys = ys_ref[...]
    ys_hi, ys_lo = _split_bf16(ys)
    mu = (_dot(ys_hi, ones512) + _dot(ys_lo, ones512)) * (1.0 / HEAD_DIM)
    dlt = ys - mu
    var = _dot_exact_rhs(dlt * dlt, ones512) * (1.0 / HEAD_DIM)
    gn = gn_ref[...]
    yn = dlt * lax.rsqrt(var + GN_EPS) * gn[0:1] + gn[1:2]
    yr = (yn + bonus_ref[...]) * g_ref[...]
    mix = (_dot(yr.astype(BF16), wout_ref[0:D_RWKV, :])
           + _dot(attn_ref[...].astype(BF16), wout_ref[D_RWKV:D_MODEL, :]))
    m = mod_ref[0]
    gate1 = m[:, 2 * D_MODEL:3 * D_MODEL]
    shift2 = m[:, 3 * D_MODEL:4 * D_MODEL]
    scale2 = m[:, 4 * D_MODEL:5 * D_MODEL]
    x1 = x_ref[...] + gate1 * _rms(mix, gpost_ref[...])
    x1_o[...] = x1
    h2_o[...] = (_rms(x1, gffn_ref[...]) * (1.0 + scale2) + shift2).astype(BF16)


def _post_call(x2d, mod3d, seq_len, ys, bonus, g, attn, wts):
    n_tok = x2d.shape[0]
    tiles_per_seq = seq_len // TOK_TILE
    n_mod = mod3d.shape[0]

    def const(shape):
        return pl.BlockSpec(shape, lambda i: tuple(0 for _ in shape))

    mod_map = (lambda i: (i // tiles_per_seq, 0, 0)) if n_mod > 1 else (lambda i: (0, 0, 0))
    tok = lambda wd: pl.BlockSpec((TOK_TILE, wd), lambda i: (i, 0))
    return pl.pallas_call(
        _post_kernel,
        grid=(n_tok // TOK_TILE,),
        in_specs=[tok(D_MODEL), pl.BlockSpec((1, 1, 6 * D_MODEL), mod_map),
                  tok(512), tok(512), tok(512), tok(512),
                  const((2, D_RWKV)), const((D_MODEL, D_MODEL)), const((1, D_MODEL)), const((1, D_MODEL))],
        out_specs=[tok(D_MODEL), tok(D_MODEL)],
        out_shape=[jax.ShapeDtypeStruct((n_tok, D_MODEL), F32), jax.ShapeDtypeStruct((n_tok, D_MODEL), BF16)],
        compiler_params=pltpu.CompilerParams(dimension_semantics=("arbitrary",),
                                             vmem_limit_bytes=VMEM_LIMIT),
        name="mix_post",
    )(x2d, mod3d, ys, bonus, g, attn, wts["gn"], wts["w_out"], wts["g_post"], wts["g_ffn_pre"])


def _ffn_kernel(h2_ref, x1_ref, mod_ref, wg_ref, wv_ref, cwg_ref, cwv_ref, cbg_ref, cbv_ref,
                wd_ref, gpost_ref, o_ref, acc_ref, *, seq_len):
    j = pl.program_id(1)

    @pl.when(j == 0)
    def _():
        acc_ref[...] = jnp.zeros(acc_ref.shape, F32)

    h2 = h2_ref[...]
    n_rows = h2.shape[0]
    assert seq_len & (seq_len - 1) == 0
    row = lax.broadcasted_iota(jnp.int32, (n_rows, FFN_SUB_TILE), 0) & (seq_len - 1)
    has_prev = row != 0
    has_next = row != seq_len - 1

    def conv(u, cw, cb):
        prev = jnp.where(has_prev, pltpu.roll(u, 1, 0), 0.0)
        nxt = jnp.where(has_next, pltpu.roll(u, n_rows - 1, 0), 0.0)
        return ((cb + prev * cw[0:1]) + u * cw[1:2]) + nxt * cw[2:3]

    down = None
    for c0 in range(0, FFN_COL_TILE, FFN_SUB_TILE):
        cols = slice(c0, c0 + FFN_SUB_TILE)
        gate = conv(_dot(h2, wg_ref[:, cols]), cwg_ref[:, cols], cbg_ref[:, cols])
        val = conv(_dot(h2, wv_ref[:, cols]), cwv_ref[:, cols], cbv_ref[:, cols])
        act = (gate * _sigmoid(gate) * val).astype(BF16)
        part = _dot(act, wd_ref[cols, :])
        down = part if down is None else down + part
    acc_ref[...] += down

    @pl.when(j == pl.num_programs(1) - 1)
    def _():
        m = mod_ref[0]
        gate2 = m[:, 5 * D_MODEL:6 * D_MODEL]
        o_ref[...] = x1_ref[...] + gate2 * _rms(acc_ref[...], gpost_ref[...])


def _ffn_call(h2, x1, mod3d, seq_len, wts):
    n_tok = h2.shape[0]
    tiles_per_seq = max(seq_len // FFN_TOK_TILE, 1)
    n_mod = mod3d.shape[0]
    n_col = D_FF_PAD // FFN_COL_TILE
    mod_map = (lambda i, j: (i // tiles_per_seq, 0, 0)) if n_mod > 1 else (lambda i, j: (0, 0, 0))
    tok = lambda: pl.BlockSpec((FFN_TOK_TILE, D_MODEL), lambda i, j: (i, 0))
    return pl.pallas_call(
        functools.partial(_ffn_kernel, seq_len=seq_len),
        grid=(n_tok // FFN_TOK_TILE, n_col),
        in_specs=[
            tok(), tok(), pl.BlockSpec((1, 1, 6 * D_MODEL), mod_map),
            pl.BlockSpec((D_MODEL, FFN_COL_TILE), lambda i, j: (0, j)),
            pl.BlockSpec((D_MODEL, FFN_COL_TILE), lambda i, j: (0, n_col + j)),
            pl.BlockSpec((3, FFN_COL_TILE), lambda i, j: (0, j)),
            pl.BlockSpec((3, FFN_COL_TILE), lambda i, j: (0, n_col + j)),
            pl.BlockSpec((1, FFN_COL_TILE), lambda i, j: (0, j)),
            pl.BlockSpec((1, FFN_COL_TILE), lambda i, j: (0, n_col + j)),
            pl.BlockSpec((FFN_COL_TILE, D_MODEL), lambda i, j: (j, 0)),
            pl.BlockSpec((1, D_MODEL), lambda i, j: (0, 0)),
        ],
        out_specs=tok(),
        out_shape=jax.ShapeDtypeStruct((n_tok, D_MODEL), F32),
        scratch_shapes=[pltpu.VMEM((FFN_TOK_TILE, D_MODEL), F32)],
        compiler_params=pltpu.CompilerParams(dimension_semantics=("arbitrary", "arbitrary"),
                                             vmem_limit_bytes=VMEM_LIMIT),
        name="conv_ffn",
    )(h2, x1, mod3d, wts["ffn_up"], wts["ffn_up"], wts["conv_w"], wts["conv_w"],
      wts["conv_b"], wts["conv_b"], wts["ffn_down"], wts["g_ffn_post"])


def _rope_tables(n_tok, n_heads):
    quarter = HEAD_DIM // 4
    pos = jnp.arange(n_tok)
    row = (pos // GRID_W).astype(F32)
    col = (pos % GRID_W).astype(F32)
    inv_freq = ROPE_THETA ** (-jnp.arange(quarter, dtype=F32) / quarter)

    def half(p):
        ang = p[:, None] * inv_freq[None, :]
        c, s = jnp.cos(ang), jnp.sin(ang)
        return jnp.concatenate([c, c], axis=-1), jnp.concatenate([-s, s], axis=-1)

    cr, sr = half(row)
    cc, sc = half(col)
    cos = jnp.concatenate([cr, cc], axis=-1)
    sin = jnp.concatenate([sr, sc], axis=-1)
    return jnp.tile(cos, (1, n_heads)), jnp.tile(sin, (1, n_heads))


def _pad_ff_cols(x):
    pad = ((0, 0), (0, D_FF_PAD - D_FF))
    return jnp.concatenate([jnp.pad(x[:, :D_FF], pad), jnp.pad(x[:, D_FF:], pad)], axis=1)


def _mixer(pre, n_batch, seq_len, packed, k_a, s0=None):
    r, v, kk, k, w, a = pre[:6]
    shared = tuple(_to_chains_call(x, n_batch, seq_len, packed) for x in (r, kk, k, v))
    ka_tab = jnp.repeat(k_a.reshape(H_RWKV, HEAD_DIM).T, n_batch, axis=1)
    ka_tab = jnp.tile(ka_tab, (1, LANES // ka_tab.shape[1]))
    y_f, y_b, s_fin = _scan_call(shared, _to_chains_call(w, n_batch, seq_len, False),
                                 _to_chains_call(a, n_batch, seq_len, False), ka_tab, packed, s0)
    return _from_chains_call(y_f, y_b, n_batch, seq_len, packed), s_fin


def kernel(x_prompt, x_sample, cache_k, cache_v, state_rwkv, c, c_ctx, w_mod, b_mod, norm_mix_pre, norm_mix_post, norm_ffn_pre, norm_ffn_post, w_in, w0, w_up, a0, a_up, g_up, k_k, k_a, r_k, gn_w, gn_b, q_norm, k_norm, w_out, ffn_up, conv_w, conv_b, ffn_down):
    n_ctx, l_ctx = x_prompt.shape[0], x_prompt.shape[1]
    n_lat, l_lat = x_sample.shape[0], x_sample.shape[1]
    l_past = cache_k.shape[2]
    layer = 0
    assert n_ctx * H_RWKV == LANES and 2 * n_lat * H_RWKV == LANES

    zeros_w = jnp.zeros((LORA_W, D_RWKV), F32)
    w_lora = jnp.concatenate([
        jnp.concatenate([w_up[layer, 0], w_up[layer, 1], zeros_w, zeros_w], axis=1),
        jnp.concatenate([zeros_w, zeros_w, a_up[layer, 0], a_up[layer, 1]], axis=1)], axis=0)
    vec = jnp.stack([k_k[layer], k_a[layer], r_k[layer].reshape(D_RWKV), jnp.tile(q_norm[layer], H_Q)])
    wts = dict(
        g_pre=norm_mix_pre[layer].reshape(1, D_MODEL),
        g_post=norm_mix_post[layer].reshape(1, D_MODEL),
        g_ffn_pre=norm_ffn_pre[layer].reshape(1, D_MODEL),
        g_ffn_post=norm_ffn_post[layer].reshape(1, D_MODEL),
        w_in=w_in[layer].astype(BF16),
        w_lora=w_lora,
        g_up=g_up[layer].astype(BF16),
        vec=vec,
        wa0=jnp.stack([w0[layer].reshape(2 * D_RWKV), a0[layer].reshape(2 * D_RWKV)]),
        k_norm=jnp.tile(k_norm[layer], H_KV).reshape(1, D_KV),
        gn=jnp.stack([gn_w[layer], gn_b[layer]]),
        w_out=w_out[layer].astype(BF16),
        ffn_up=_pad_ff_cols(ffn_up[layer].astype(BF16)),
        conv_w=_pad_ff_cols(conv_w[layer]),
        conv_b=_pad_ff_cols(conv_b[layer].reshape(1, 2 * D_FF)),
        ffn_down=jnp.pad(ffn_down[layer].astype(BF16), ((0, D_FF_PAD - D_FF), (0, 0))),
    )

    cvec = jnp.concatenate([c, c_ctx[None, :], jnp.zeros((16 - n_lat - 1, D_MODEL), F32)], axis=0)
    mod = _mod_call(cvec, w_mod[layer], b_mod[layer])
    mod_lat = mod[:n_lat].reshape(n_lat, 1, 6 * D_MODEL)
    mod_ctx = mod[n_lat:n_lat + 1].reshape(1, 1, 6 * D_MODEL)

    xc = x_prompt.reshape(n_ctx * l_ctx, D_MODEL)
    pre_c = _pre_call(xc, mod_ctx, l_ctx, wts, None)
    g_c, bonus_c, q_c, ka_c, va_c = pre_c[6:]
    ys_c, s_c = _mixer(pre_c, n_ctx, l_ctx, False, k_a[layer])
    attn_c = _attn_call(q_c, ka_c, va_c, n_ctx, l_ctx, l_ctx)
    x1_c, h2_c = _post_call(xc, mod_ctx, l_ctx, ys_c, bonus_c, g_c, attn_c, wts)
    out_c = _ffn_call(h2_c, x1_c, mod_ctx, l_ctx, wts)

    xl = x_sample.reshape(n_lat * l_lat, D_MODEL)
    tabs = _rope_tables(l_lat, H_Q) + _rope_tables(l_lat, H_KV)
    pre_l = _pre_call(xl, mod_lat, l_lat, wts, tabs)
    g_l, bonus_l, q_l, kr_l, va_l = pre_l[6:]
    s0 = state_rwkv[:, layer].transpose(4, 3, 1, 2, 0).reshape(1, HEAD_DIM, HEAD_DIM, LANES)
    ys_l, _ = _mixer(pre_l, n_lat, l_lat, True, k_a[layer], s0)
    k_all = jnp.concatenate([kr_l.reshape(n_lat, l_lat, D_KV), cache_k[:, layer].reshape(n_lat, l_past, D_KV)],
                            axis=1).reshape(n_lat * (l_lat + l_past), D_KV)
    v_all = jnp.concatenate([va_l.reshape(n_lat, l_lat, D_KV), cache_v[:, layer].reshape(n_lat, l_past, D_KV)],
                            axis=1).reshape(n_lat * (l_lat + l_past), D_KV)
    attn_l = _attn_call(q_l, k_all, v_all, n_lat, l_lat, l_lat + l_past)
    x1_l, h2_l = _post_call(xl, mod_lat, l_lat, ys_l, bonus_l, g_l, attn_l, wts)
    out_l = _ffn_call(h2_l, x1_l, mod_lat, l_lat, wts)

    y_prompt = out_c.reshape(n_ctx, l_ctx, D_MODEL)
    y_sample = out_l.reshape(n_lat, l_lat, D_MODEL)
    new_cache_k = ka_c.reshape(n_ctx, 1, l_ctx, H_KV, HEAD_DIM)
    new_cache_v = va_c.reshape(n_ctx, 1, l_ctx, H_KV, HEAD_DIM)
    new_state = s_c.reshape(2, HEAD_DIM, HEAD_DIM, H_RWKV, n_ctx).transpose(4, 0, 3, 2, 1)[:, None]
    return (y_prompt, y_sample, new_cache_k, new_cache_v, new_state)
```

```python
import functools

import jax
import jax.numpy as jnp
from jax import lax
from jax.experimental import pallas as pl
from jax.experimental.pallas import tpu as pltpu

F32 = jnp.float32
BF16 = jnp.bfloat16

D_MODEL = 1024
HEAD_DIM = 64
D_RWKV = 512
H_RWKV = 8
D_ATTN = 512
H_Q = 8
H_KV = 2
D_KV = 128
LORA_W = 64
LORA_A = 64
LORA_G = 128
D_FF = 2816
D_IN = 2560
GRID_W = 64
ROPE_THETA = 10000.0
RMS_EPS = 1e-6
GN_EPS = 64e-5

LANES = 128
SUBLANES = 8
TOK_TILE = 512
FFN_TOK_TILE = 1024
FFN_COL_TILE = 512
FFN_SUB_TILE = 256
D_FF_PAD = -(-D_FF // FFN_COL_TILE) * FFN_COL_TILE
SCAN_STEPS = 32
SCAN_K_UNROLL = 16
RELAYOUT_TOK = 128
RELAYOUT_BATCH_UNROLL = 4
CHAIN_PITCH = 72
ATTN_Q_TILE = 512
MOD_COL_TILE = 768
VMEM_LIMIT = 56 * 1024 * 1024


def _dot(a, b):
    return jnp.dot(a, b, preferred_element_type=F32)


def _split_bf16(x):
    hi = x.astype(BF16)
    lo = (x - hi.astype(F32)).astype(BF16)
    return hi, lo


def _dot_exact_rhs(x, w_bf16):
    return _dot(x.astype(BF16), w_bf16)


def _dot3(a, b):
    ah, al = _split_bf16(a)
    bh, bl = _split_bf16(b)
    return _dot(ah, bh) + (_dot(ah, bl) + _dot(al, bh))


def _head_ones(n):
    shift = HEAD_DIM.bit_length() - 1
    r = lax.shift_right_logical(lax.broadcasted_iota(jnp.int32, (n, n), 0), shift)
    c = lax.shift_right_logical(lax.broadcasted_iota(jnp.int32, (n, n), 1), shift)
    return jnp.where(r == c, 1.0, 0.0).astype(BF16)


def _sigmoid(x):
    return 1.0 / (1.0 + jnp.exp(-x))


def _softplus(x):
    return jnp.maximum(x, 0.0) + jnp.log(1.0 + jnp.exp(-jnp.abs(x)))


def _rms(x, g):
    return x * lax.rsqrt(jnp.mean(x * x, axis=-1, keepdims=True) + RMS_EPS) * g


def _rope(x, cos, sgn_sin):
    n = x.shape[-1]
    lane = lax.broadcasted_iota(jnp.int32, x.shape, 1)
    partner = jnp.where((lane & 16) == 0, pltpu.roll(x, n - 16, 1), pltpu.roll(x, 16, 1))
    return x * cos + partner * sgn_sin


def _mod_kernel(c_ref, w_ref, b_ref, o_ref):
    c = c_ref[...]
    o_ref[...] = _dot3(c * _sigmoid(c), w_ref[...]) + b_ref[...]


def _mod_call(cvec, w_mod, b_mod):
    n = w_mod.shape[1]
    rows = cvec.shape[0]
    return pl.pallas_call(
        _mod_kernel,
        grid=(n // MOD_COL_TILE,),
        in_specs=[
            pl.BlockSpec((rows, D_MODEL), lambda j: (0, 0)),
            pl.BlockSpec((D_MODEL, MOD_COL_TILE), lambda j: (0, j)),
            pl.BlockSpec((1, MOD_COL_TILE), lambda j: (0, j)),
        ],
        out_specs=pl.BlockSpec((rows, MOD_COL_TILE), lambda j: (0, j)),
        out_shape=jax.ShapeDtypeStruct((rows, n), F32),
        compiler_params=pltpu.CompilerParams(dimension_semantics=("arbitrary",)),
        name="adaln_mod",
    )(cvec, w_mod, b_mod.reshape(1, n))


_V_KK, _V_KA, _V_RK, _V_QN = range(4)


def _pre_kernel(*refs, rope):
    if rope:
        (x_ref, mod_ref, gpre_ref, win_ref, wlora_ref, gup_ref, vec_ref, wa0_ref, kn_ref,
         cq_ref, sq_ref, ck_ref, sk_ref,
         r_o, v_o, kk_o, k_o, w_o, a_o, g_o, bonus_o, q_o, ka_o, va_o) = refs
    else:
        (x_ref, mod_ref, gpre_ref, win_ref, wlora_ref, gup_ref, vec_ref, wa0_ref, kn_ref,
         r_o, v_o, kk_o, k_o, w_o, a_o, g_o, bonus_o, q_o, ka_o, va_o) = refs

    x = x_ref[...]
    m = mod_ref[0]
    shift1 = m[:, 0:D_MODEL]
    scale1 = m[:, D_MODEL:2 * D_MODEL]
    h = _rms(x, gpre_ref[...]) * (1.0 + scale1) + shift1
    p = _dot(h.astype(BF16), win_ref[...])

    r = p[:, 0:512]
    k = p[:, 512:1024]
    v = p[:, 1024:1536]
    xwa = p[:, 1536:1664]
    xg = p[:, 1664:1792]
    q = p[:, 1792:2304]
    ka = p[:, 2304:2432]
    va = p[:, 2432:2560]

    vec = vec_ref[...]
    ones512 = _head_ones(D_RWKV)

    g_o[...] = _dot(_sigmoid(xg).astype(BF16), gup_ref[...])

    kx = k * vec[_V_KK:_V_KK + 1]
    kk_o[...] = kx * lax.rsqrt(_dot_exact_rhs(kx * kx, ones512) + 1e-12)

    lane = lax.broadcasted_iota(jnp.int32, xwa.shape, 1)
    lora_in = jnp.where(lane < LORA_W, jnp.tanh(xwa), xwa)
    lora = _dot(lora_in.astype(BF16), wlora_ref[...])

    z = wa0_ref[0:1] + lora[:, 0:2 * D_RWKV]
    logw = -_softplus(-z) - 0.5
    w_o[...] = jnp.exp(-jnp.exp(logw))
    a = _sigmoid(wa0_ref[1:2] + lora[:, 2 * D_RWKV:4 * D_RWKV])
    a_o[...] = a

    k_a = vec[_V_KA:_V_KA + 1]
    kd_sum = k * (1.0 + (a[:, 0:D_RWKV] - 1.0) * k_a) + k * (1.0 + (a[:, D_RWKV:2 * D_RWKV] - 1.0) * k_a)
    bonus_o[...] = _dot_exact_rhs(r * kd_sum * vec[_V_RK:_V_RK + 1], ones512) * v

    r_o[...] = r
    v_o[...] = v
    k_o[...] = k

    qn = q * lax.rsqrt(_dot_exact_rhs(q * q, ones512) * (1.0 / HEAD_DIM) + RMS_EPS) * vec[_V_QN:_V_QN + 1]
    kan = ka * lax.rsqrt(_dot_exact_rhs(ka * ka, _head_ones(D_KV)) * (1.0 / HEAD_DIM) + RMS_EPS) * kn_ref[...]
    if rope:
        qn = _rope(qn, cq_ref[...], sq_ref[...])
        kan = _rope(kan, ck_ref[...], sk_ref[...])
    q_o[...] = qn
    ka_o[...] = kan
    va_o[...] = va


def _pre_call(x2d, mod3d, seq_len, wts, rope_tabs):
    n_tok = x2d.shape[0]
    tiles_per_seq = seq_len // TOK_TILE
    n_mod = mod3d.shape[0]
    rope = rope_tabs is not None

    def const(shape):
        return pl.BlockSpec(shape, lambda i: tuple(0 for _ in shape))

    mod_map = (lambda i: (i // tiles_per_seq, 0, 0)) if n_mod > 1 else (lambda i: (0, 0, 0))
    in_specs = [
        pl.BlockSpec((TOK_TILE, D_MODEL), lambda i: (i, 0)),
        pl.BlockSpec((1, 1, 6 * D_MODEL), mod_map),
        const((1, D_MODEL)),
        const((D_MODEL, D_IN)),
        const((LORA_W + LORA_A, 4 * D_RWKV)),
        const((LORA_G, D_RWKV)),
        const((4, D_RWKV)),
        const((2, 2 * D_RWKV)),
        const((1, D_KV)),
    ]
    args = [x2d, mod3d, wts["g_pre"], wts["w_in"], wts["w_lora"], wts["g_up"], wts["vec"], wts["wa0"],
            wts["k_norm"]]
    if rope:
        tab_map = lambda i: (i % tiles_per_seq, 0)
        in_specs += [pl.BlockSpec((TOK_TILE, D_ATTN), tab_map), pl.BlockSpec((TOK_TILE, D_ATTN), tab_map),
                     pl.BlockSpec((TOK_TILE, D_KV), tab_map), pl.BlockSpec((TOK_TILE, D_KV), tab_map)]
        args += list(rope_tabs)

    widths = [512] * 4 + [1024, 1024] + [512] * 3 + [D_KV, D_KV]
    out_specs = [pl.BlockSpec((TOK_TILE, wd), lambda i: (i, 0)) for wd in widths]
    out_shape = [jax.ShapeDtypeStruct((n_tok, wd), F32) for wd in widths]
    return pl.pallas_call(
        functools.partial(_pre_kernel, rope=rope),
        grid=(n_tok // TOK_TILE,),
        in_specs=in_specs,
        out_specs=out_specs,
        out_shape=out_shape,
        compiler_params=pltpu.CompilerParams(dimension_semantics=("arbitrary",),
                                             vmem_limit_bytes=VMEM_LIMIT),
        name="pre_rope" if rope else "pre",
    )(*args)


def _to_chains_kernel(x_ref, o_ref, slab_ref, *, n_batch, dup):
    n_col = x_ref.shape[2] // HEAD_DIM
    n_grp = o_ref.shape[0]

    def stage1(b, carry):
        for jc in range(n_col // 2):
            a_t = x_ref[b, :, LANES * jc:LANES * (jc + 1)].T
            for h2 in range(2):
                row0 = pl.multiple_of(((2 * jc + h2) * n_batch + b) * CHAIN_PITCH, SUBLANES)
                slab_ref[pl.ds(row0, HEAD_DIM), :] = a_t[HEAD_DIM * h2:HEAD_DIM * (h2 + 1), :]
        return carry

    lax.fori_loop(0, n_batch, stage1, 0, unroll=RELAYOUT_BATCH_UNROLL)

    def stage2(kb, carry):
        for ki in range(SUBLANES):
            k = kb * SUBLANES + ki
            for d in range(n_grp):
                if dup:
                    half = slab_ref[pl.ds(k, LANES // 2, stride=CHAIN_PITCH), :]
                    tile = jnp.concatenate([half, half], axis=0)
                else:
                    tile = slab_ref[pl.ds(d * LANES * CHAIN_PITCH + k, LANES, stride=CHAIN_PITCH), :]
                o_ref.at[d][pl.ds(k, RELAYOUT_TOK, stride=CHAIN_PITCH), :] = tile.T
        return carry

    lax.fori_loop(0, HEAD_DIM // SUBLANES, stage2, 0)

    zero = jnp.zeros((RELAYOUT_TOK, LANES), F32)
    for d in range(n_grp):
        for pad in range(HEAD_DIM, CHAIN_PITCH):
            o_ref.at[d][pl.ds(pad, RELAYOUT_TOK, stride=CHAIN_PITCH), :] = zero


def _to_chains_call(x2d, n_batch, seq_len, dup):
    n_chan = x2d.shape[1]
    n_slab = (n_chan // HEAD_DIM) * n_batch
    n_grp = n_slab * (2 if dup else 1) // LANES
    out = pl.pallas_call(
        functools.partial(_to_chains_kernel, n_batch=n_batch, dup=dup),
        grid=(seq_len // RELAYOUT_TOK,),
        in_specs=[pl.BlockSpec((n_batch, RELAYOUT_TOK, n_chan), lambda i: (0, i, 0))],
        out_specs=pl.BlockSpec((n_grp, RELAYOUT_TOK * CHAIN_PITCH, LANES), lambda i: (0, i, 0)),
        out_shape=jax.ShapeDtypeStruct((n_grp, seq_len * CHAIN_PITCH, LANES), F32),
        scratch_shapes=[pltpu.VMEM((n_slab * CHAIN_PITCH, LANES), F32)],
        compiler_params=pltpu.CompilerParams(dimension_semantics=("arbitrary",),
                                             vmem_limit_bytes=VMEM_LIMIT),
        name="to_chains_dup" if dup else "to_chains",
    )(x2d.reshape(n_batch, seq_len, n_chan))
    return out.reshape(n_grp, seq_len, CHAIN_PITCH, LANES)


def _from_chains_kernel(yf_ref, yb_ref, o_ref, slab_ref, *, n_batch, packed):
    def stage1(vb, carry):
        for vi in range(SUBLANES):
            v = vb * SUBLANES + vi
            tf = yf_ref[pl.ds(v, RELAYOUT_TOK, stride=CHAIN_PITCH), :]
            tb = yb_ref[pl.ds(v, RELAYOUT_TOK, stride=CHAIN_PITCH), :]
            if packed:
                tb = pltpu.roll(tb, LANES // 2, 1)
            slab_ref[pl.ds(v, LANES, stride=CHAIN_PITCH), :] = (tf + tb).T
        return carry

    lax.fori_loop(0, HEAD_DIM // SUBLANES, stage1, 0)

    def stage2(b, carry):
        for jc in range(H_RWKV // 2):
            rows = []
            for h2 in range(2):
                row0 = pl.multiple_of(((2 * jc + h2) * n_batch + b) * CHAIN_PITCH, SUBLANES)
                rows.append(slab_ref[pl.ds(row0, HEAD_DIM), :])
            o_ref[b, :, LANES * jc:LANES * (jc + 1)] = jnp.concatenate(rows, axis=0).T
        return carry

    lax.fori_loop(0, n_batch, stage2, 0, unroll=RELAYOUT_BATCH_UNROLL)


def _from_chains_call(y_f, y_b, n_batch, seq_len, packed):
    n_rows = seq_len * CHAIN_PITCH
    grp_b = 0 if packed else 1
    blk = (None, RELAYOUT_TOK * CHAIN_PITCH, LANES)
    out = pl.pallas_call(
        functools.partial(_from_chains_kernel, n_batch=n_batch, packed=packed),
        grid=(seq_len // RELAYOUT_TOK,),
        in_specs=[pl.BlockSpec(blk, lambda i: (0, i, 0)), pl.BlockSpec(blk, lambda i: (grp_b, i, 0))],
        out_specs=pl.BlockSpec((n_batch, RELAYOUT_TOK, D_RWKV), lambda i: (0, i, 0)),
        out_shape=jax.ShapeDtypeStruct((n_batch, seq_len, D_RWKV), F32),
        scratch_shapes=[pltpu.VMEM((LANES * CHAIN_PITCH, LANES), F32)],
        compiler_params=pltpu.CompilerParams(dimension_semantics=("arbitrary",),
                                             vmem_limit_bytes=VMEM_LIMIT),
        name="from_chains_packed" if packed else "from_chains",
    )(y_f.reshape(-1, n_rows, LANES), y_b.reshape(-1, n_rows, LANES))
    return out.reshape(n_batch * seq_len, D_RWKV)


def _scan_kernel(*refs, packed, has_init):
    (rF, rB, kkF, kkB, kF, kB, vF, vB, wF, wB, aF, aB, ka_ref), rest = refs[:13], refs[13:]
    if has_init:
        s0_ref, rest = rest[0], rest[1:]
    yF, yB, sfin_ref, s_scr, sa_scr, op_scr, p_scr = rest
    g = pl.program_id(0)
    j = pl.program_id(1)
    tile = (HEAD_DIM, LANES)

    if packed:
        use_b = lax.broadcasted_iota(jnp.int32, tile, 1) >= LANES // 2
    else:
        use_b = jnp.full(tile, g, jnp.int32) == 1

    def pick(f_ref, b_ref, s):
        return jnp.where(use_b, b_ref[SCAN_STEPS - 1 - s, 0:HEAD_DIM, :], f_ref[s, 0:HEAD_DIM, :])

    @pl.when(j == 0)
    def _():
        if has_init:
            s_scr[...] = s0_ref[...]
        else:
            s_scr[...] = jnp.zeros(s_scr.shape, F32)

    op_scr[3] = pick(kkF, kkB, 0)
    acc = jnp.zeros(tile, F32)
    for k in range(HEAD_DIM):
        acc = acc + s_scr[k] * op_scr[3, k:k + 1, :]
    sa_scr[...] = acc

    ka = ka_ref[...]
    pad = jnp.zeros((SCAN_STEPS, CHAIN_PITCH - HEAD_DIM, LANES), F32)
    yF[:, HEAD_DIM:CHAIN_PITCH, :] = pad
    yB[:, HEAD_DIM:CHAIN_PITCH, :] = pad

    p_scr[...] = jnp.ones(tile, F32)

    def step(s, carry):
        a = pick(aF, aB, s)
        p = p_scr[...] * pick(wF, wB, s)
        p_scr[...] = p
        inv_p = 1.0 / p
        op_scr[0] = pick(kkF, kkB, s) * a * inv_p
        op_scr[1] = pick(kF, kB, s) * (1.0 + (a - 1.0) * ka) * inv_p
        op_scr[2] = pick(rF, rB, s) * p
        op_scr[3] = pick(kkF, kkB, jnp.minimum(s + 1, SCAN_STEPS - 1)) * p
        vt = pick(vF, vB, s)
        sa = sa_scr[...]

        def kblock(kb, acc):
            y, san = acc
            for ki in range(SCAN_K_UNROLL):
                k = kb * SCAN_K_UNROLL + ki
                bk = op_scr[0, pl.ds(k, 1), :]
                kdk = op_scr[1, pl.ds(k, 1), :]
                rk = op_scr[2, pl.ds(k, 1), :]
                kn = op_scr[3, pl.ds(k, 1), :]
                sn = s_scr[k] - sa * bk + vt * kdk
                s_scr[k] = sn
                y = y + sn * rk
                san = san + sn * kn
            return y, san

        zero = jnp.zeros(tile, F32)
        y, san = lax.fori_loop(0, HEAD_DIM // SCAN_K_UNROLL, kblock, (zero, zero))
        yF[s, 0:HEAD_DIM, :] = y
        yB[SCAN_STEPS - 1 - s, 0:HEAD_DIM, :] = y
        sa_scr[...] = san
        return carry

    lax.fori_loop(0, SCAN_STEPS, step, 0)

    for k in range(HEAD_DIM):
        s_scr[k] = s_scr[k] * p_scr[k:k + 1, :]

    @pl.when(j == pl.num_programs(1) - 1)
    def _():
        sfin_ref[...] = s_scr[...]


def _scan_call(shared, w, a, ka_tab, packed, s0=None):
    n_step = w.shape[1]
    n_grp = 1 if packed else 2
    n_blk = n_step // SCAN_STEPS
    has_init = s0 is not None
    blk = (None, SCAN_STEPS, CHAIN_PITCH, LANES)
    if packed:
        f_map = lambda g, j: (0, j, 0, 0)
        b_map = lambda g, j: (0, n_blk - 1 - j, 0, 0)
        fd_map, bd_map = f_map, b_map
    else:
        f_map = lambda g, j: (0, jnp.where(g == 0, j, 0), 0, 0)
        b_map = lambda g, j: (0, jnp.where(g == 1, n_blk - 1 - j, 0), 0, 0)
        fd_map = lambda g, j: (0, jnp.where(g == 0, j, 0), 0, 0)
        bd_map = lambda g, j: (1, jnp.where(g == 1, n_blk - 1 - j, 0), 0, 0)
    in_specs, args = [], []
    for x in shared:
        in_specs += [pl.BlockSpec(blk, f_map), pl.BlockSpec(blk, b_map)]
        args += [x, x]
    for x in (w, a):
        in_specs += [pl.BlockSpec(blk, fd_map), pl.BlockSpec(blk, bd_map)]
        args += [x, x]
    in_specs.append(pl.BlockSpec((HEAD_DIM, LANES), lambda g, j: (0, 0)))
    args.append(ka_tab)
    state_spec = pl.BlockSpec((None, HEAD_DIM, HEAD_DIM, LANES), lambda g, j: (g, 0, 0, 0))
    if has_init:
        in_specs.append(state_spec)
        args.append(s0)
    y_shape = jax.ShapeDtypeStruct((n_grp, n_step, CHAIN_PITCH, LANES), F32)
    return pl.pallas_call(
        functools.partial(_scan_kernel, packed=packed, has_init=has_init),
        grid=(n_grp, n_blk),
        in_specs=in_specs,
        out_specs=[pl.BlockSpec(blk, lambda g, j: (g, j, 0, 0)),
                   pl.BlockSpec(blk, lambda g, j: (g, n_blk - 1 - j, 0, 0)),
                   state_spec],
        out_shape=[y_shape, y_shape, jax.ShapeDtypeStruct((n_grp, HEAD_DIM, HEAD_DIM, LANES), F32)],
        scratch_shapes=[pltpu.VMEM((HEAD_DIM, HEAD_DIM, LANES), F32), pltpu.VMEM((HEAD_DIM, LANES), F32),
                        pltpu.VMEM((4, HEAD_DIM, LANES), F32), pltpu.VMEM((HEAD_DIM, LANES), F32)],
        compiler_params=pltpu.CompilerParams(dimension_semantics=("arbitrary", "arbitrary"),
                                             vmem_limit_bytes=VMEM_LIMIT),
        name="wkv_scan_packed" if packed else "wkv_scan",
    )(*args)


def _attn_kernel(q_ref, k_ref, v_ref, o_ref):
    q = q_ref[...]
    k = k_ref[...]
    v = v_ref[...]
    group = H_Q // H_KV
    for kvh in range(H_KV):
        kh = k[:, HEAD_DIM * kvh:HEAD_DIM * (kvh + 1)].astype(BF16)
        vh = v[:, HEAD_DIM * kvh:HEAD_DIM * (kvh + 1)].astype(BF16)
        for gq in range(group):
            hq = kvh * group + gq
            qh = (q[:, HEAD_DIM * hq:HEAD_DIM * (hq + 1)] * (HEAD_DIM ** -0.5)).astype(BF16)
            s = lax.dot_general(qh, kh, (((1,), (1,)), ((), ())), preferred_element_type=F32)
            e = jnp.exp(s - jnp.max(s, axis=-1, keepdims=True))
            o = _dot(e.astype(BF16), vh) / jnp.sum(e, axis=-1, keepdims=True)
            o_ref[:, HEAD_DIM * hq:HEAD_DIM * (hq + 1)] = o


def _attn_call(q2d, k2d, v2d, n_batch, lq, lk):
    q_tile = min(ATTN_Q_TILE, lq)
    q_tiles = lq // q_tile
    return pl.pallas_call(
        _attn_kernel,
        grid=(n_batch, q_tiles),
        in_specs=[
            pl.BlockSpec((q_tile, D_ATTN), lambda b, i: (b * q_tiles + i, 0)),
            pl.BlockSpec((lk, D_KV), lambda b, i: (b, 0)),
            pl.BlockSpec((lk, D_KV), lambda b, i: (b, 0)),
        ],
        out_specs=pl.BlockSpec((q_tile, D_ATTN), lambda b, i: (b * q_tiles + i, 0)),
        out_shape=jax.ShapeDtypeStruct((n_batch * lq, D_ATTN), F32),
        compiler_params=pltpu.CompilerParams(dimension_semantics=("arbitrary", "arbitrary"),
                                             vmem_limit_bytes=VMEM_LIMIT),
        name="gqa_attn",
    )(q2d, k2d, v2d)


def _post_kernel(x_ref, mod_ref, ys_ref, bonus_ref, g_ref, attn_ref,
                 gn_ref, wout_ref, gpost_ref, gffn_ref, x1_o, h2_o):
    ones512 = _head_ones(D_RWKV)
    ys = ys_ref[...]
    ys_hi, ys_lo = _split_bf16(ys)
    mu = (_dot(ys_hi, ones512) + _dot(ys_lo, ones512)) * (1.0 / HEAD_DIM)
    dlt = ys - mu
    var = _dot_exact_rhs(dlt * dlt, ones512) * (1.0 / HEAD_DIM)
    gn = gn_ref[...]
    yn = dlt * lax.rsqrt(var + GN_EPS) * gn[0:1] + gn[1:2]
    yr = (yn + bonus_ref[...]) * g_ref[...]
    mix = (_dot(yr.astype(BF16), wout_ref[0:D_RWKV, :])
           + _dot(attn_ref[...].astype(BF16), wout_ref[D_RWKV:D_MODEL, :]))
    m = mod_ref[0]
    gate1 = m[:, 2 * D_MODEL:3 * D_MODEL]
    shift2 = m[:, 3 * D_MODEL:4 * D_MODEL]
    scale2 = m[:, 4 * D_MODEL:5 * D_MODEL]
    x1 = x_ref[...] + gate1 * _rms(mix, gpost_ref[...])
    x1_o[...] = x1
    h2_o[...] = (_rms(x1, gffn_ref[...]) * (1.0 + scale2) + shift2).astype(BF16)


def _post_call(x2d, mod3d, seq_len, ys, bonus, g, attn, wts):
    n_tok = x2d.shape[0]
    tiles_per_seq = seq_len // TOK_TILE
    n_mod = mod3d.shape[0]

    def const(shape):
        return pl.BlockSpec(shape, lambda i: tuple(0 for _ in shape))

    mod_map = (lambda i: (i // tiles_per_seq, 0, 0)) if n_mod > 1 else (lambda i: (0, 0, 0))
    tok = lambda wd: pl.BlockSpec((TOK_TILE, wd), lambda i: (i, 0))
    return pl.pallas_call(
        _post_kernel,
        grid=(n_tok // TOK_TILE,),
        in_specs=[tok(D_MODEL), pl.BlockSpec((1, 1, 6 * D_MODEL), mod_map),
                  tok(512), tok(512), tok(512), tok(512),
                  const((2, D_RWKV)), const((D_MODEL, D_MODEL)), const((1, D_MODEL)), const((1, D_MODEL))],
        out_specs=[tok(D_MODEL), tok(D_MODEL)],
        out_shape=[jax.ShapeDtypeStruct((n_tok, D_MODEL), F32), jax.ShapeDtypeStruct((n_tok, D_MODEL), BF16)],
        compiler_params=pltpu.CompilerParams(dimension_semantics=("arbitrary",),
                                             vmem_limit_bytes=VMEM_LIMIT),
        name="mix_post",
    )(x2d, mod3d, ys, bonus, g, attn, wts["gn"], wts["w_out"], wts["g_post"], wts["g_ffn_pre"])


def _ffn_kernel(h2_ref, x1_ref, mod_ref, wg_ref, wv_ref, cwg_ref, cwv_ref, cbg_ref, cbv_ref,
                wd_ref, gpost_ref, o_ref, acc_ref, *, seq_len):
    j = pl.program_id(1)

    @pl.when(j == 0)
    def _():
        acc_ref[...] = jnp.zeros(acc_ref.shape, F32)

    h2 = h2_ref[...]
    n_rows = h2.shape[0]
    assert seq_len & (seq_len - 1) == 0
    row = lax.broadcasted_iota(jnp.int32, (n_rows, FFN_SUB_TILE), 0) & (seq_len - 1)
    has_prev = row != 0
    has_next = row != seq_len - 1

    def conv(u, cw, cb):
        prev = jnp.where(has_prev, pltpu.roll(u, 1, 0), 0.0)
        nxt = jnp.where(has_next, pltpu.roll(u, n_rows - 1, 0), 0.0)
        return ((cb + prev * cw[0:1]) + u * cw[1:2]) + nxt * cw[2:3]

    down = None
    for c0 in range(0, FFN_COL_TILE, FFN_SUB_TILE):
        cols = slice(c0, c0 + FFN_SUB_TILE)
        gate = conv(_dot(h2, wg_ref[:, cols]), cwg_ref[:, cols], cbg_ref[:, cols])
        val = conv(_dot(h2, wv_ref[:, cols]), cwv_ref[:, cols], cbv_ref[:, cols])
        act = (gate * _sigmoid(gate) * val).astype(BF16)
        part = _dot(act, wd_ref[cols, :])
        down = part if down is None else down + part
    acc_ref[...] += down

    @pl.when(j == pl.num_programs(1) - 1)
    def _():
        m = mod_ref[0]
        gate2 = m[:, 5 * D_MODEL:6 * D_MODEL]
        o_ref[...] = x1_ref[...] + gate2 * _rms(acc_ref[...], gpost_ref[...])


def _ffn_call(h2, x1, mod3d, seq_len, wts):
    n_tok = h2.shape[0]
    tiles_per_seq = max(seq_len // FFN_TOK_TILE, 1)
    n_mod = mod3d.shape[0]
    n_col = D_FF_PAD // FFN_COL_TILE
    mod_map = (lambda i, j: (i // tiles_per_seq, 0, 0)) if n_mod > 1 else (lambda i, j: (0, 0, 0))
    tok = lambda: pl.BlockSpec((FFN_TOK_TILE, D_MODEL), lambda i, j: (i, 0))
    return pl.pallas_call(
        functools.partial(_ffn_kernel, seq_len=seq_len),
        grid=(n_tok // FFN_TOK_TILE, n_col),
        in_specs=[
            tok(), tok(), pl.BlockSpec((1, 1, 6 * D_MODEL), mod_map),
            pl.BlockSpec((D_MODEL, FFN_COL_TILE), lambda i, j: (0, j)),
            pl.BlockSpec((D_MODEL, FFN_COL_TILE), lambda i, j: (0, n_col + j)),
            pl.BlockSpec((3, FFN_COL_TILE), lambda i, j: (0, j)),
            pl.BlockSpec((3, FFN_COL_TILE), lambda i, j: (0, n_col + j)),
            pl.BlockSpec((1, FFN_COL_TILE), lambda i, j: (0, j)),
            pl.BlockSpec((1, FFN_COL_TILE), lambda i, j: (0, n_col + j)),
            pl.BlockSpec((FFN_COL_TILE, D_MODEL), lambda i, j: (j, 0)),
            pl.BlockSpec((1, D_MODEL), lambda i, j: (0, 0)),
        ],
        out_specs=tok(),
        out_shape=jax.ShapeDtypeStruct((n_tok, D_MODEL), F32),
        scratch_shapes=[pltpu.VMEM((FFN_TOK_TILE, D_MODEL), F32)],
        compiler_params=pltpu.CompilerParams(dimension_semantics=("arbitrary", "arbitrary"),
                                             vmem_limit_bytes=VMEM_LIMIT),
        name="conv_ffn",
    )(h2, x1, mod3d, wts["ffn_up"], wts["ffn_up"], wts["conv_w"], wts["conv_w"],
      wts["conv_b"], wts["conv_b"], wts["ffn_down"], wts["g_ffn_post"])


def _rope_tables(n_tok, n_heads):
    quarter = HEAD_DIM // 4
    pos = jnp.arange(n_tok)
    row = (pos // GRID_W).astype(F32)
    col = (pos % GRID_W).astype(F32)
    inv_freq = ROPE_THETA ** (-jnp.arange(quarter, dtype=F32) / quarter)

    def half(p):
        ang = p[:, None] * inv_freq[None, :]
        c, s = jnp.cos(ang), jnp.sin(ang)
        return jnp.concatenate([c, c], axis=-1), jnp.concatenate([-s, s], axis=-1)

    cr, sr = half(row)
    cc, sc = half(col)
    cos = jnp.concatenate([cr, cc], axis=-1)
    sin = jnp.concatenate([sr, sc], axis=-1)
    return jnp.tile(cos, (1, n_heads)), jnp.tile(sin, (1, n_heads))


def _pad_ff_cols(x):
    pad = ((0, 0), (0, D_FF_PAD - D_FF))
    return jnp.concatenate([jnp.pad(x[:, :D_FF], pad), jnp.pad(x[:, D_FF:], pad)], axis=1)


def _mixer(pre, n_batch, seq_len, packed, k_a, s0=None):
    r, v, kk, k, w, a = pre[:6]
    shared = tuple(_to_chains_call(x, n_batch, seq_len, packed) for x in (r, kk, k, v))
    ka_tab = jnp.repeat(k_a.reshape(H_RWKV, HEAD_DIM).T, n_batch, axis=1)
    ka_tab = jnp.tile(ka_tab, (1, LANES // ka_tab.shape[1]))
    y_f, y_b, s_fin = _scan_call(shared, _to_chains_call(w, n_batch, seq_len, False),
                                 _to_chains_call(a, n_batch, seq_len, False), ka_tab, packed, s0)
    return _from_chains_call(y_f, y_b, n_batch, seq_len, packed), s_fin


def kernel(x_prompt, x_sample, cache_k, cache_v, state_rwkv, c, c_ctx, w_mod, b_mod, norm_mix_pre, norm_mix_post, norm_ffn_pre, norm_ffn_post, w_in, w0, w_up, a0, a_up, g_up, k_k, k_a, r_k, gn_w, gn_b, q_norm, k_norm, w_out, ffn_up, conv_w, conv_b, ffn_down):
    n_ctx, l_ctx = x_prompt.shape[0], x_prompt.shape[1]
    n_lat, l_lat = x_sample.shape[0], x_sample.shape[1]
    l_past = cache_k.shape[2]
    layer = 0
    assert n_ctx * H_RWKV == LANES and 2 * n_lat * H_RWKV == LANES

    zeros_w = jnp.zeros((LORA_W, D_RWKV), F32)
    w_lora = jnp.concatenate([
        jnp.concatenate([w_up[layer, 0], w_up[layer, 1], zeros_w, zeros_w], axis=1),
        jnp.concatenate([zeros_w, zeros_w, a_up[layer, 0], a_up[layer, 1]], axis=1)], axis=0)
    vec = jnp.stack([k_k[layer], k_a[layer], r_k[layer].reshape(D_RWKV), jnp.tile(q_norm[layer], H_Q)])
    wts = dict(
        g_pre=norm_mix_pre[layer].reshape(1, D_MODEL),
        g_post=norm_mix_post[layer].reshape(1, D_MODEL),
        g_ffn_pre=norm_ffn_pre[layer].reshape(1, D_MODEL),
        g_ffn_post=norm_ffn_post[layer].reshape(1, D_MODEL),
        w_in=w_in[layer].astype(BF16),
        w_lora=w_lora.astype(BF16),
        g_up=g_up[layer].astype(BF16),
        vec=vec,
        wa0=jnp.stack([w0[layer].reshape(2 * D_RWKV), a0[layer].reshape(2 * D_RWKV)]),
        k_norm=jnp.tile(k_norm[layer], H_KV).reshape(1, D_KV),
        gn=jnp.stack([gn_w[layer], gn_b[layer]]),
        w_out=w_out[layer].astype(BF16),
        ffn_up=_pad_ff_cols(ffn_up[layer].astype(BF16)),
        conv_w=_pad_ff_cols(conv_w[layer]),
        conv_b=_pad_ff_cols(conv_b[layer].reshape(1, 2 * D_FF)),
        ffn_down=jnp.pad(ffn_down[layer].astype(BF16), ((0, D_FF_PAD - D_FF), (0, 0))),
    )

    cvec = jnp.concatenate([c, c_ctx[None, :], jnp.zeros((16 - n_lat - 1, D_MODEL), F32)], axis=0)
    mod = _mod_call(cvec, w_mod[layer], b_mod[layer])
    mod_lat = mod[:n_lat].reshape(n_lat, 1, 6 * D_MODEL)
    mod_ctx = mod[n_lat:n_lat + 1].reshape(1, 1, 6 * D_MODEL)

    xc = x_prompt.reshape(n_ctx * l_ctx, D_MODEL)
    pre_c = _pre_call(xc, mod_ctx, l_ctx, wts, None)
    g_c, bonus_c, q_c, ka_c, va_c = pre_c[6:]
    ys_c, s_c = _mixer(pre_c, n_ctx, l_ctx, False, k_a[layer])
    attn_c = _attn_call(q_c, ka_c, va_c, n_ctx, l_ctx, l_ctx)
    x1_c, h2_c = _post_call(xc, mod_ctx, l_ctx, ys_c, bonus_c, g_c, attn_c, wts)
    out_c = _ffn_call(h2_c, x1_c, mod_ctx, l_ctx, wts)

    xl = x_sample.reshape(n_lat * l_lat, D_MODEL)
    tabs = _rope_tables(l_lat, H_Q) + _rope_tables(l_lat, H_KV)
    pre_l = _pre_call(xl, mod_lat, l_lat, wts, tabs)
    g_l, bonus_l, q_l, kr_l, va_l = pre_l[6:]
    s0 = state_rwkv[:, layer].transpose(4, 3, 1, 2, 0).reshape(1, HEAD_DIM, HEAD_DIM, LANES)
    ys_l, _ = _mixer(pre_l, n_lat, l_lat, True, k_a[layer], s0)
    k_all = jnp.concatenate([kr_l.reshape(n_lat, l_lat, D_KV), cache_k[:, layer].reshape(n_lat, l_past, D_KV)],
                            axis=1).reshape(n_lat * (l_lat + l_past), D_KV)
    v_all = jnp.concatenate([va_l.reshape(n_lat, l_lat, D_KV), cache_v[:, layer].reshape(n_lat, l_past, D_KV)],
                            axis=1).reshape(n_lat * (l_lat + l_past), D_KV)
    attn_l = _attn_call(q_l, k_all, v_all, n_lat, l_lat, l_lat + l_past)
    x1_l, h2_l = _post_call(xl, mod_lat, l_lat, ys_l, bonus_l, g_l, attn_l, wts)
    out_l = _ffn_call(h2_l, x1_l, mod_lat, l_lat, wts)

    y_prompt = out_c.reshape(n_ctx, l_ctx, D_MODEL)
    y_sample = out_l.reshape(n_lat, l_lat, D_MODEL)
    new_cache_k = ka_c.reshape(n_ctx, 1, l_ctx, H_KV, HEAD_DIM)
    new_cache_v = va_c.reshape(n_ctx, 1, l_ctx, H_KV, HEAD_DIM)
    new_state = s_c.reshape(2, HEAD_DIM, HEAD_DIM, H_RWKV, n_ctx).transpose(4, 0, 3, 2, 1)[:, None]
    return (y_prompt, y_sample, new_cache_k, new_cache_v, new_state)
```

```python
import functools

import jax
import jax.numpy as jnp
from jax import lax
from jax.experimental import pallas as pl
from jax.experimental.pallas import tpu as pltpu

F32 = jnp.float32
BF16 = jnp.bfloat16

D_MODEL = 1024
HEAD_DIM = 64
D_RWKV = 512
H_RWKV = 8
D_ATTN = 512
H_Q = 8
H_KV = 2
D_KV = 128
LORA_W = 64
LORA_A = 64
LORA_G = 128
D_FF = 2816
D_IN = 2560
GRID_W = 64
ROPE_THETA = 10000.0
RMS_EPS = 1e-6
GN_EPS = 64e-5

LANES = 128
MXU_TILE = 256
SUBLANES = 8
TOK_TILE = 512
FFN_TOK_TILE = 1024
FFN_COL_TILE = 512
FFN_SUB_TILE = 256
D_FF_PAD = -(-D_FF // FFN_COL_TILE) * FFN_COL_TILE
SCAN_STEPS = 32
SCAN_K_UNROLL = 16
RELAYOUT_TOK = 128
RELAYOUT_BATCH_UNROLL = 4
CHAIN_PITCH = 72
ATTN_Q_TILE = 512
MOD_COL_TILE = 768
VMEM_LIMIT = 56 * 1024 * 1024


def _dot(a, b):
    return jnp.dot(a, b, preferred_element_type=F32)


def _split_bf16(x):
    hi = x.astype(BF16)
    lo = (x - hi.astype(F32)).astype(BF16)
    return hi, lo


def _dot_exact_rhs(x, w_bf16):
    xb = x.astype(BF16)
    wd = w_bf16.shape[0]
    parts = [_dot(xb[:, c:c + wd], w_bf16) for c in range(0, x.shape[-1], wd)]
    return parts[0] if len(parts) == 1 else jnp.concatenate(parts, axis=1)


def _dot3(a, b):
    ah, al = _split_bf16(a)
    bh, bl = _split_bf16(b)
    return _dot(ah, bh) + (_dot(ah, bl) + _dot(al, bh))


def _head_ones(n):
    shift = HEAD_DIM.bit_length() - 1
    r = lax.shift_right_logical(lax.broadcasted_iota(jnp.int32, (n, n), 0), shift)
    c = lax.shift_right_logical(lax.broadcasted_iota(jnp.int32, (n, n), 1), shift)
    return jnp.where(r == c, 1.0, 0.0).astype(BF16)


def _sigmoid(x):
    return 1.0 / (1.0 + jnp.exp(-x))


def _softplus(x):
    return jnp.maximum(x, 0.0) + jnp.log(1.0 + jnp.exp(-jnp.abs(x)))


def _rms(x, g):
    return x * lax.rsqrt(jnp.mean(x * x, axis=-1, keepdims=True) + RMS_EPS) * g


def _rope(x, cos, sgn_sin):
    n = x.shape[-1]
    lane = lax.broadcasted_iota(jnp.int32, x.shape, 1)
    partner = jnp.where((lane & 16) == 0, pltpu.roll(x, n - 16, 1), pltpu.roll(x, 16, 1))
    return x * cos + partner * sgn_sin


def _mod_kernel(c_ref, w_ref, b_ref, o_ref):
    c = c_ref[...]
    o_ref[...] = _dot3(c * _sigmoid(c), w_ref[...]) + b_ref[...]


def _mod_call(cvec, w_mod, b_mod):
    n = w_mod.shape[1]
    rows = cvec.shape[0]
    return pl.pallas_call(
        _mod_kernel,
        grid=(n // MOD_COL_TILE,),
        in_specs=[
            pl.BlockSpec((rows, D_MODEL), lambda j: (0, 0)),
            pl.BlockSpec((D_MODEL, MOD_COL_TILE), lambda j: (0, j)),
            pl.BlockSpec((1, MOD_COL_TILE), lambda j: (0, j)),
        ],
        out_specs=pl.BlockSpec((rows, MOD_COL_TILE), lambda j: (0, j)),
        out_shape=jax.ShapeDtypeStruct((rows, n), F32),
        compiler_params=pltpu.CompilerParams(dimension_semantics=("arbitrary",)),
        name="adaln_mod",
    )(cvec, w_mod, b_mod.reshape(1, n))


_V_KK, _V_KA, _V_RK, _V_QN = range(4)


def _pre_kernel(*refs, rope):
    if rope:
        (x_ref, mod_ref, gpre_ref, win_ref, wlora_ref, gup_ref, vec_ref, wa0_ref, kn_ref,
         cq_ref, sq_ref, ck_ref, sk_ref,
         r_o, v_o, kk_o, k_o, w_o, a_o, g_o, bonus_o, q_o, ka_o, va_o) = refs
    else:
        (x_ref, mod_ref, gpre_ref, win_ref, wlora_ref, gup_ref, vec_ref, wa0_ref, kn_ref,
         r_o, v_o, kk_o, k_o, w_o, a_o, g_o, bonus_o, q_o, ka_o, va_o) = refs

    x = x_ref[...]
    m = mod_ref[0]
    shift1 = m[:, 0:D_MODEL]
    scale1 = m[:, D_MODEL:2 * D_MODEL]
    h = _rms(x, gpre_ref[...]) * (1.0 + scale1) + shift1
    p = _dot(h.astype(BF16), win_ref[...])

    r = p[:, 0:512]
    k = p[:, 512:1024]
    v = p[:, 1024:1536]
    xwa = p[:, 1536:1664]
    xg = p[:, 1664:1792]
    q = p[:, 1792:2304]
    ka = p[:, 2304:2432]
    va = p[:, 2432:2560]

    vec = vec_ref[...]
    head_ones = _head_ones(MXU_TILE)

    g_o[...] = _dot(_sigmoid(xg).astype(BF16), gup_ref[...])

    kx = k * vec[_V_KK:_V_KK + 1]
    kk_o[...] = kx * lax.rsqrt(_dot_exact_rhs(kx * kx, head_ones) + 1e-12)

    lane = lax.broadcasted_iota(jnp.int32, xwa.shape, 1)
    lora_in = jnp.where(lane < LORA_W, jnp.tanh(xwa), xwa)
    lora = _dot(lora_in.astype(BF16), wlora_ref[...])

    z = wa0_ref[0:1] + lora[:, 0:2 * D_RWKV]
    logw = -_softplus(-z) - 0.5
    w_o[...] = jnp.exp(-jnp.exp(logw))
    a = _sigmoid(wa0_ref[1:2] + lora[:, 2 * D_RWKV:4 * D_RWKV])
    a_o[...] = a

    k_a = vec[_V_KA:_V_KA + 1]
    kd_sum = k * (1.0 + (a[:, 0:D_RWKV] - 1.0) * k_a) + k * (1.0 + (a[:, D_RWKV:2 * D_RWKV] - 1.0) * k_a)
    bonus_o[...] = _dot_exact_rhs(r * kd_sum * vec[_V_RK:_V_RK + 1], head_ones) * v

    r_o[...] = r
    v_o[...] = v
    k_o[...] = k

    qn = q * lax.rsqrt(_dot_exact_rhs(q * q, head_ones) * (1.0 / HEAD_DIM) + RMS_EPS) * vec[_V_QN:_V_QN + 1]
    kan = ka * lax.rsqrt(_dot_exact_rhs(ka * ka, _head_ones(D_KV)) * (1.0 / HEAD_DIM) + RMS_EPS) * kn_ref[...]
    if rope:
        qn = _rope(qn, cq_ref[...], sq_ref[...])
        kan = _rope(kan, ck_ref[...], sk_ref[...])
    q_o[...] = qn
    ka_o[...] = kan
    va_o[...] = va


def _pre_call(x2d, mod3d, seq_len, wts, rope_tabs):
    n_tok = x2d.shape[0]
    tiles_per_seq = seq_len // TOK_TILE
    n_mod = mod3d.shape[0]
    rope = rope_tabs is not None

    def const(shape):
        return pl.BlockSpec(shape, lambda i: tuple(0 for _ in shape))

    mod_map = (lambda i: (i // tiles_per_seq, 0, 0)) if n_mod > 1 else (lambda i: (0, 0, 0))
    in_specs = [
        pl.BlockSpec((TOK_TILE, D_MODEL), lambda i: (i, 0)),
        pl.BlockSpec((1, 1, 6 * D_MODEL), mod_map),
        const((1, D_MODEL)),
        const((D_MODEL, D_IN)),
        const((LORA_W + LORA_A, 4 * D_RWKV)),
        const((LORA_G, D_RWKV)),
        const((4, D_RWKV)),
        const((2, 2 * D_RWKV)),
        const((1, D_KV)),
    ]
    args = [x2d, mod3d, wts["g_pre"], wts["w_in"], wts["w_lora"], wts["g_up"], wts["vec"], wts["wa0"],
            wts["k_norm"]]
    if rope:
        tab_map = lambda i: (i % tiles_per_seq, 0)
        in_specs += [pl.BlockSpec((TOK_TILE, D_ATTN), tab_map), pl.BlockSpec((TOK_TILE, D_ATTN), tab_map),
                     pl.BlockSpec((TOK_TILE, D_KV), tab_map), pl.BlockSpec((TOK_TILE, D_KV), tab_map)]
        args += list(rope_tabs)

    widths = [512] * 4 + [1024, 1024] + [512] * 3 + [D_KV, D_KV]
    out_specs = [pl.BlockSpec((TOK_TILE, wd), lambda i: (i, 0)) for wd in widths]
    out_shape = [jax.ShapeDtypeStruct((n_tok, wd), F32) for wd in widths]
    return pl.pallas_call(
        functools.partial(_pre_kernel, rope=rope),
        grid=(n_tok // TOK_TILE,),
        in_specs=in_specs,
        out_specs=out_specs,
        out_shape=out_shape,
        compiler_params=pltpu.CompilerParams(dimension_semantics=("arbitrary",),
                                             vmem_limit_bytes=VMEM_LIMIT),
        name="pre_rope" if rope else "pre",
    )(*args)


def _to_chains_kernel(x_ref, o_ref, slab_ref, *, n_batch, dup):
    n_col = x_ref.shape[2] // HEAD_DIM
    n_grp = o_ref.shape[0]

    def stage1(b, carry):
        for jc in range(n_col // 2):
            a_t = x_ref[b, :, LANES * jc:LANES * (jc + 1)].T
            for h2 in range(2):
                row0 = pl.multiple_of(((2 * jc + h2) * n_batch + b) * CHAIN_PITCH, SUBLANES)
                slab_ref[pl.ds(row0, HEAD_DIM), :] = a_t[HEAD_DIM * h2:HEAD_DIM * (h2 + 1), :]
        return carry

    lax.fori_loop(0, n_batch, stage1, 0, unroll=RELAYOUT_BATCH_UNROLL)

    def stage2(kb, carry):
        for ki in range(SUBLANES):
            k = kb * SUBLANES + ki
            for d in range(n_grp):
                if dup:
                    half = slab_ref[pl.ds(k, LANES // 2, stride=CHAIN_PITCH), :]
                    tile = jnp.concatenate([half, half], axis=0)
                else:
                    tile = slab_ref[pl.ds(d * LANES * CHAIN_PITCH + k, LANES, stride=CHAIN_PITCH), :]
                o_ref.at[d][pl.ds(k, RELAYOUT_TOK, stride=CHAIN_PITCH), :] = tile.T
        return carry

    lax.fori_loop(0, HEAD_DIM // SUBLANES, stage2, 0)

    zero = jnp.zeros((RELAYOUT_TOK, LANES), F32)
    for d in range(n_grp):
        for pad in range(HEAD_DIM, CHAIN_PITCH):
            o_ref.at[d][pl.ds(pad, RELAYOUT_TOK, stride=CHAIN_PITCH), :] = zero


def _to_chains_call(x2d, n_batch, seq_len, dup):
    n_chan = x2d.shape[1]
    n_slab = (n_chan // HEAD_DIM) * n_batch
    n_grp = n_slab * (2 if dup else 1) // LANES
    out = pl.pallas_call(
        functools.partial(_to_chains_kernel, n_batch=n_batch, dup=dup),
        grid=(seq_len // RELAYOUT_TOK,),
        in_specs=[pl.BlockSpec((n_batch, RELAYOUT_TOK, n_chan), lambda i: (0, i, 0))],
        out_specs=pl.BlockSpec((n_grp, RELAYOUT_TOK * CHAIN_PITCH, LANES), lambda i: (0, i, 0)),
        out_shape=jax.ShapeDtypeStruct((n_grp, seq_len * CHAIN_PITCH, LANES), F32),
        scratch_shapes=[pltpu.VMEM((n_slab * CHAIN_PITCH, LANES), F32)],
        compiler_params=pltpu.CompilerParams(dimension_semantics=("arbitrary",),
                                             vmem_limit_bytes=VMEM_LIMIT),
        name="to_chains_dup" if dup else "to_chains",
    )(x2d.reshape(n_batch, seq_len, n_chan))
    return out.reshape(n_grp, seq_len, CHAIN_PITCH, LANES)


def _from_chains_kernel(yf_ref, yb_ref, o_ref, slab_ref, *, n_batch, packed):
    def stage1(vb, carry):
        for vi in range(SUBLANES):
            v = vb * SUBLANES + vi
            tf = yf_ref[pl.ds(v, RELAYOUT_TOK, stride=CHAIN_PITCH), :]
            tb = yb_ref[pl.ds(v, RELAYOUT_TOK, stride=CHAIN_PITCH), :]
            if packed:
                tb = pltpu.roll(tb, LANES // 2, 1)
            slab_ref[pl.ds(v, LANES, stride=CHAIN_PITCH), :] = (tf + tb).T
        return carry

    lax.fori_loop(0, HEAD_DIM // SUBLANES, stage1, 0)

    def stage2(b, carry):
        for jc in range(H_RWKV // 2):
            rows = []
            for h2 in range(2):
                row0 = pl.multiple_of(((2 * jc + h2) * n_batch + b) * CHAIN_PITCH, SUBLANES)
                rows.append(slab_ref[pl.ds(row0, HEAD_DIM), :])
            o_ref[b, :, LANES * jc:LANES * (jc + 1)] = jnp.concatenate(rows, axis=0).T
        return carry

    lax.fori_loop(0, n_batch, stage2, 0, unroll=RELAYOUT_BATCH_UNROLL)


def _from_chains_call(y_f, y_b, n_batch, seq_len, packed):
    n_rows = seq_len * CHAIN_PITCH
    grp_b = 0 if packed else 1
    blk = (None, RELAYOUT_TOK * CHAIN_PITCH, LANES)
    out = pl.pallas_call(
        functools.partial(_from_chains_kernel, n_batch=n_batch, packed=packed),
        grid=(seq_len // RELAYOUT_TOK,),
        in_specs=[pl.BlockSpec(blk, lambda i: (0, i, 0)), pl.BlockSpec(blk, lambda i: (grp_b, i, 0))],
        out_specs=pl.BlockSpec((n_batch, RELAYOUT_TOK, D_RWKV), lambda i: (0, i, 0)),
        out_shape=jax.ShapeDtypeStruct((n_batch, seq_len, D_RWKV), F32),
        scratch_shapes=[pltpu.VMEM((LANES * CHAIN_PITCH, LANES), F32)],
        compiler_params=pltpu.CompilerParams(dimension_semantics=("arbitrary",),
                                             vmem_limit_bytes=VMEM_LIMIT),
        name="from_chains_packed" if packed else "from_chains",
    )(y_f.reshape(-1, n_rows, LANES), y_b.reshape(-1, n_rows, LANES))
    return out.reshape(n_batch * seq_len, D_RWKV)


def _scan_kernel(*refs, packed, has_init):
    (rF, rB, kkF, kkB, kF, kB, vF, vB, wF, wB, aF, aB, ka_ref), rest = refs[:13], refs[13:]
    if has_init:
        s0_ref, rest = rest[0], rest[1:]
    yF, yB, sfin_ref, s_scr, sa_scr, op_scr, p_scr = rest
    g = pl.program_id(0)
    j = pl.program_id(1)
    tile = (HEAD_DIM, LANES)

    if packed:
        use_b = lax.broadcasted_iota(jnp.int32, tile, 1) >= LANES // 2
    else:
        use_b = jnp.full(tile, g, jnp.int32) == 1

    def pick(f_ref, b_ref, s):
        return jnp.where(use_b, b_ref[SCAN_STEPS - 1 - s, 0:HEAD_DIM, :], f_ref[s, 0:HEAD_DIM, :])

    @pl.when(j == 0)
    def _():
        if has_init:
            s_scr[...] = s0_ref[...]
        else:
            s_scr[...] = jnp.zeros(s_scr.shape, F32)

    op_scr[3] = pick(kkF, kkB, 0)
    acc = jnp.zeros(tile, F32)
    for k in range(HEAD_DIM):
        acc = acc + s_scr[k] * op_scr[3, k:k + 1, :]
    sa_scr[...] = acc

    ka = ka_ref[...]
    pad = jnp.zeros((SCAN_STEPS, CHAIN_PITCH - HEAD_DIM, LANES), F32)
    yF[:, HEAD_DIM:CHAIN_PITCH, :] = pad
    yB[:, HEAD_DIM:CHAIN_PITCH, :] = pad

    p_scr[...] = jnp.ones(tile, F32)

    def step(s, carry):
        a = pick(aF, aB, s)
        p = p_scr[...] * pick(wF, wB, s)
        p_scr[...] = p
        inv_p = 1.0 / p
        op_scr[0] = pick(kkF, kkB, s) * a * inv_p
        op_scr[1] = pick(kF, kB, s) * (1.0 + (a - 1.0) * ka) * inv_p
        op_scr[2] = pick(rF, rB, s) * p
        op_scr[3] = pick(kkF, kkB, jnp.minimum(s + 1, SCAN_STEPS - 1)) * p
        vt = pick(vF, vB, s)
        sa = sa_scr[...]

        def kblock(kb, acc):
            y, san = acc
            for ki in range(SCAN_K_UNROLL):
                k = kb * SCAN_K_UNROLL + ki
                bk = op_scr[0, pl.ds(k, 1), :]
                kdk = op_scr[1, pl.ds(k, 1), :]
                rk = op_scr[2, pl.ds(k, 1), :]
                kn = op_scr[3, pl.ds(k, 1), :]
                sn = s_scr[k] - sa * bk + vt * kdk
                s_scr[k] = sn
                y = y + sn * rk
                san = san + sn * kn
            return y, san

        zero = jnp.zeros(tile, F32)
        y, san = lax.fori_loop(0, HEAD_DIM // SCAN_K_UNROLL, kblock, (zero, zero))
        yF[s, 0:HEAD_DIM, :] = y
        yB[SCAN_STEPS - 1 - s, 0:HEAD_DIM, :] = y
        sa_scr[...] = san
        return carry

    lax.fori_loop(0, SCAN_STEPS, step, 0)

    for k in range(HEAD_DIM):
        s_scr[k] = s_scr[k] * p_scr[k:k + 1, :]

    @pl.when(j == pl.num_programs(1) - 1)
    def _():
        sfin_ref[...] = s_scr[...]


def _scan_call(shared, w, a, ka_tab, packed, s0=None):
    n_step = w.shape[1]
    n_grp = 1 if packed else 2
    n_blk = n_step // SCAN_STEPS
    has_init = s0 is not None
    blk = (None, SCAN_STEPS, CHAIN_PITCH, LANES)
    if packed:
        f_map = lambda g, j: (0, j, 0, 0)
        b_map = lambda g, j: (0, n_blk - 1 - j, 0, 0)
        fd_map, bd_map = f_map, b_map
    else:
        f_map = lambda g, j: (0, jnp.where(g == 0, j, 0), 0, 0)
        b_map = lambda g, j: (0, jnp.where(g == 1, n_blk - 1 - j, 0), 0, 0)
        fd_map = lambda g, j: (0, jnp.where(g == 0, j, 0), 0, 0)
        bd_map = lambda g, j: (1, jnp.where(g == 1, n_blk - 1 - j, 0), 0, 0)
    in_specs, args = [], []
    for x in shared:
        in_specs += [pl.BlockSpec(blk, f_map), pl.BlockSpec(blk, b_map)]
        args += [x, x]
    for x in (w, a):
        in_specs += [pl.BlockSpec(blk, fd_map), pl.BlockSpec(blk, bd_map)]
        args += [x, x]
    in_specs.append(pl.BlockSpec((HEAD_DIM, LANES), lambda g, j: (0, 0)))
    args.append(ka_tab)
    state_spec = pl.BlockSpec((None, HEAD_DIM, HEAD_DIM, LANES), lambda g, j: (g, 0, 0, 0))
    if has_init:
        in_specs.append(state_spec)
        args.append(s0)
    y_shape = jax.ShapeDtypeStruct((n_grp, n_step, CHAIN_PITCH, LANES), F32)
    return pl.pallas_call(
        functools.partial(_scan_kernel, packed=packed, has_init=has_init),
        grid=(n_grp, n_blk),
        in_specs=in_specs,
        out_specs=[pl.BlockSpec(blk, lambda g, j: (g, j, 0, 0)),
                   pl.BlockSpec(blk, lambda g, j: (g, n_blk - 1 - j, 0, 0)),
                   state_spec],
        out_shape=[y_shape, y_shape, jax.ShapeDtypeStruct((n_grp, HEAD_DIM, HEAD_DIM, LANES), F32)],
        scratch_shapes=[pltpu.VMEM((HEAD_DIM, HEAD_DIM, LANES), F32), pltpu.VMEM((HEAD_DIM, LANES), F32),
                        pltpu.VMEM((4, HEAD_DIM, LANES), F32), pltpu.VMEM((HEAD_DIM, LANES), F32)],
        compiler_params=pltpu.CompilerParams(dimension_semantics=("arbitrary", "arbitrary"),
                                             vmem_limit_bytes=VMEM_LIMIT),
        name="wkv_scan_packed" if packed else "wkv_scan",
    )(*args)


def _attn_kernel(q_ref, k_ref, v_ref, o_ref):
    q = q_ref[...]
    k = k_ref[...]
    v = v_ref[...]
    group = H_Q // H_KV
    for kvh in range(H_KV):
        kh = k[:, HEAD_DIM * kvh:HEAD_DIM * (kvh + 1)].astype(BF16)
        vh = v[:, HEAD_DIM * kvh:HEAD_DIM * (kvh + 1)].astype(BF16)
        for gq in range(group):
            hq = kvh * group + gq
            qh = (q[:, HEAD_DIM * hq:HEAD_DIM * (hq + 1)] * (HEAD_DIM ** -0.5)).astype(BF16)
            s = lax.dot_general(qh, kh, (((1,), (1,)), ((), ())), preferred_element_type=F32)
            e = jnp.exp(s - jnp.max(s, axis=-1, keepdims=True))
            o = _dot(e.astype(BF16), vh) / jnp.sum(e, axis=-1, keepdims=True)
            o_ref[:, HEAD_DIM * hq:HEAD_DIM * (hq + 1)] = o


def _attn_call(q2d, k2d, v2d, n_batch, lq, lk):
    q_tile = min(ATTN_Q_TILE, lq)
    q_tiles = lq // q_tile
    return pl.pallas_call(
        _attn_kernel,
        grid=(n_batch, q_tiles),
        in_specs=[
            pl.BlockSpec((q_tile, D_ATTN), lambda b, i: (b * q_tiles + i, 0)),
            pl.BlockSpec((lk, D_KV), lambda b, i: (b, 0)),
            pl.BlockSpec((lk, D_KV), lambda b, i: (b, 0)),
        ],
        out_specs=pl.BlockSpec((q_tile, D_ATTN), lambda b, i: (b * q_tiles + i, 0)),
        out_shape=jax.ShapeDtypeStruct((n_batch * lq, D_ATTN), F32),
        compiler_params=pltpu.CompilerParams(dimension_semantics=("arbitrary", "arbitrary"),
                                             vmem_limit_bytes=VMEM_LIMIT),
        name="gqa_attn",
    )(q2d, k2d, v2d)


def _post_kernel(x_ref, mod_ref, ys_ref, bonus_ref, g_ref, attn_ref,
                 gn_ref, wout_ref, gpost_ref, gffn_ref, x1_o, h2_o):
    head_ones = _head_ones(MXU_TILE)
    ys = ys_ref[...]
    ys_hi, ys_lo = _split_bf16(ys)
    mu = (_dot_exact_rhs(ys_hi, head_ones) + _dot_exact_rhs(ys_lo, head_ones)) * (1.0 / HEAD_DIM)
    dlt = ys - mu
    var = _dot_exact_rhs(dlt * dlt, head_ones) * (1.0 / HEAD_DIM)
    gn = gn_ref[...]
    yn = dlt * lax.rsqrt(var + GN_EPS) * gn[0:1] + gn[1:2]
    yr = (yn + bonus_ref[...]) * g_ref[...]
    mix = (_dot(yr.astype(BF16), wout_ref[0:D_RWKV, :])
           + _dot(attn_ref[...].astype(BF16), wout_ref[D_RWKV:D_MODEL, :]))
    m = mod_ref[0]
    gate1 = m[:, 2 * D_MODEL:3 * D_MODEL]
    shift2 = m[:, 3 * D_MODEL:4 * D_MODEL]
    scale2 = m[:, 4 * D_MODEL:5 * D_MODEL]
    x1 = x_ref[...] + gate1 * _rms(mix, gpost_ref[...])
    x1_o[...] = x1
    h2_o[...] = (_rms(x1, gffn_ref[...]) * (1.0 + scale2) + shift2).astype(BF16)


def _post_call(x2d, mod3d, seq_len, ys, bonus, g, attn, wts):
    n_tok = x2d.shape[0]
    tiles_per_seq = seq_len // TOK_TILE
    n_mod = mod3d.shape[0]

    def const(shape):
        return pl.BlockSpec(shape, lambda i: tuple(0 for _ in shape))

    mod_map = (lambda i: (i // tiles_per_seq, 0, 0)) if n_mod > 1 else (lambda i: (0, 0, 0))
    tok = lambda wd: pl.BlockSpec((TOK_TILE, wd), lambda i: (i, 0))
    return pl.pallas_call(
        _post_kernel,
        grid=(n_tok // TOK_TILE,),
        in_specs=[tok(D_MODEL), pl.BlockSpec((1, 1, 6 * D_MODEL), mod_map),
                  tok(512), tok(512), tok(512), tok(512),
                  const((2, D_RWKV)), const((D_MODEL, D_MODEL)), const((1, D_MODEL)), const((1, D_MODEL))],
        out_specs=[tok(D_MODEL), tok(D_MODEL)],
        out_shape=[jax.ShapeDtypeStruct((n_tok, D_MODEL), F32), jax.ShapeDtypeStruct((n_tok, D_MODEL), BF16)],
        compiler_params=pltpu.CompilerParams(dimension_semantics=("arbitrary",),
                                             vmem_limit_bytes=VMEM_LIMIT),
        name="mix_post",
    )(x2d, mod3d, ys, bonus, g, attn, wts["gn"], wts["w_out"], wts["g_post"], wts["g_ffn_pre"])


def _ffn_kernel(h2_ref, x1_ref, mod_ref, wg_ref, wv_ref, cwg_ref, cwv_ref, cbg_ref, cbv_ref,
                wd_ref, gpost_ref, o_ref, acc_ref, *, seq_len):
    j = pl.program_id(1)

    @pl.when(j == 0)
    def _():
        acc_ref[...] = jnp.zeros(acc_ref.shape, F32)

    h2 = h2_ref[...]
    n_rows = h2.shape[0]
    assert seq_len & (seq_len - 1) == 0
    row = lax.broadcasted_iota(jnp.int32, (n_rows, FFN_SUB_TILE), 0) & (seq_len - 1)
    has_prev = row != 0
    has_next = row != seq_len - 1

    def conv(u, cw, cb):
        prev = jnp.where(has_prev, pltpu.roll(u, 1, 0), 0.0)
        nxt = jnp.where(has_next, pltpu.roll(u, n_rows - 1, 0), 0.0)
        return ((cb + prev * cw[0:1]) + u * cw[1:2]) + nxt * cw[2:3]

    down = None
    for c0 in range(0, FFN_COL_TILE, FFN_SUB_TILE):
        cols = slice(c0, c0 + FFN_SUB_TILE)
        gate = conv(_dot(h2, wg_ref[:, cols]), cwg_ref[:, cols], cbg_ref[:, cols])
        val = conv(_dot(h2, wv_ref[:, cols]), cwv_ref[:, cols], cbv_ref[:, cols])
        act = (gate * _sigmoid(gate) * val).astype(BF16)
        part = _dot(act, wd_ref[cols, :])
        down = part if down is None else down + part
    acc_ref[...] += down

    @pl.when(j == pl.num_programs(1) - 1)
    def _():
        m = mod_ref[0]
        gate2 = m[:, 5 * D_MODEL:6 * D_MODEL]
        o_ref[...] = x1_ref[...] + gate2 * _rms(acc_ref[...], gpost_ref[...])


def _ffn_call(h2, x1, mod3d, seq_len, wts):
    n_tok = h2.shape[0]
    tiles_per_seq = max(seq_len // FFN_TOK_TILE, 1)
    n_mod = mod3d.shape[0]
    n_col = D_FF_PAD // FFN_COL_TILE
    mod_map = (lambda i, j: (i // tiles_per_seq, 0, 0)) if n_mod > 1 else (lambda i, j: (0, 0, 0))
    tok = lambda: pl.BlockSpec((FFN_TOK_TILE, D_MODEL), lambda i, j: (i, 0))
    return pl.pallas_call(
        functools.partial(_ffn_kernel, seq_len=seq_len),
        grid=(n_tok // FFN_TOK_TILE, n_col),
        in_specs=[
            tok(), tok(), pl.BlockSpec((1, 1, 6 * D_MODEL), mod_map),
            pl.BlockSpec((D_MODEL, FFN_COL_TILE), lambda i, j: (0, j)),
            pl.BlockSpec((D_MODEL, FFN_COL_TILE), lambda i, j: (0, n_col + j)),
            pl.BlockSpec((3, FFN_COL_TILE), lambda i, j: (0, j)),
            pl.BlockSpec((3, FFN_COL_TILE), lambda i, j: (0, n_col + j)),
            pl.BlockSpec((1, FFN_COL_TILE), lambda i, j: (0, j)),
            pl.BlockSpec((1, FFN_COL_TILE), lambda i, j: (0, n_col + j)),
            pl.BlockSpec((FFN_COL_TILE, D_MODEL), lambda i, j: (j, 0)),
            pl.BlockSpec((1, D_MODEL), lambda i, j: (0, 0)),
        ],
        out_specs=tok(),
        out_shape=jax.ShapeDtypeStruct((n_tok, D_MODEL), F32),
        scratch_shapes=[pltpu.VMEM((FFN_TOK_TILE, D_MODEL), F32)],
        compiler_params=pltpu.CompilerParams(dimension_semantics=("arbitrary", "arbitrary"),
                                             vmem_limit_bytes=VMEM_LIMIT),
        name="conv_ffn",
    )(h2, x1, mod3d, wts["ffn_up"], wts["ffn_up"], wts["conv_w"], wts["conv_w"],
      wts["conv_b"], wts["conv_b"], wts["ffn_down"], wts["g_ffn_post"])


def _rope_tables(n_tok, n_heads):
    quarter = HEAD_DIM // 4
    pos = jnp.arange(n_tok)
    row = (pos // GRID_W).astype(F32)
    col = (pos % GRID_W).astype(F32)
    inv_freq = ROPE_THETA ** (-jnp.arange(quarter, dtype=F32) / quarter)

    def half(p):
        ang = p[:, None] * inv_freq[None, :]
        c, s = jnp.cos(ang), jnp.sin(ang)
        return jnp.concatenate([c, c], axis=-1), jnp.concatenate([-s, s], axis=-1)

    cr, sr = half(row)
    cc, sc = half(col)
    cos = jnp.concatenate([cr, cc], axis=-1)
    sin = jnp.concatenate([sr, sc], axis=-1)
    return jnp.tile(cos, (1, n_heads)), jnp.tile(sin, (1, n_heads))


def _pad_ff_cols(x):
    pad = ((0, 0), (0, D_FF_PAD - D_FF))
    return jnp.concatenate([jnp.pad(x[:, :D_FF], pad), jnp.pad(x[:, D_FF:], pad)], axis=1)


def _mixer(pre, n_batch, seq_len, packed, k_a, s0=None):
    r, v, kk, k, w, a = pre[:6]
    shared = tuple(_to_chains_call(x, n_batch, seq_len, packed) for x in (r, kk, k, v))
    ka_tab = jnp.repeat(k_a.reshape(H_RWKV, HEAD_DIM).T, n_batch, axis=1)
    ka_tab = jnp.tile(ka_tab, (1, LANES // ka_tab.shape[1]))
    y_f, y_b, s_fin = _scan_call(shared, _to_chains_call(w, n_batch, seq_len, False),
                                 _to_chains_call(a, n_batch, seq_len, False), ka_tab, packed, s0)
    return _from_chains_call(y_f, y_b, n_batch, seq_len, packed), s_fin


def kernel(x_prompt, x_sample, cache_k, cache_v, state_rwkv, c, c_ctx, w_mod, b_mod, norm_mix_pre, norm_mix_post, norm_ffn_pre, norm_ffn_post, w_in, w0, w_up, a0, a_up, g_up, k_k, k_a, r_k, gn_w, gn_b, q_norm, k_norm, w_out, ffn_up, conv_w, conv_b, ffn_down):
    n_ctx, l_ctx = x_prompt.shape[0], x_prompt.shape[1]
    n_lat, l_lat = x_sample.shape[0], x_sample.shape[1]
    l_past = cache_k.shape[2]
    layer = 0
    assert n_ctx * H_RWKV == LANES and 2 * n_lat * H_RWKV == LANES

    zeros_w = jnp.zeros((LORA_W, D_RWKV), F32)
    w_lora = jnp.concatenate([
        jnp.concatenate([w_up[layer, 0], w_up[layer, 1], zeros_w, zeros_w], axis=1),
        jnp.concatenate([zeros_w, zeros_w, a_up[layer, 0], a_up[layer, 1]], axis=1)], axis=0)
    vec = jnp.stack([k_k[layer], k_a[layer], r_k[layer].reshape(D_RWKV), jnp.tile(q_norm[layer], H_Q)])
    wts = dict(
        g_pre=norm_mix_pre[layer].reshape(1, D_MODEL),
        g_post=norm_mix_post[layer].reshape(1, D_MODEL),
        g_ffn_pre=norm_ffn_pre[layer].reshape(1, D_MODEL),
        g_ffn_post=norm_ffn_post[layer].reshape(1, D_MODEL),
        w_in=w_in[layer].astype(BF16),
        w_lora=w_lora.astype(BF16),
        g_up=g_up[layer].astype(BF16),
        vec=vec,
        wa0=jnp.stack([w0[layer].reshape(2 * D_RWKV), a0[layer].reshape(2 * D_RWKV)]),
        k_norm=jnp.tile(k_norm[layer], H_KV).reshape(1, D_KV),
        gn=jnp.stack([gn_w[layer], gn_b[layer]]),
        w_out=w_out[layer].astype(BF16),
        ffn_up=_pad_ff_cols(ffn_up[layer].astype(BF16)),
        conv_w=_pad_ff_cols(conv_w[layer]),
        conv_b=_pad_ff_cols(conv_b[layer].reshape(1, 2 * D_FF)),
        ffn_down=jnp.pad(ffn_down[layer].astype(BF16), ((0, D_FF_PAD - D_FF), (0, 0))),
    )

    cvec = jnp.concatenate([c, c_ctx[None, :], jnp.zeros((16 - n_lat - 1, D_MODEL), F32)], axis=0)
    mod = _mod_call(cvec, w_mod[layer], b_mod[layer])
    mod_lat = mod[:n_lat].reshape(n_lat, 1, 6 * D_MODEL)
    mod_ctx = mod[n_lat:n_lat + 1].reshape(1, 1, 6 * D_MODEL)

    xc = x_prompt.reshape(n_ctx * l_ctx, D_MODEL)
    pre_c = _pre_call(xc, mod_ctx, l_ctx, wts, None)
    g_c, bonus_c, q_c, ka_c, va_c = pre_c[6:]
    ys_c, s_c = _mixer(pre_c, n_ctx, l_ctx, False, k_a[layer])
    attn_c = _attn_call(q_c, ka_c, va_c, n_ctx, l_ctx, l_ctx)
    x1_c, h2_c = _post_call(xc, mod_ctx, l_ctx, ys_c, bonus_c, g_c, attn_c, wts)
    out_c = _ffn_call(h2_c, x1_c, mod_ctx, l_ctx, wts)

    xl = x_sample.reshape(n_lat * l_lat, D_MODEL)
    tabs = _rope_tables(l_lat, H_Q) + _rope_tables(l_lat, H_KV)
    pre_l = _pre_call(xl, mod_lat, l_lat, wts, tabs)
    g_l, bonus_l, q_l, kr_l, va_l = pre_l[6:]
    s0 = state_rwkv[:, layer].transpose(4, 3, 1, 2, 0).reshape(1, HEAD_DIM, HEAD_DIM, LANES)
    ys_l, _ = _mixer(pre_l, n_lat, l_lat, True, k_a[layer], s0)
    k_all = jnp.concatenate([kr_l.reshape(n_lat, l_lat, D_KV), cache_k[:, layer].reshape(n_lat, l_past, D_KV)],
                            axis=1).reshape(n_lat * (l_lat + l_past), D_KV)
    v_all = jnp.concatenate([va_l.reshape(n_lat, l_lat, D_KV), cache_v[:, layer].reshape(n_lat, l_past, D_KV)],
                            axis=1).reshape(n_lat * (l_lat + l_past), D_KV)
    attn_l = _attn_call(q_l, k_all, v_all, n_lat, l_lat, l_lat + l_past)
    x1_l, h2_l = _post_call(xl, mod_lat, l_lat, ys_l, bonus_l, g_l, attn_l, wts)
    out_l = _ffn_call(h2_l, x1_l, mod_lat, l_lat, wts)

    y_prompt = out_c.reshape(n_ctx, l_ctx, D_MODEL)
    y_sample = out_l.reshape(n_lat, l_lat, D_MODEL)
    new_cache_k = ka_c.reshape(n_ctx, 1, l_ctx, H_KV, HEAD_DIM)
    new_cache_v = va_c.reshape(n_ctx, 1, l_ctx, H_KV, HEAD_DIM)
    new_state = s_c.reshape(2, HEAD_DIM, HEAD_DIM, H_RWKV, n_ctx).transpose(4, 0, 3, 2, 1)[:, None]
    return (y_prompt, y_sample, new_cache_k, new_cache_v, new_state)
```

```python
import functools

import jax
import jax.numpy as jnp
from jax import lax
from jax.experimental import pallas as pl
from jax.experimental.pallas import tpu as pltpu

F32 = jnp.float32
BF16 = jnp.bfloat16

D_MODEL = 1024
HEAD_DIM = 64
D_RWKV = 512
H_RWKV = 8
D_ATTN = 512
H_Q = 8
H_KV = 2
D_KV = 128
LORA_W = 64
LORA_A = 64
LORA_G = 128
D_FF = 2816
D_IN = 2560
GRID_W = 64
ROPE_THETA = 10000.0
RMS_EPS = 1e-6
GN_EPS = 64e-5

LANES = 128
MXU_TILE = 256
SUBLANES = 8
TOK_TILE = 512
FFN_TOK_TILE = 1024
FFN_COL_TILE = 512
FFN_SUB_TILE = 256
D_FF_PAD = -(-D_FF // FFN_COL_TILE) * FFN_COL_TILE
SCAN_STEPS = 32
SCAN_K_UNROLL = 16
RELAYOUT_TOK = 128
RELAYOUT_BATCH_UNROLL = 4
CHAIN_PITCH = 72
ATTN_Q_TILE = 512
MOD_COL_TILE = 768
VMEM_LIMIT = 56 * 1024 * 1024


def _dot(a, b):
    return jnp.dot(a, b, preferred_element_type=F32)


def _split_bf16(x):
    hi = x.astype(BF16)
    lo = (x - hi.astype(F32)).astype(BF16)
    return hi, lo


def _dot_exact_rhs(x, w_bf16):
    xb = x.astype(BF16)
    wd = w_bf16.shape[0]
    parts = [_dot(xb[:, c:c + wd], w_bf16) for c in range(0, x.shape[-1], wd)]
    return parts[0] if len(parts) == 1 else jnp.concatenate(parts, axis=1)


def _dot3(a, b):
    ah, al = _split_bf16(a)
    bh, bl = _split_bf16(b)
    return _dot(ah, bh) + (_dot(ah, bl) + _dot(al, bh))


def _head_ones(n):
    shift = HEAD_DIM.bit_length() - 1
    r = lax.shift_right_logical(lax.broadcasted_iota(jnp.int32, (n, n), 0), shift)
    c = lax.shift_right_logical(lax.broadcasted_iota(jnp.int32, (n, n), 1), shift)
    return jnp.where(r == c, 1.0, 0.0).astype(BF16)


def _sigmoid(x):
    return 1.0 / (1.0 + jnp.exp(-x))


def _softplus(x):
    return jnp.maximum(x, 0.0) + jnp.log(1.0 + jnp.exp(-jnp.abs(x)))


def _rms(x, g):
    return x * lax.rsqrt(jnp.mean(x * x, axis=-1, keepdims=True) + RMS_EPS) * g


def _rope(x, cos, sgn_sin):
    n = x.shape[-1]
    lane = lax.broadcasted_iota(jnp.int32, x.shape, 1)
    partner = jnp.where((lane & 16) == 0, pltpu.roll(x, n - 16, 1), pltpu.roll(x, 16, 1))
    return x * cos + partner * sgn_sin


def _mod_kernel(c_ref, w_ref, b_ref, o_ref):
    c = c_ref[...]
    o_ref[...] = _dot3(c * _sigmoid(c), w_ref[...]) + b_ref[...]


def _mod_call(cvec, w_mod, b_mod):
    n = w_mod.shape[1]
    rows = cvec.shape[0]
    return pl.pallas_call(
        _mod_kernel,
        grid=(n // MOD_COL_TILE,),
        in_specs=[
            pl.BlockSpec((rows, D_MODEL), lambda j: (0, 0)),
            pl.BlockSpec((D_MODEL, MOD_COL_TILE), lambda j: (0, j)),
            pl.BlockSpec((1, MOD_COL_TILE), lambda j: (0, j)),
        ],
        out_specs=pl.BlockSpec((rows, MOD_COL_TILE), lambda j: (0, j)),
        out_shape=jax.ShapeDtypeStruct((rows, n), F32),
        compiler_params=pltpu.CompilerParams(dimension_semantics=("arbitrary",)),
        name="adaln_mod",
    )(cvec, w_mod, b_mod.reshape(1, n))


_V_KK, _V_KA, _V_RK, _V_QN = range(4)


def _pre_kernel(*refs, rope):
    if rope:
        (x_ref, mod_ref, gpre_ref, win_ref, wlora_ref, gup_ref, vec_ref, wa0_ref, kn_ref,
         cq_ref, sq_ref, ck_ref, sk_ref,
         sh_o, w_o, a_o, g_o, bonus_o, q_o, ka_o, va_o) = refs
    else:
        (x_ref, mod_ref, gpre_ref, win_ref, wlora_ref, gup_ref, vec_ref, wa0_ref, kn_ref,
         sh_o, w_o, a_o, g_o, bonus_o, q_o, ka_o, va_o) = refs

    x = x_ref[...]
    m = mod_ref[0]
    shift1 = m[:, 0:D_MODEL]
    scale1 = m[:, D_MODEL:2 * D_MODEL]
    h = _rms(x, gpre_ref[...]) * (1.0 + scale1) + shift1
    p = _dot(h.astype(BF16), win_ref[...])

    r = p[:, 0:512]
    k = p[:, 512:1024]
    v = p[:, 1024:1536]
    xwa = p[:, 1536:1664]
    xg = p[:, 1664:1792]
    q = p[:, 1792:2304]
    ka = p[:, 2304:2432]
    va = p[:, 2432:2560]

    vec = vec_ref[...]
    head_ones = _head_ones(MXU_TILE)

    g_o[...] = _dot(_sigmoid(xg).astype(BF16), gup_ref[...])

    kx = k * vec[_V_KK:_V_KK + 1]
    sh_o[:, D_RWKV:2 * D_RWKV] = kx * lax.rsqrt(_dot_exact_rhs(kx * kx, head_ones) + 1e-12)

    lane = lax.broadcasted_iota(jnp.int32, xwa.shape, 1)
    lora_in = jnp.where(lane < LORA_W, jnp.tanh(xwa), xwa)
    lora = _dot(lora_in.astype(BF16), wlora_ref[...])

    z = wa0_ref[0:1] + lora[:, 0:2 * D_RWKV]
    logw = -_softplus(-z) - 0.5
    w_o[...] = jnp.exp(-jnp.exp(logw))
    a = _sigmoid(wa0_ref[1:2] + lora[:, 2 * D_RWKV:4 * D_RWKV])
    a_o[...] = a

    k_a = vec[_V_KA:_V_KA + 1]
    kd_sum = k * (1.0 + (a[:, 0:D_RWKV] - 1.0) * k_a) + k * (1.0 + (a[:, D_RWKV:2 * D_RWKV] - 1.0) * k_a)
    bonus_o[...] = _dot_exact_rhs(r * kd_sum * vec[_V_RK:_V_RK + 1], head_ones) * v

    sh_o[:, 0:D_RWKV] = r
    sh_o[:, 2 * D_RWKV:3 * D_RWKV] = k
    sh_o[:, 3 * D_RWKV:4 * D_RWKV] = v

    qn = q * lax.rsqrt(_dot_exact_rhs(q * q, head_ones) * (1.0 / HEAD_DIM) + RMS_EPS) * vec[_V_QN:_V_QN + 1]
    kan = ka * lax.rsqrt(_dot_exact_rhs(ka * ka, _head_ones(D_KV)) * (1.0 / HEAD_DIM) + RMS_EPS) * kn_ref[...]
    if rope:
        qn = _rope(qn, cq_ref[...], sq_ref[...])
        kan = _rope(kan, ck_ref[...], sk_ref[...])
    q_o[...] = qn
    ka_o[...] = kan
    va_o[...] = va


def _pre_call(x2d, mod3d, seq_len, wts, rope_tabs):
    n_tok = x2d.shape[0]
    tiles_per_seq = seq_len // TOK_TILE
    n_mod = mod3d.shape[0]
    rope = rope_tabs is not None

    def const(shape):
        return pl.BlockSpec(shape, lambda i: tuple(0 for _ in shape))

    mod_map = (lambda i: (i // tiles_per_seq, 0, 0)) if n_mod > 1 else (lambda i: (0, 0, 0))
    in_specs = [
        pl.BlockSpec((TOK_TILE, D_MODEL), lambda i: (i, 0)),
        pl.BlockSpec((1, 1, 6 * D_MODEL), mod_map),
        const((1, D_MODEL)),
        const((D_MODEL, D_IN)),
        const((LORA_W + LORA_A, 4 * D_RWKV)),
        const((LORA_G, D_RWKV)),
        const((4, D_RWKV)),
        const((2, 2 * D_RWKV)),
        const((1, D_KV)),
    ]
    args = [x2d, mod3d, wts["g_pre"], wts["w_in"], wts["w_lora"], wts["g_up"], wts["vec"], wts["wa0"],
            wts["k_norm"]]
    if rope:
        tab_map = lambda i: (i % tiles_per_seq, 0)
        in_specs += [pl.BlockSpec((TOK_TILE, D_ATTN), tab_map), pl.BlockSpec((TOK_TILE, D_ATTN), tab_map),
                     pl.BlockSpec((TOK_TILE, D_KV), tab_map), pl.BlockSpec((TOK_TILE, D_KV), tab_map)]
        args += list(rope_tabs)

    widths = [4 * D_RWKV, 1024, 1024] + [512] * 3 + [D_KV, D_KV]
    out_specs = [pl.BlockSpec((TOK_TILE, wd), lambda i: (i, 0)) for wd in widths]
    out_shape = [jax.ShapeDtypeStruct((n_tok, wd), F32) for wd in widths]
    return pl.pallas_call(
        functools.partial(_pre_kernel, rope=rope),
        grid=(n_tok // TOK_TILE,),
        in_specs=in_specs,
        out_specs=out_specs,
        out_shape=out_shape,
        compiler_params=pltpu.CompilerParams(dimension_semantics=("arbitrary",),
                                             vmem_limit_bytes=VMEM_LIMIT),
        name="pre_rope" if rope else "pre",
    )(*args)


def _to_chains_kernel(x_ref, o_ref, slab_ref, *, n_batch, dup):
    n_col = x_ref.shape[2] // HEAD_DIM
    n_grp = o_ref.shape[0]

    def stage1(b, carry):
        for jc in range(n_col // 2):
            a_t = x_ref[b, :, LANES * jc:LANES * (jc + 1)].T
            for h2 in range(2):
                row0 = pl.multiple_of(((2 * jc + h2) * n_batch + b) * CHAIN_PITCH, SUBLANES)
                slab_ref[pl.ds(row0, HEAD_DIM), :] = a_t[HEAD_DIM * h2:HEAD_DIM * (h2 + 1), :]
        return carry

    lax.fori_loop(0, n_batch, stage1, 0, unroll=RELAYOUT_BATCH_UNROLL)

    def stage2(kb, carry):
        for ki in range(SUBLANES):
            k = kb * SUBLANES + ki
            for d in range(n_grp):
                if dup:
                    half = slab_ref[pl.ds(k, LANES // 2, stride=CHAIN_PITCH), :]
                    tile = jnp.concatenate([half, half], axis=0)
                else:
                    tile = slab_ref[pl.ds(d * LANES * CHAIN_PITCH + k, LANES, stride=CHAIN_PITCH), :]
                o_ref.at[d][pl.ds(k, RELAYOUT_TOK, stride=CHAIN_PITCH), :] = tile.T
        return carry

    lax.fori_loop(0, HEAD_DIM // SUBLANES, stage2, 0)

    zero = jnp.zeros((RELAYOUT_TOK, LANES), F32)
    for d in range(n_grp):
        for pad in range(HEAD_DIM, CHAIN_PITCH):
            o_ref.at[d][pl.ds(pad, RELAYOUT_TOK, stride=CHAIN_PITCH), :] = zero


def _to_chains_call(x2d, n_batch, seq_len, dup, n_split=1):
    n_chan = x2d.shape[1] // n_split
    n_slab = (n_chan // HEAD_DIM) * n_batch
    n_grp = n_slab * (2 if dup else 1) // LANES
    out = pl.pallas_call(
        functools.partial(_to_chains_kernel, n_batch=n_batch, dup=dup),
        grid=(seq_len // RELAYOUT_TOK, n_split),
        in_specs=[pl.BlockSpec((n_batch, RELAYOUT_TOK, n_chan), lambda i, c: (0, i, c))],
        out_specs=pl.BlockSpec((n_grp, RELAYOUT_TOK * CHAIN_PITCH, LANES), lambda i, c: (c, i, 0)),
        out_shape=jax.ShapeDtypeStruct((n_grp * n_split, seq_len * CHAIN_PITCH, LANES), F32),
        scratch_shapes=[pltpu.VMEM((n_slab * CHAIN_PITCH, LANES), F32)],
        compiler_params=pltpu.CompilerParams(dimension_semantics=("arbitrary", "arbitrary"),
                                             vmem_limit_bytes=VMEM_LIMIT),
        name="to_chains_dup" if dup else "to_chains",
    )(x2d.reshape(n_batch, seq_len, n_chan * n_split))
    return out.reshape(n_grp * n_split, seq_len, CHAIN_PITCH, LANES)


def _from_chains_kernel(yf_ref, yb_ref, o_ref, slab_ref, *, n_batch, packed):
    def stage1(vb, carry):
        for vi in range(SUBLANES):
            v = vb * SUBLANES + vi
            tf = yf_ref[pl.ds(v, RELAYOUT_TOK, stride=CHAIN_PITCH), :]
            tb = yb_ref[pl.ds(v, RELAYOUT_TOK, stride=CHAIN_PITCH), :]
            if packed:
                tb = pltpu.roll(tb, LANES // 2, 1)
            slab_ref[pl.ds(v, LANES, stride=CHAIN_PITCH), :] = (tf + tb).T
        return carry

    lax.fori_loop(0, HEAD_DIM // SUBLANES, stage1, 0)

    def stage2(b, carry):
        for jc in range(H_RWKV // 2):
            rows = []
            for h2 in range(2):
                row0 = pl.multiple_of(((2 * jc + h2) * n_batch + b) * CHAIN_PITCH, SUBLANES)
                rows.append(slab_ref[pl.ds(row0, HEAD_DIM), :])
            o_ref[b, :, LANES * jc:LANES * (jc + 1)] = jnp.concatenate(rows, axis=0).T
        return carry

    lax.fori_loop(0, n_batch, stage2, 0, unroll=RELAYOUT_BATCH_UNROLL)


def _from_chains_call(y_f, y_b, n_batch, seq_len, packed):
    n_rows = seq_len * CHAIN_PITCH
    grp_b = 0 if packed else 1
    blk = (None, RELAYOUT_TOK * CHAIN_PITCH, LANES)
    out = pl.pallas_call(
        functools.partial(_from_chains_kernel, n_batch=n_batch, packed=packed),
        grid=(seq_len // RELAYOUT_TOK,),
        in_specs=[pl.BlockSpec(blk, lambda i: (0, i, 0)), pl.BlockSpec(blk, lambda i: (grp_b, i, 0))],
        out_specs=pl.BlockSpec((n_batch, RELAYOUT_TOK, D_RWKV), lambda i: (0, i, 0)),
        out_shape=jax.ShapeDtypeStruct((n_batch, seq_len, D_RWKV), F32),
        scratch_shapes=[pltpu.VMEM((LANES * CHAIN_PITCH, LANES), F32)],
        compiler_params=pltpu.CompilerParams(dimension_semantics=("arbitrary",),
                                             vmem_limit_bytes=VMEM_LIMIT),
        name="from_chains_packed" if packed else "from_chains",
    )(y_f.reshape(-1, n_rows, LANES), y_b.reshape(-1, n_rows, LANES))
    return out.reshape(n_batch * seq_len, D_RWKV)


def _scan_kernel(*refs, packed, has_init):
    (rF, rB, kkF, kkB, kF, kB, vF, vB, wF, wB, aF, aB, ka_ref), rest = refs[:13], refs[13:]
    if has_init:
        s0_ref, rest = rest[0], rest[1:]
    yF, yB, sfin_ref, s_scr, sa_scr, op_scr, p_scr = rest
    g = pl.program_id(0)
    j = pl.program_id(1)
    tile = (HEAD_DIM, LANES)

    if packed:
        use_b = lax.broadcasted_iota(jnp.int32, tile, 1) >= LANES // 2
    else:
        use_b = jnp.full(tile, g, jnp.int32) == 1

    def pick(f_ref, b_ref, s):
        return jnp.where(use_b, b_ref[SCAN_STEPS - 1 - s, 0:HEAD_DIM, :], f_ref[s, 0:HEAD_DIM, :])

    @pl.when(j == 0)
    def _():
        if has_init:
            s_scr[...] = s0_ref[...]
        else:
            s_scr[...] = jnp.zeros(s_scr.shape, F32)

    op_scr[3] = pick(kkF, kkB, 0)
    acc = jnp.zeros(tile, F32)
    for k in range(HEAD_DIM):
        acc = acc + s_scr[k] * op_scr[3, k:k + 1, :]
    sa_scr[...] = acc

    ka = ka_ref[...]
    pad = jnp.zeros((SCAN_STEPS, CHAIN_PITCH - HEAD_DIM, LANES), F32)
    yF[:, HEAD_DIM:CHAIN_PITCH, :] = pad
    yB[:, HEAD_DIM:CHAIN_PITCH, :] = pad

    p_scr[...] = jnp.ones(tile, F32)

    def step(s, carry):
        a = pick(aF, aB, s)
        p = p_scr[...] * pick(wF, wB, s)
        p_scr[...] = p
        inv_p = 1.0 / p
        op_scr[0] = pick(kkF, kkB, s) * a * inv_p
        op_scr[1] = pick(kF, kB, s) * (1.0 + (a - 1.0) * ka) * inv_p
        op_scr[2] = pick(rF, rB, s) * p
        op_scr[3] = pick(kkF, kkB, jnp.minimum(s + 1, SCAN_STEPS - 1)) * p
        vt = pick(vF, vB, s)
        sa = sa_scr[...]

        def kblock(kb, acc):
            y, san = acc
            for ki in range(SCAN_K_UNROLL):
                k = kb * SCAN_K_UNROLL + ki
                bk = op_scr[0, pl.ds(k, 1), :]
                kdk = op_scr[1, pl.ds(k, 1), :]
                rk = op_scr[2, pl.ds(k, 1), :]
                kn = op_scr[3, pl.ds(k, 1), :]
                sn = s_scr[k] - sa * bk + vt * kdk
                s_scr[k] = sn
                y = y + sn * rk
                san = san + sn * kn
            return y, san

        zero = jnp.zeros(tile, F32)
        y, san = lax.fori_loop(0, HEAD_DIM // SCAN_K_UNROLL, kblock, (zero, zero))
        yF[s, 0:HEAD_DIM, :] = y
        yB[SCAN_STEPS - 1 - s, 0:HEAD_DIM, :] = y
        sa_scr[...] = san
        return carry

    lax.fori_loop(0, SCAN_STEPS, step, 0)

    for k in range(HEAD_DIM):
        s_scr[k] = s_scr[k] * p_scr[k:k + 1, :]

    @pl.when(j == pl.num_programs(1) - 1)
    def _():
        sfin_ref[...] = s_scr[...]


def _scan_call(shared, w, a, ka_tab, packed, s0=None):
    n_step = w.shape[1]
    n_grp = 1 if packed else 2
    n_blk = n_step // SCAN_STEPS
    has_init = s0 is not None
    blk = (None, SCAN_STEPS, CHAIN_PITCH, LANES)
    if packed:
        f_map = lambda g, j: (0, j, 0, 0)
        b_map = lambda g, j: (0, n_blk - 1 - j, 0, 0)
        fd_map, bd_map = f_map, b_map
    else:
        f_map = lambda g, j: (0, jnp.where(g == 0, j, 0), 0, 0)
        b_map = lambda g, j: (0, jnp.where(g == 1, n_blk - 1 - j, 0), 0, 0)
        fd_map = lambda g, j: (0, jnp.where(g == 0, j, 0), 0, 0)
        bd_map = lambda g, j: (1, jnp.where(g == 1, n_blk - 1 - j, 0), 0, 0)
    in_specs, args = [], []
    for op in range(shared.shape[0]):
        in_specs += [pl.BlockSpec(blk, lambda g, j, op=op: (op,) + f_map(g, j)[1:]),
                     pl.BlockSpec(blk, lambda g, j, op=op: (op,) + b_map(g, j)[1:])]
        args += [shared, shared]
    for x in (w, a):
        in_specs += [pl.BlockSpec(blk, fd_map), pl.BlockSpec(blk, bd_map)]
        args += [x, x]
    in_specs.append(pl.BlockSpec((HEAD_DIM, LANES), lambda g, j: (0, 0)))
    args.append(ka_tab)
    state_spec = pl.BlockSpec((None, HEAD_DIM, HEAD_DIM, LANES), lambda g, j: (g, 0, 0, 0))
    if has_init:
        in_specs.append(state_spec)
        args.append(s0)
    y_shape = jax.ShapeDtypeStruct((n_grp, n_step, CHAIN_PITCH, LANES), F32)
    return pl.pallas_call(
        functools.partial(_scan_kernel, packed=packed, has_init=has_init),
        grid=(n_grp, n_blk),
        in_specs=in_specs,
        out_specs=[pl.BlockSpec(blk, lambda g, j: (g, j, 0, 0)),
                   pl.BlockSpec(blk, lambda g, j: (g, n_blk - 1 - j, 0, 0)),
                   state_spec],
        out_shape=[y_shape, y_shape, jax.ShapeDtypeStruct((n_grp, HEAD_DIM, HEAD_DIM, LANES), F32)],
        scratch_shapes=[pltpu.VMEM((HEAD_DIM, HEAD_DIM, LANES), F32), pltpu.VMEM((HEAD_DIM, LANES), F32),
                        pltpu.VMEM((4, HEAD_DIM, LANES), F32), pltpu.VMEM((HEAD_DIM, LANES), F32)],
        compiler_params=pltpu.CompilerParams(dimension_semantics=("arbitrary", "arbitrary"),
                                             vmem_limit_bytes=VMEM_LIMIT),
        name="wkv_scan_packed" if packed else "wkv_scan",
    )(*args)


def _attn_kernel(q_ref, k_ref, v_ref, o_ref):
    q = q_ref[...]
    k = k_ref[...]
    v = v_ref[...]
    group = H_Q // H_KV
    for kvh in range(H_KV):
        kh = k[:, HEAD_DIM * kvh:HEAD_DIM * (kvh + 1)].astype(BF16)
        vh = v[:, HEAD_DIM * kvh:HEAD_DIM * (kvh + 1)].astype(BF16)
        for gq in range(group):
            hq = kvh * group + gq
            qh = (q[:, HEAD_DIM * hq:HEAD_DIM * (hq + 1)] * (HEAD_DIM ** -0.5)).astype(BF16)
            s = lax.dot_general(qh, kh, (((1,), (1,)), ((), ())), preferred_element_type=F32)
            e = jnp.exp(s - jnp.max(s, axis=-1, keepdims=True))
            o = _dot(e.astype(BF16), vh) / jnp.sum(e, axis=-1, keepdims=True)
            o_ref[:, HEAD_DIM * hq:HEAD_DIM * (hq + 1)] = o


def _attn_call(q2d, k2d, v2d, n_batch, lq, lk):
    q_tile = min(ATTN_Q_TILE, lq)
    q_tiles = lq // q_tile
    return pl.pallas_call(
        _attn_kernel,
        grid=(n_batch, q_tiles),
        in_specs=[
            pl.BlockSpec((q_tile, D_ATTN), lambda b, i: (b * q_tiles + i, 0)),
            pl.BlockSpec((lk, D_KV), lambda b, i: (b, 0)),
            pl.BlockSpec((lk, D_KV), lambda b, i: (b, 0)),
        ],
        out_specs=pl.BlockSpec((q_tile, D_ATTN), lambda b, i: (b * q_tiles + i, 0)),
        out_shape=jax.ShapeDtypeStruct((n_batch * lq, D_ATTN), F32),
        compiler_params=pltpu.CompilerParams(dimension_semantics=("arbitrary", "arbitrary"),
                                             vmem_limit_bytes=VMEM_LIMIT),
        name="gqa_attn",
    )(q2d, k2d, v2d)


def _post_kernel(x_ref, mod_ref, ys_ref, bonus_ref, g_ref, attn_ref,
                 gn_ref, wout_ref, gpost_ref, gffn_ref, x1_o, h2_o):
    head_ones = _head_ones(MXU_TILE)
    ys = ys_ref[...]
    ys_hi, ys_lo = _split_bf16(ys)
    mu = (_dot_exact_rhs(ys_hi, head_ones) + _dot_exact_rhs(ys_lo, head_ones)) * (1.0 / HEAD_DIM)
    dlt = ys - mu
    var = _dot_exact_rhs(dlt * dlt, head_ones) * (1.0 / HEAD_DIM)
    gn = gn_ref[...]
    yn = dlt * lax.rsqrt(var + GN_EPS) * gn[0:1] + gn[1:2]
    yr = (yn + bonus_ref[...]) * g_ref[...]
    mix = (_dot(yr.astype(BF16), wout_ref[0:D_RWKV, :])
           + _dot(attn_ref[...].astype(BF16), wout_ref[D_RWKV:D_MODEL, :]))
    m = mod_ref[0]
    gate1 = m[:, 2 * D_MODEL:3 * D_MODEL]
    shift2 = m[:, 3 * D_MODEL:4 * D_MODEL]
    scale2 = m[:, 4 * D_MODEL:5 * D_MODEL]
    x1 = x_ref[...] + gate1 * _rms(mix, gpost_ref[...])
    x1_o[...] = x1
    h2_o[...] = (_rms(x1, gffn_ref[...]) * (1.0 + scale2) + shift2).astype(BF16)


def _post_call(x2d, mod3d, seq_len, ys, bonus, g, attn, wts):
    n_tok = x2d.shape[0]
    tiles_per_seq = seq_len // TOK_TILE
    n_mod = mod3d.shape[0]

    def const(shape):
        return pl.BlockSpec(shape, lambda i: tuple(0 for _ in shape))

    mod_map = (lambda i: (i // tiles_per_seq, 0, 0)) if n_mod > 1 else (lambda i: (0, 0, 0))
    tok = lambda wd: pl.BlockSpec((TOK_TILE, wd), lambda i: (i, 0))
    return pl.pallas_call(
        _post_kernel,
        grid=(n_tok // TOK_TILE,),
        in_specs=[tok(D_MODEL), pl.BlockSpec((1, 1, 6 * D_MODEL), mod_map),
                  tok(512), tok(512), tok(512), tok(512),
                  const((2, D_RWKV)), const((D_MODEL, D_MODEL)), const((1, D_MODEL)), const((1, D_MODEL))],
        out_specs=[tok(D_MODEL), tok(D_MODEL)],
        out_shape=[jax.ShapeDtypeStruct((n_tok, D_MODEL), F32), jax.ShapeDtypeStruct((n_tok, D_MODEL), BF16)],
        compiler_params=pltpu.CompilerParams(dimension_semantics=("arbitrary",),
                                             vmem_limit_bytes=VMEM_LIMIT),
        name="mix_post",
    )(x2d, mod3d, ys, bonus, g, attn, wts["gn"], wts["w_out"], wts["g_post"], wts["g_ffn_pre"])


def _ffn_kernel(h2_ref, x1_ref, mod_ref, wg_ref, wv_ref, cwg_ref, cwv_ref, cbg_ref, cbv_ref,
                wd_ref, gpost_ref, o_ref, acc_ref, *, seq_len):
    j = pl.program_id(1)

    @pl.when(j == 0)
    def _():
        acc_ref[...] = jnp.zeros(acc_ref.shape, F32)

    h2 = h2_ref[...]
    n_rows = h2.shape[0]
    assert seq_len & (seq_len - 1) == 0
    row = lax.broadcasted_iota(jnp.int32, (n_rows, FFN_SUB_TILE), 0) & (seq_len - 1)
    has_prev = row != 0
    has_next = row != seq_len - 1

    def conv(u, cw, cb):
        prev = jnp.where(has_prev, pltpu.roll(u, 1, 0), 0.0)
        nxt = jnp.where(has_next, pltpu.roll(u, n_rows - 1, 0), 0.0)
        return ((cb + prev * cw[0:1]) + u * cw[1:2]) + nxt * cw[2:3]

    down = None
    for c0 in range(0, FFN_COL_TILE, FFN_SUB_TILE):
        cols = slice(c0, c0 + FFN_SUB_TILE)
        gate = conv(_dot(h2, wg_ref[:, cols]), cwg_ref[:, cols], cbg_ref[:, cols])
        val = conv(_dot(h2, wv_ref[:, cols]), cwv_ref[:, cols], cbv_ref[:, cols])
        act = (gate * _sigmoid(gate) * val).astype(BF16)
        part = _dot(act, wd_ref[cols, :])
        down = part if down is None else down + part
    acc_ref[...] += down

    @pl.when(j == pl.num_programs(1) - 1)
    def _():
        m = mod_ref[0]
        gate2 = m[:, 5 * D_MODEL:6 * D_MODEL]
        o_ref[...] = x1_ref[...] + gate2 * _rms(acc_ref[...], gpost_ref[...])


def _ffn_call(h2, x1, mod3d, seq_len, wts):
    n_tok = h2.shape[0]
    tiles_per_seq = max(seq_len // FFN_TOK_TILE, 1)
    n_mod = mod3d.shape[0]
    n_col = D_FF_PAD // FFN_COL_TILE
    mod_map = (lambda i, j: (i // tiles_per_seq, 0, 0)) if n_mod > 1 else (lambda i, j: (0, 0, 0))
    tok = lambda: pl.BlockSpec((FFN_TOK_TILE, D_MODEL), lambda i, j: (i, 0))
    return pl.pallas_call(
        functools.partial(_ffn_kernel, seq_len=seq_len),
        grid=(n_tok // FFN_TOK_TILE, n_col),
        in_specs=[
            tok(), tok(), pl.BlockSpec((1, 1, 6 * D_MODEL), mod_map),
            pl.BlockSpec((D_MODEL, FFN_COL_TILE), lambda i, j: (0, j)),
            pl.BlockSpec((D_MODEL, FFN_COL_TILE), lambda i, j: (0, n_col + j)),
            pl.BlockSpec((3, FFN_COL_TILE), lambda i, j: (0, j)),
            pl.BlockSpec((3, FFN_COL_TILE), lambda i, j: (0, n_col + j)),
            pl.BlockSpec((1, FFN_COL_TILE), lambda i, j: (0, j)),
            pl.BlockSpec((1, FFN_COL_TILE), lambda i, j: (0, n_col + j)),
            pl.BlockSpec((FFN_COL_TILE, D_MODEL), lambda i, j: (j, 0)),
            pl.BlockSpec((1, D_MODEL), lambda i, j: (0, 0)),
        ],
        out_specs=tok(),
        out_shape=jax.ShapeDtypeStruct((n_tok, D_MODEL), F32),
        scratch_shapes=[pltpu.VMEM((FFN_TOK_TILE, D_MODEL), F32)],
        compiler_params=pltpu.CompilerParams(dimension_semantics=("arbitrary", "arbitrary"),
                                             vmem_limit_bytes=VMEM_LIMIT),
        name="conv_ffn",
    )(h2, x1, mod3d, wts["ffn_up"], wts["ffn_up"], wts["conv_w"], wts["conv_w"],
      wts["conv_b"], wts["conv_b"], wts["ffn_down"], wts["g_ffn_post"])


def _rope_tables(n_tok, n_heads):
    quarter = HEAD_DIM // 4
    pos = jnp.arange(n_tok)
    row = (pos // GRID_W).astype(F32)
    col = (pos % GRID_W).astype(F32)
    inv_freq = ROPE_THETA ** (-jnp.arange(quarter, dtype=F32) / quarter)

    def half(p):
        ang = p[:, None] * inv_freq[None, :]
        c, s = jnp.cos(ang), jnp.sin(ang)
        return jnp.concatenate([c, c], axis=-1), jnp.concatenate([-s, s], axis=-1)

    cr, sr = half(row)
    cc, sc = half(col)
    cos = jnp.concatenate([cr, cc], axis=-1)
    sin = jnp.concatenate([sr, sc], axis=-1)
    return jnp.tile(cos, (1, n_heads)), jnp.tile(sin, (1, n_heads))


def _pad_ff_cols(x):
    pad = ((0, 0), (0, D_FF_PAD - D_FF))
    return jnp.concatenate([jnp.pad(x[:, :D_FF], pad), jnp.pad(x[:, D_FF:], pad)], axis=1)


def _mixer(pre, n_batch, seq_len, packed, k_a, s0=None):
    sh, w, a = pre[:3]
    shared = _to_chains_call(sh, n_batch, seq_len, packed, n_split=4)
    ka_tab = jnp.repeat(k_a.reshape(H_RWKV, HEAD_DIM).T, n_batch, axis=1)
    ka_tab = jnp.tile(ka_tab, (1, LANES // ka_tab.shape[1]))
    y_f, y_b, s_fin = _scan_call(shared, _to_chains_call(w, n_batch, seq_len, False),
                                 _to_chains_call(a, n_batch, seq_len, False), ka_tab, packed, s0)
    return _from_chains_call(y_f, y_b, n_batch, seq_len, packed), s_fin


def kernel(x_prompt, x_sample, cache_k, cache_v, state_rwkv, c, c_ctx, w_mod, b_mod, norm_mix_pre, norm_mix_post, norm_ffn_pre, norm_ffn_post, w_in, w0, w_up, a0, a_up, g_up, k_k, k_a, r_k, gn_w, gn_b, q_norm, k_norm, w_out, ffn_up, conv_w, conv_b, ffn_down):
    n_ctx, l_ctx = x_prompt.shape[0], x_prompt.shape[1]
    n_lat, l_lat = x_sample.shape[0], x_sample.shape[1]
    l_past = cache_k.shape[2]
    layer = 0
    assert n_ctx * H_RWKV == LANES and 2 * n_lat * H_RWKV == LANES

    zeros_w = jnp.zeros((LORA_W, D_RWKV), F32)
    w_lora = jnp.concatenate([
        jnp.concatenate([w_up[layer, 0], w_up[layer, 1], zeros_w, zeros_w], axis=1),
        jnp.concatenate([zeros_w, zeros_w, a_up[layer, 0], a_up[layer, 1]], axis=1)], axis=0)
    vec = jnp.stack([k_k[layer], k_a[layer], r_k[layer].reshape(D_RWKV), jnp.tile(q_norm[layer], H_Q)])
    wts = dict(
        g_pre=norm_mix_pre[layer].reshape(1, D_MODEL),
        g_post=norm_mix_post[layer].reshape(1, D_MODEL),
        g_ffn_pre=norm_ffn_pre[layer].reshape(1, D_MODEL),
        g_ffn_post=norm_ffn_post[layer].reshape(1, D_MODEL),
        w_in=w_in[layer].astype(BF16),
        w_lora=w_lora.astype(BF16),
        g_up=g_up[layer].astype(BF16),
        vec=vec,
        wa0=jnp.stack([w0[layer].reshape(2 * D_RWKV), a0[layer].reshape(2 * D_RWKV)]),
        k_norm=jnp.tile(k_norm[layer], H_KV).reshape(1, D_KV),
        gn=jnp.stack([gn_w[layer], gn_b[layer]]),
        w_out=w_out[layer].astype(BF16),
        ffn_up=_pad_ff_cols(ffn_up[layer].astype(BF16)),
        conv_w=_pad_ff_cols(conv_w[layer]),
        conv_b=_pad_ff_cols(conv_b[layer].reshape(1, 2 * D_FF)),
        ffn_down=jnp.pad(ffn_down[layer].astype(BF16), ((0, D_FF_PAD - D_FF), (0, 0))),
    )

    cvec = jnp.concatenate([c, c_ctx[None, :], jnp.zeros((16 - n_lat - 1, D_MODEL), F32)], axis=0)
    mod = _mod_call(cvec, w_mod[layer], b_mod[layer])
    mod_lat = mod[:n_lat].reshape(n_lat, 1, 6 * D_MODEL)
    mod_ctx = mod[n_lat:n_lat + 1].reshape(1, 1, 6 * D_MODEL)

    xc = x_prompt.reshape(n_ctx * l_ctx, D_MODEL)
    pre_c = _pre_call(xc, mod_ctx, l_ctx, wts, None)
    g_c, bonus_c, q_c, ka_c, va_c = pre_c[3:]
    ys_c, s_c = _mixer(pre_c, n_ctx, l_ctx, False, k_a[layer])
    attn_c = _attn_call(q_c, ka_c, va_c, n_ctx, l_ctx, l_ctx)
    x1_c, h2_c = _post_call(xc, mod_ctx, l_ctx, ys_c, bonus_c, g_c, attn_c, wts)
    out_c = _ffn_call(h2_c, x1_c, mod_ctx, l_ctx, wts)

    xl = x_sample.reshape(n_lat * l_lat, D_MODEL)
    tabs = _rope_tables(l_lat, H_Q) + _rope_tables(l_lat, H_KV)
    pre_l = _pre_call(xl, mod_lat, l_lat, wts, tabs)
    g_l, bonus_l, q_l, kr_l, va_l = pre_l[3:]
    s0 = state_rwkv[:, layer].transpose(4, 3, 1, 2, 0).reshape(1, HEAD_DIM, HEAD_DIM, LANES)
    ys_l, _ = _mixer(pre_l, n_lat, l_lat, True, k_a[layer], s0)
    k_all = jnp.concatenate([kr_l.reshape(n_lat, l_lat, D_KV), cache_k[:, layer].reshape(n_lat, l_past, D_KV)],
                            axis=1).reshape(n_lat * (l_lat + l_past), D_KV)
    v_all = jnp.concatenate([va_l.reshape(n_lat, l_lat, D_KV), cache_v[:, layer].reshape(n_lat, l_past, D_KV)],
                            axis=1).reshape(n_lat * (l_lat + l_past), D_KV)
    attn_l = _attn_call(q_l, k_all, v_all, n_lat, l_lat, l_lat + l_past)
    x1_l, h2_l = _post_call(xl, mod_lat, l_lat, ys_l, bonus_l, g_l, attn_l, wts)
    out_l = _ffn_call(h2_l, x1_l, mod_lat, l_lat, wts)

    y_prompt = out_c.reshape(n_ctx, l_ctx, D_MODEL)
    y_sample = out_l.reshape(n_lat, l_lat, D_MODEL)
    new_cache_k = ka_c.reshape(n_ctx, 1, l_ctx, H_KV, HEAD_DIM)
    new_cache_v = va_c.reshape(n_ctx, 1, l_ctx, H_KV, HEAD_DIM)
    new_state = s_c.reshape(2, HEAD_DIM, HEAD_DIM, H_RWKV, n_ctx).transpose(4, 0, 3, 2, 1)[:, None]
    return (y_prompt, y_sample, new_cache_k, new_cache_v, new_state)
```
